```python
import jax, jax.numpy as jnp
from jax import lax
import numpy as np

D_MODEL = 1024
BATCH = 16
SEQ = 4096
DEPTH = 2

MEM_LEN = 256
ML_HEADS = 4
ML_DQK = 64
ML_DV = 128
ML_CONV = 4
ML_CHUNK = 64
ML_W = ML_HEADS * ML_DV
ML_QK_W = 2 * ML_HEADS * ML_DQK
RW_HEADS = 8
RW_DH = 64
RW_W = RW_HEADS * RW_DH
RW_DECAY_LORA = 64
RW_AAA_LORA = 64
RW_GATE_LORA = 128
RW_IN_W = 3 * RW_W + RW_DECAY_LORA + RW_AAA_LORA + RW_GATE_LORA
RW_SIZES = (RW_W, RW_W, RW_W, RW_DECAY_LORA, RW_AAA_LORA, RW_GATE_LORA)
RW_GN_EPS = 64e-5
CA_HEADS = 4
CA_DH = 128
CA_W = CA_HEADS * CA_DH
N_BRANCH = 3
IN_SIZES = (ML_QK_W, ML_W, ML_W, ML_HEADS, ML_HEADS, RW_IN_W, CA_W, N_BRANCH * D_MODEL)
D_IN = sum(IN_SIZES)
N_EXPERTS = 32
TOP_K = 4
D_FF = 1024
SWIGLU_LIMIT = 7.0
SWIGLU_ALPHA = 1.702
MOE_BLOCK = 128
DN_ALPHA = (2 * DEPTH) ** 0.25
DN_BETA = (8 * DEPTH) ** -0.25
LN_EPS = 1e-5

kernel_name = 'hybrid_mlstm_rwkv7_xattn_moe_deepnorm'


def split_cols(u, sizes):
    idx, acc = [], 0
    for s in sizes[:-1]:
        acc += s
        idx.append(acc)
    return jnp.split(u, idx, axis=-1)


def layer_norm(x, g, b, eps=LN_EPS):
    xf = x.astype(jnp.float32)
    mu = jnp.mean(xf, axis=-1, keepdims=True)
    var = jnp.mean(jnp.square(xf - mu), axis=-1, keepdims=True)
    return ((xf - mu) * lax.rsqrt(var + eps) * g + b).astype(x.dtype)


def head_norm(h, eps):
    hf = h.astype(jnp.float32)
    mu = jnp.mean(hf, axis=-1, keepdims=True)
    var = jnp.mean(jnp.square(hf - mu), axis=-1, keepdims=True)
    y = (hf - mu) * lax.rsqrt(var + eps)
    return y.reshape(*h.shape[:-2], h.shape[-2] * h.shape[-1])


def causal_dwconv(u, w, b):
    c = u.shape[-1]
    y = lax.conv_general_dilated(u, w[:, None, :].astype(u.dtype), window_strides=(1,),
                                 padding=[(w.shape[0] - 1, 0)],
                                 dimension_numbers=('NWC', 'WIO', 'NWC'),
                                 feature_group_count=c)
    return y + b


def token_shift(u, mu):
    u_prev = jnp.pad(u[:, :-1], ((0, 0), (1, 0), (0, 0)))
    return u + (u_prev - u) * mu


def mlstm_chunkwise(q, k, v, ig, lf):
    B, S, H, dk = q.shape
    L = ML_CHUNK
    nc = S // L

    def to_chunks(t):
        t = t.reshape(B, nc, L, H, *t.shape[3:])
        return jnp.moveaxis(t, (1, 3), (0, 2))

    causal = jnp.tril(jnp.ones((L, L), dtype=bool))

    def step(carry, inp):
        C, n, m = carry
        qc, kc, vc, ic, fc = inp
        bcum = jnp.cumsum(fc, axis=-1)
        dmat = bcum[..., :, None] - bcum[..., None, :] + ic[..., None, :]
        dmat = jnp.where(causal, dmat, -jnp.inf)
        m_inter = bcum + m[..., None]
        m_t = jnp.maximum(m_inter, jnp.max(dmat, axis=-1))
        w_intra = jnp.exp(dmat - m_t[..., None]) * jnp.einsum('bhtk,bhsk->bhts', qc, kc)
        s_inter = jnp.exp(m_inter - m_t)
        num = (s_inter[..., None] * jnp.einsum('bhvk,bhtk->bhtv', C, qc)
               + jnp.einsum('bhts,bhsv->bhtv', w_intra, vc))
        den = s_inter * jnp.einsum('bhk,bhtk->bht', n, qc) + jnp.sum(w_intra, axis=-1)
        h = num / jnp.maximum(jnp.abs(den), jnp.exp(-m_t))[..., None]
        b_last = bcum[..., -1]
        g = b_last[..., None] - bcum + ic
        m_new = jnp.maximum(b_last + m, jnp.max(g, axis=-1))
        carry_scale = jnp.exp(b_last + m - m_new)
        wg = jnp.exp(g - m_new[..., None])
        C_new = carry_scale[..., None, None] * C + jnp.einsum('bhs,bhsv,bhsk->bhvk', wg, vc, kc)
        n_new = carry_scale[..., None] * n + jnp.einsum('bhs,bhsk->bhk', wg, kc)
        return (C_new, n_new, m_new), h

    dv = v.shape[-1]
    init = (jnp.zeros((B, H, dv, dk), jnp.float32), jnp.zeros((B, H, dk), jnp.float32),
            jnp.zeros((B, H), jnp.float32))
    xs = (to_chunks(q), to_chunks(k), to_chunks(v), to_chunks(ig), to_chunks(lf))
    _, h = lax.scan(step, init, xs)
    return jnp.moveaxis(h, (0, 2), (1, 3)).reshape(B, S, H, dv)


def rwkv7_scan(r, w, k, v, kk, a):
    B, S, H, N = r.shape

    def step(state, inp):
        rt, wt, kt, vt, kkt, at = inp
        sa = jnp.einsum('bhvk,bhk->bhv', state, kkt)
        state = (state * wt[:, :, None, :] - sa[..., None] * (kkt * at)[:, :, None, :]
                 + vt[..., None] * kt[:, :, None, :])
        return state, jnp.einsum('bhvk,bhk->bhv', state, rt)

    xs = tuple(jnp.moveaxis(t, 1, 0) for t in (r, w, k, v, kk, a))
    _, out = lax.scan(step, jnp.zeros((B, H, N, N), jnp.float32), xs)
    return jnp.moveaxis(out, 0, 1)


def hybrid_mixer(x, mem_n, w_in, ml_conv_w, ml_conv_b, ml_ig_b, ml_fg_b, ml_norm_g,
                 rw_mu, rw_w0, rw_w_up, rw_a0, rw_a_up, rw_g_up, rw_kk, rw_ka, rw_rk,
                 rw_ln_g, rw_ln_b, ca_w_kv, gate_b, w_br_ml, w_br_rw, w_br_ca, w_o):
    B, S, _ = x.shape
    f32 = jnp.float32
    u = x @ w_in
    u_qk, ml_v, ml_og, ml_ig, ml_fg, u_rw, ca_q, u_gate = split_cols(u, IN_SIZES)

    qk = jax.nn.silu(causal_dwconv(u_qk, ml_conv_w, ml_conv_b))
    q, k = jnp.split(qk, 2, axis=-1)
    q = q.reshape(B, S, ML_HEADS, ML_DQK).astype(f32) * (ML_DQK ** -0.5)
    k = k.reshape(B, S, ML_HEADS, ML_DQK).astype(f32)
    v = ml_v.reshape(B, S, ML_HEADS, ML_DV).astype(f32)
    ig = (ml_ig + ml_ig_b).astype(f32)
    lf = jax.nn.log_sigmoid((ml_fg + ml_fg_b).astype(f32))
    h = mlstm_chunkwise(q, k, v, ig, lf)
    h_ml = jax.nn.sigmoid(ml_og) * (head_norm(h, LN_EPS) * ml_norm_g).astype(x.dtype)

    u_rw = token_shift(u_rw, rw_mu)
    r, kr, vr, wd, ad, gd = split_cols(u_rw, RW_SIZES)
    w_log = -jax.nn.softplus(-(rw_w0 + jnp.tanh(wd) @ rw_w_up)) - 0.5
    decay = jnp.exp(-jnp.exp(w_log.astype(f32)))
    a = jax.nn.sigmoid(rw_a0 + ad @ rw_a_up)
    g = jax.nn.sigmoid(gd) @ rw_g_up

    def heads(t):
        return t.reshape(B, S, RW_HEADS, RW_DH).astype(f32)

    kk = heads(kr * rw_kk)
    kk = kk * lax.rsqrt(jnp.maximum(jnp.sum(jnp.square(kk), axis=-1, keepdims=True), 1e-24))
    kr = kr * (1.0 + (a - 1.0) * rw_ka)
    rh, kh, vh, ah, wh = heads(r), heads(kr), heads(vr), heads(a), heads(decay)
    o = rwkv7_scan(rh, wh, kh, vh, kk, ah)
    bonus = (jnp.sum(rh * kh * rw_rk, axis=-1, keepdims=True) * vh).reshape(B, S, RW_W)
    h_rw = ((head_norm(o, RW_GN_EPS) * rw_ln_g + rw_ln_b) + bonus).astype(x.dtype) * g

    km, vm = jnp.split(mem_n @ ca_w_kv, 2, axis=-1)
    km = km.reshape(B, MEM_LEN, CA_HEADS, CA_DH).astype(f32)
    vm = vm.reshape(B, MEM_LEN, CA_HEADS, CA_DH).astype(f32)
    qc = ca_q.reshape(B, S, CA_HEADS, CA_DH).astype(f32)
    p = jax.nn.softmax(jnp.einsum('bshd,bmhd->bhsm', qc, km) * (CA_DH ** -0.5), axis=-1)
    h_ca = jnp.einsum('bhsm,bmhd->bshd', p, vm).reshape(B, S, CA_W).astype(x.dtype)

    gates = jax.nn.sigmoid(u_gate + gate_b).reshape(B, S, N_BRANCH, D_MODEL)
    y = (gates[:, :, 0] * (h_ml @ w_br_ml) + gates[:, :, 1] * (h_rw @ w_br_rw)
         + gates[:, :, 2] * (h_ca @ w_br_ca))
    return y @ w_o


def clamped_swiglu(h):
    x_glu = jnp.minimum(h[..., ::2], SWIGLU_LIMIT)
    x_lin = jnp.clip(h[..., 1::2], -SWIGLU_LIMIT, SWIGLU_LIMIT)
    return x_glu * jax.nn.sigmoid(SWIGLU_ALPHA * x_glu) * (x_lin + 1.0)


def moe_ffn(x, router_w, router_b, w_gu, b_gu, w_dn, b_dn):
    B, S, D = x.shape
    n_tok = B * S
    xt = x.reshape(n_tok, D)
    logits = (xt @ router_w + router_b).astype(jnp.float32)
    top_val, top_idx = lax.top_k(logits, TOP_K)
    gate = jax.nn.softmax(top_val, axis=-1)
    n_asg = n_tok * TOP_K
    e_flat = top_idx.reshape(n_asg).astype(jnp.int32)
    tok_flat = jnp.arange(n_asg, dtype=jnp.int32) // TOP_K
    counts = jax.ops.segment_sum(jnp.ones((n_asg,), jnp.int32), e_flat, num_segments=N_EXPERTS)
    padded = (counts + MOE_BLOCK - 1) // MOE_BLOCK * MOE_BLOCK
    pad_end = jnp.cumsum(padded)
    pad_start = pad_end - padded
    start = jnp.cumsum(counts) - counts
    order = jnp.argsort(e_flat)
    e_sorted = e_flat[order]
    dest_sorted = pad_start[e_sorted] + (jnp.arange(n_asg, dtype=jnp.int32) - start[e_sorted])
    n_blocks = -(-n_asg // MOE_BLOCK) + N_EXPERTS
    n_rows = n_blocks * MOE_BLOCK
    disp_tok = jnp.full((n_rows,), n_tok, jnp.int32).at[dest_sorted].set(tok_flat[order])
    x_pad = jnp.concatenate([xt, jnp.zeros((1, D), xt.dtype)], axis=0)
    x_disp = x_pad[disp_tok].reshape(n_blocks, MOE_BLOCK, D)
    blk_start = jnp.arange(n_blocks, dtype=jnp.int32) * MOE_BLOCK
    blk_e = jnp.minimum(jnp.searchsorted(pad_end, blk_start, side='right'), N_EXPERTS - 1).astype(jnp.int32)

    def expert_block(args):
        xb, e = args
        h = clamped_swiglu(xb @ w_gu[e] + b_gu[e])
        return h @ w_dn[e] + b_dn[e]

    y_disp = lax.map(expert_block, (x_disp, blk_e)).reshape(n_rows, D)
    dest = jnp.zeros((n_asg,), jnp.int32).at[order].set(dest_sorted)
    y = y_disp[dest].reshape(n_tok, TOP_K, D)
    out = jnp.einsum('nkd,nk->nd', y, gate.astype(y.dtype))
    return out.reshape(B, S, D)


def setup_inputs(seed: int = 0) -> dict:
    key = jax.random.key(seed)
    ks = iter(jax.random.split(key, 48))
    f32 = jnp.float32
    L, D = DEPTH, D_MODEL

    def nrm(shape, scale):
        return jax.random.normal(next(ks), shape, f32) * scale

    def gain(shape):
        return 1.0 + nrm(shape, 0.02)

    return {
        'x': nrm((BATCH, SEQ, D), 1.0),
        'mem': nrm((BATCH, MEM_LEN, D), 1.0),
        'ln_in_g': gain((D,)),
        'ln_in_b': nrm((D,), 0.02),
        'mem_ln_g': gain((D,)),
        'mem_ln_b': nrm((D,), 0.02),
        'w_in': nrm((L, D, D_IN), D ** -0.5),
        'ml_conv_w': nrm((L, ML_CONV, ML_QK_W), ML_CONV ** -0.5),
        'ml_conv_b': nrm((L, ML_QK_W), 0.02),
        'ml_ig_b': nrm((L, ML_HEADS), 0.1),
        'ml_fg_b': jnp.linspace(3.0, 6.0, ML_HEADS, dtype=f32)[None] + nrm((L, ML_HEADS), 0.1),
        'ml_norm_g': gain((L, ML_W)),
        'rw_mu': jax.random.uniform(next(ks), (L, RW_IN_W), f32, 0.1, 0.9),
        'rw_w0': jnp.linspace(-6.5, -1.5, RW_W, dtype=f32)[None] + nrm((L, RW_W), 0.1),
        'rw_w_up': nrm((L, RW_DECAY_LORA, RW_W), 0.5 * RW_DECAY_LORA ** -0.5),
        'rw_a0': nrm((L, RW_W), 0.1),
        'rw_a_up': nrm((L, RW_AAA_LORA, RW_W), 0.5 * RW_AAA_LORA ** -0.5),
        'rw_g_up': nrm((L, RW_GATE_LORA, RW_W), RW_GATE_LORA ** -0.5),
        'rw_kk': 0.85 + nrm((L, RW_W), 0.05),
        'rw_ka': 1.0 + nrm((L, RW_W), 0.05),
        'rw_rk': nrm((L, RW_HEADS, RW_DH), 0.1),
        'rw_ln_g': gain((L, RW_W)),
        'rw_ln_b': nrm((L, RW_W), 0.02),
        'ca_w_kv': nrm((L, D, 2 * CA_W), D ** -0.5),
        'gate_b': nrm((L, N_BRANCH * D), 0.1),
        'w_br_ml': nrm((L, ML_W, D), DN_BETA * ML_W ** -0.5),
        'w_br_rw': nrm((L, RW_W, D), DN_BETA * RW_W ** -0.5),
        'w_br_ca': nrm((L, CA_W, D), DN_BETA * CA_W ** -0.5),
        'w_o': nrm((L, D, D), DN_BETA * D ** -0.5),
        'ln1_g': gain((L, D)),
        'ln1_b': nrm((L, D), 0.02),
        'router_w': nrm((L, D, N_EXPERTS), D ** -0.5),
        'router_b': nrm((L, N_EXPERTS), 0.01),
        'w_gu': nrm((L, N_EXPERTS, D, 2 * D_FF), D ** -0.5),
        'b_gu': nrm((L, N_EXPERTS, 2 * D_FF), 0.02),
        'w_dn': nrm((L, N_EXPERTS, D_FF, D), DN_BETA * D_FF ** -0.5),
        'b_dn': nrm((L, N_EXPERTS, D), 0.02),
        'ln2_g': gain((L, D)),
        'ln2_b': nrm((L, D), 0.02),
    }


def reference(x, mem, ln_in_g, ln_in_b, mem_ln_g, mem_ln_b, w_in, ml_conv_w, ml_conv_b,
              ml_ig_b, ml_fg_b, ml_norm_g, rw_mu, rw_w0, rw_w_up, rw_a0, rw_a_up, rw_g_up,
              rw_kk, rw_ka, rw_rk, rw_ln_g, rw_ln_b, ca_w_kv, gate_b, w_br_ml, w_br_rw,
              w_br_ca, w_o, ln1_g, ln1_b, router_w, router_b, w_gu, b_gu, w_dn, b_dn,
              ln2_g, ln2_b):
    x = layer_norm(x, ln_in_g, ln_in_b)
    mem_n = layer_norm(mem, mem_ln_g, mem_ln_b)
    for l in range(DEPTH):
        y = hybrid_mixer(x, mem_n, w_in[l], ml_conv_w[l], ml_conv_b[l], ml_ig_b[l], ml_fg_b[l],
                         ml_norm_g[l], rw_mu[l], rw_w0[l], rw_w_up[l], rw_a0[l], rw_a_up[l],
                         rw_g_up[l], rw_kk[l], rw_ka[l], rw_rk[l], rw_ln_g[l], rw_ln_b[l],
                         ca_w_kv[l], gate_b[l], w_br_ml[l], w_br_rw[l], w_br_ca[l], w_o[l])
        x = layer_norm(DN_ALPHA * x + y, ln1_g[l], ln1_b[l])
        y = moe_ffn(x, router_w[l], router_b[l], w_gu[l], b_gu[l], w_dn[l], b_dn[l])
        x = layer_norm(DN_ALPHA * x + y, ln2_g[l], ln2_b[l])
    return x
```

```python
import functools

import jax
import jax.numpy as jnp
from jax import lax
from jax.experimental import pallas as pl
from jax.experimental.pallas import tpu as pltpu

ML_HEADS, ML_DQK, ML_DV, ML_CONV = 4, 64, 128, 4
ML_W = ML_HEADS * ML_DV
ML_QK_W = 2 * ML_HEADS * ML_DQK
RW_HEADS, RW_DH = 8, 64
RW_W = RW_HEADS * RW_DH
RW_DECAY_LORA, RW_AAA_LORA, RW_GATE_LORA = 64, 64, 128
RW_GN_EPS = 64e-5
CA_HEADS, CA_DH = 4, 128
CA_W = CA_HEADS * CA_DH
N_BRANCH = 3
N_EXPERTS, TOP_K = 32, 4
SWIGLU_LIMIT, SWIGLU_ALPHA = 7.0, 1.702
LN_EPS = 1e-5

LANES = 128
SUBLANES = 8
VMEM_LIMIT = 56 * 1024 * 1024

BF16 = jnp.bfloat16
F32 = jnp.float32
NEG_BIG = -1e30


def _dot(a, b):
    return jnp.dot(a.astype(BF16), b.astype(BF16), preferred_element_type=F32)


def _split3(a):
    hi = a.astype(BF16)
    r1 = a - hi.astype(F32)
    mid = r1.astype(BF16)
    lo = (r1 - mid.astype(F32)).astype(BF16)
    return hi, mid, lo


def _dot_exact_rhs(a, b_bf16):
    hi, mid, lo = _split3(a)
    d = functools.partial(jnp.dot, preferred_element_type=F32)
    return d(hi, b_bf16) + d(mid, b_bf16) + d(lo, b_bf16)


def _dot_exact_lhs(a_bf16, b):
    hi, mid, lo = _split3(b)
    d = functools.partial(jnp.dot, preferred_element_type=F32)
    return d(a_bf16, hi) + d(a_bf16, mid) + d(a_bf16, lo)


def _sigmoid(x):
    return 1.0 / (1.0 + jnp.exp(-x))


def _softplus(x):
    return jnp.maximum(x, 0.0) + jnp.log1p(jnp.exp(-jnp.abs(x)))


def _layer_norm(v, g, b, eps=LN_EPS):
    mu = jnp.mean(v, axis=-1, keepdims=True)
    c = v - mu
    var = jnp.mean(c * c, axis=-1, keepdims=True)
    return c * lax.rsqrt(var + eps) * g + b


def _params(*sem):
    return pltpu.CompilerParams(dimension_semantics=sem, vmem_limit_bytes=VMEM_LIMIT)


def _const_spec(shape):
    nd = len(shape)
    return pl.BlockSpec(shape, lambda *_: (0,) * nd)


def _ln_kernel(x_ref, g_ref, b_ref, o_ref, ob_ref):
    y = _layer_norm(x_ref[...], g_ref[...], b_ref[...])
    o_ref[...] = y
    ob_ref[...] = y.astype(BF16)


def _ln_rows(x2d, g, b, tile):
    n, d = x2d.shape
    return pl.pallas_call(
        _ln_kernel,
        grid=(n // tile,),
        in_specs=[pl.BlockSpec((tile, d), lambda i: (i, 0)), _const_spec((1, d)), _const_spec((1, d))],
        out_specs=[pl.BlockSpec((tile, d), lambda i: (i, 0)), pl.BlockSpec((tile, d), lambda i: (i, 0))],
        out_shape=[jax.ShapeDtypeStruct((n, d), F32), jax.ShapeDtypeStruct((n, d), BF16)],
        compiler_params=_params("parallel"),
        name="ln_rows",
    )(x2d, g.reshape(1, d), b.reshape(1, d))


def _mm_kernel(a_ref, b_ref, o_ref):
    o_ref[...] = jnp.dot(a_ref[...], b_ref[...], preferred_element_type=F32)


def _matmul(a, b, tile):
    m, k = a.shape
    n = b.shape[1]
    return pl.pallas_call(
        _mm_kernel,
        grid=(m // tile,),
        in_specs=[pl.BlockSpec((tile, k), lambda i: (i, 0)), _const_spec((k, n))],
        out_specs=pl.BlockSpec((tile, n), lambda i: (i, 0)),
        out_shape=jax.ShapeDtypeStruct((m, n), F32),
        compiler_params=_params("parallel"),
        name="matmul",
    )(a, b)


def _mlstm_kernel(xb_ref, wml_ref, wg_ref, convw_ref, convb_ref, gb_ref, normg_ref, tri_ref,
                  h_ref, ubuf, c_ref, n_ref, m_ref, *, tile, chunk):
    @pl.when(pl.program_id(1) == 0)
    def _():
        ubuf[0:SUBLANES, :] = jnp.zeros((SUBLANES, ML_QK_W), F32)
        c_ref[...] = jnp.zeros_like(c_ref)
        n_ref[...] = jnp.zeros_like(n_ref)
        m_ref[...] = jnp.zeros_like(m_ref)

    xb = xb_ref[...]
    u = jnp.dot(xb, wml_ref[...], preferred_element_type=F32)
    gates = jnp.dot(xb, wg_ref[...], preferred_element_type=F32) + gb_ref[...]

    ubuf[SUBLANES:SUBLANES + tile, :] = u[:, :ML_QK_W]
    acc = jnp.broadcast_to(convb_ref[...], (tile, ML_QK_W))
    for j in range(ML_CONV):
        acc = acc + convw_ref[j:j + 1, :] * ubuf[pl.ds(SUBLANES - ML_CONV + 1 + j, tile), :]
    ubuf[0:SUBLANES, :] = ubuf[tile:tile + SUBLANES, :]
    qk = acc * _sigmoid(acc)
    q = qk[:, :ML_HEADS * ML_DQK] * (ML_DQK ** -0.5)
    k = qk[:, ML_HEADS * ML_DQK:]
    v = u[:, ML_QK_W:ML_QK_W + ML_W]
    og = u[:, ML_QK_W + ML_W:]

    lane = lax.broadcasted_iota(jnp.int32, (chunk, LANES), 1)
    row = lax.broadcasted_iota(jnp.int32, (chunk, chunk), 0)
    col = lax.broadcasted_iota(jnp.int32, (chunk, chunk), 1)
    causal = row >= col
    log_f = -_softplus(-gates)

    for c in range(tile // chunk):
        rs = slice(c * chunk, (c + 1) * chunk)
        bcum = _dot_exact_lhs(tri_ref[...], log_f[rs])
        mcol = jnp.where(lane < ML_HEADS, gates[rs], bcum)
        mrow = mcol.T
        for h in range(ML_HEADS):
            i_col = mcol[:, h:h + 1]
            b_col = mcol[:, ML_HEADS + h:ML_HEADS + h + 1]
            i_row = mrow[h:h + 1, :]
            b_row = mrow[ML_HEADS + h:ML_HEADS + h + 1, :]
            m_prev = m_ref[h][0:1, 0:1]
            n_prev = n_ref[h][0:1, :]
            c_prev = c_ref[h]
            qh = q[rs, h * ML_DQK:(h + 1) * ML_DQK]
            kh = k[rs, h * ML_DQK:(h + 1) * ML_DQK]
            vh = v[rs, h * ML_DV:(h + 1) * ML_DV]

            dmat = jnp.where(causal, b_col - b_row + i_row, -jnp.inf)
            m_inter = b_col + m_prev
            m_t = jnp.maximum(m_inter, jnp.max(dmat, axis=1, keepdims=True))
            s = lax.dot_general(qh.astype(BF16), kh.astype(BF16), (((1,), (1,)), ((), ())),
                                preferred_element_type=F32)
            w_intra = jnp.exp(dmat - m_t) * s
            s_inter = jnp.exp(m_inter - m_t)
            num = s_inter * _dot(qh, c_prev) + _dot(w_intra, vh)
            den = (s_inter * jnp.sum(qh * n_prev, axis=1, keepdims=True)
                   + jnp.sum(w_intra, axis=1, keepdims=True))
            hh = num / jnp.maximum(jnp.abs(den), jnp.exp(-m_t))

            b_last = b_col[chunk - 1:chunk, :]
            g_col = b_last - b_col + i_col
            m_new = jnp.maximum(b_last + m_prev, jnp.max(g_col, axis=0, keepdims=True))
            carry = jnp.exp(b_last + m_prev - m_new)
            kw = kh * jnp.exp(g_col - m_new)
            c_ref[h] = carry * c_prev + lax.dot_general(
                kw.astype(BF16), vh.astype(BF16), (((0,), (0,)), ((), ())), preferred_element_type=F32)
            n_new = carry * n_prev + jnp.sum(kw, axis=0, keepdims=True)
            n_ref[h] = jnp.broadcast_to(n_new, (SUBLANES, ML_DQK))
            m_ref[h] = jnp.broadcast_to(m_new, (SUBLANES, LANES))

            mu = jnp.mean(hh, axis=1, keepdims=True)
            cen = hh - mu
            var = jnp.mean(cen * cen, axis=1, keepdims=True)
            y = cen * lax.rsqrt(var + LN_EPS) * normg_ref[:, h * ML_DV:(h + 1) * ML_DV]
            h_ref[rs, h * ML_DV:(h + 1) * ML_DV] = (
                _sigmoid(og[rs, h * ML_DV:(h + 1) * ML_DV]) * y).astype(BF16)


def _mlstm(xb, wml, wg, convw, convb, gb, normg, tile, chunk):
    bsz, s, d = xb.shape
    tri = (jnp.arange(chunk)[:, None] >= jnp.arange(chunk)[None, :]).astype(BF16)
    kern = functools.partial(_mlstm_kernel, tile=tile, chunk=chunk)
    return pl.pallas_call(
        kern,
        grid=(bsz, s // tile),
        in_specs=[pl.BlockSpec((None, tile, d), lambda b, j: (b, j, 0)),
                  _const_spec(wml.shape), _const_spec(wg.shape), _const_spec(convw.shape),
                  _const_spec(convb.shape), _const_spec(gb.shape), _const_spec(normg.shape),
                  _const_spec(tri.shape)],
        out_specs=pl.BlockSpec((None, tile, ML_W), lambda b, j: (b, j, 0)),
        out_shape=jax.ShapeDtypeStruct((bsz, s, ML_W), BF16),
        scratch_shapes=[pltpu.VMEM((tile + SUBLANES, ML_QK_W), F32),
                        pltpu.VMEM((ML_HEADS, ML_DQK, ML_DV), F32),
                        pltpu.VMEM((ML_HEADS, SUBLANES, ML_DQK), F32),
                        pltpu.VMEM((ML_HEADS, SUBLANES, LANES), F32)],
        compiler_params=_params("parallel", "arbitrary"),
        name="mlstm",
    )(xb, wml, wg, convw, convb, gb, normg, tri)


RW_PAD_IN = 3 * RW_W + 3 * LANES


def _rwkv_prep_kernel(xb_ref, wrw_ref, mu_ref, w0_ref, wup_ref, a0_ref, aup_ref, gup_ref, kkw_ref,
                      ka_ref, rk_ref, bd_ref,
                      r_ref, w_ref, k_ref, v_ref, kk_ref, kka_ref, g_ref, bonus_ref, ubuf, *, tile):
    @pl.when(pl.program_id(1) == 0)
    def _():
        ubuf[0:SUBLANES, :] = jnp.zeros((SUBLANES, RW_PAD_IN), F32)

    u = jnp.dot(xb_ref[...], wrw_ref[...], preferred_element_type=F32)
    ubuf[SUBLANES:SUBLANES + tile, :] = u
    u_prev = ubuf[pl.ds(SUBLANES - 1, tile), :]
    ubuf[0:SUBLANES, :] = ubuf[tile:tile + SUBLANES, :]
    us = u + (u_prev - u) * mu_ref[...]
    r = us[:, 0:RW_W]
    kr = us[:, RW_W:2 * RW_W]
    vr = us[:, 2 * RW_W:3 * RW_W]
    wd = us[:, 3 * RW_W:3 * RW_W + LANES]
    ad = us[:, 3 * RW_W + LANES:3 * RW_W + 2 * LANES]
    gd = us[:, 3 * RW_W + 2 * LANES:]

    w_log = -_softplus(-(w0_ref[...] + _dot(jnp.tanh(wd), wup_ref[...]))) - 0.5
    decay = jnp.exp(-jnp.exp(w_log))
    a = _sigmoid(a0_ref[...] + _dot(ad, aup_ref[...]))
    g = _dot(_sigmoid(gd), gup_ref[...])

    bd = bd_ref[...]
    kk = kr * kkw_ref[...]
    kk = kk * lax.rsqrt(jnp.maximum(_dot_exact_rhs(kk * kk, bd), 1e-24))
    k2 = kr * (1.0 + (a - 1.0) * ka_ref[...])
    bonus = _dot_exact_rhs(r * k2 * rk_ref[...], bd) * vr

    r_ref[...] = r
    w_ref[...] = decay
    k_ref[...] = k2
    v_ref[...] = vr
    kk_ref[...] = kk
    kka_ref[...] = kk * a
    g_ref[...] = g
    bonus_ref[...] = bonus


def _rwkv_prep(xb, wrw, mu, w0, wup, a0, aup, gup, kkw, ka, rk, bd, tile):
    bsz, s, d = xb.shape
    consts = (wrw, mu, w0, wup, a0, aup, gup, kkw, ka, rk, bd)
    spec = pl.BlockSpec((None, tile, RW_W), lambda b, j: (b, j, 0))
    return pl.pallas_call(
        functools.partial(_rwkv_prep_kernel, tile=tile),
        grid=(bsz, s // tile),
        in_specs=[pl.BlockSpec((None, tile, d), lambda b, j: (b, j, 0))] + [_const_spec(c.shape) for c in consts],
        out_specs=[spec] * 8,
        out_shape=[jax.ShapeDtypeStruct((bsz, s, RW_W), F32)] * 8,
        scratch_shapes=[pltpu.VMEM((tile + SUBLANES, RW_PAD_IN), F32)],
        compiler_params=_params("parallel", "arbitrary"),
        name="rwkv_prep",
    )(xb, *consts)


def _rwkv_scan_kernel(r_ref, w_ref, k_ref, v_ref, kk_ref, kka_ref, o_ref, st_ref, *, steps):
    @pl.when(pl.program_id(1) == 0)
    def _():
        st_ref[...] = jnp.zeros_like(st_ref)

    def step(t, carry):
        sa = jnp.zeros((RW_DH, LANES), F32)
        for kc in range(RW_DH):
            sa = sa + st_ref[kc] * kk_ref[t, kc:kc + 1, :]
        vt = v_ref[t]
        out = jnp.zeros((RW_DH, LANES), F32)
        for kc in range(RW_DH):
            new = (st_ref[kc] * w_ref[t, kc:kc + 1, :] - sa * kka_ref[t, kc:kc + 1, :]
                   + vt * k_ref[t, kc:kc + 1, :])
            st_ref[kc] = new
            out = out + new * r_ref[t, kc:kc + 1, :]
        o_ref[t] = out
        return carry

    lax.fori_loop(0, steps, step, 0)


def _rwkv_scan(r, w, k, v, kk, kka, steps):
    s, dh, nl = r.shape
    spec = pl.BlockSpec((steps, dh, LANES), lambda g, j: (j, 0, g))
    return pl.pallas_call(
        functools.partial(_rwkv_scan_kernel, steps=steps),
        grid=(nl // LANES, s // steps),
        in_specs=[spec] * 6,
        out_specs=spec,
        out_shape=jax.ShapeDtypeStruct((s, dh, nl), F32),
        scratch_shapes=[pltpu.VMEM((dh, dh, LANES), F32)],
        compiler_params=_params("parallel", "arbitrary"),
        name="rwkv_scan",
    )(r, w, k, v, kk, kka)


def _xattn_kernel(xb_ref, wq_ref, kt_ref, v_ref, o_ref):
    q = jnp.dot(xb_ref[...], wq_ref[...], preferred_element_type=F32)
    for h in range(CA_HEADS):
        hs = slice(h * CA_DH, (h + 1) * CA_DH)
        s = jnp.dot(q[:, hs].astype(BF16), kt_ref[hs, :], preferred_element_type=F32) * (CA_DH ** -0.5)
        p = jnp.exp(s - jnp.max(s, axis=1, keepdims=True))
        den = jnp.sum(p, axis=1, keepdims=True)
        o = jnp.dot(p.astype(BF16), v_ref[:, hs], preferred_element_type=F32) / den
        o_ref[:, hs] = o.astype(BF16)


def _xattn(xb, wq, kt, v, tile):
    bsz, s, d = xb.shape
    m = v.shape[1]
    return pl.pallas_call(
        _xattn_kernel,
        grid=(bsz, s // tile),
        in_specs=[pl.BlockSpec((None, tile, d), lambda b, j: (b, j, 0)), _const_spec(wq.shape),
                  pl.BlockSpec((None, CA_W, m), lambda b, j: (b, 0, 0)),
                  pl.BlockSpec((None, m, CA_W), lambda b, j: (b, 0, 0))],
        out_specs=pl.BlockSpec((None, tile, CA_W), lambda b, j: (b, j, 0)),
        out_shape=jax.ShapeDtypeStruct((bsz, s, CA_W), BF16),
        compiler_params=_params("parallel", "parallel"),
        name="xattn",
    )(xb, wq, kt, v)


ROUTE_IDX, ROUTE_RANK, ROUTE_GATE = 0, TOP_K, 2 * TOP_K


def _merge_kernel(x_ref, xb_ref, hml_ref, o_ref, bonus_ref, g_ref, hca_ref,
                  wgate_ref, gateb_ref, bd_ref, rwg_ref, rwb_ref, wml_ref, wrw_ref, wca_ref, wo_ref,
                  lng_ref, lnb_ref, rwhi_ref, rwlo_ref, rb_ref, tri_ref,
                  x1_ref, x1b_ref, route_ref, cnt_ref, carry_ref, *, tile, alpha, d_model):
    @pl.when(pl.program_id(0) == 0)
    def _():
        carry_ref[...] = jnp.zeros_like(carry_ref)

    bd = bd_ref[...]
    o = o_ref[...]
    mu = _dot_exact_rhs(o, bd) * (1.0 / RW_DH)
    cen = o - mu
    var = _dot_exact_rhs(cen * cen, bd) * (1.0 / RW_DH)
    h_rw = (cen * lax.rsqrt(var + RW_GN_EPS) * rwg_ref[...] + rwb_ref[...] + bonus_ref[...]) * g_ref[...]

    gates = _sigmoid(jnp.dot(xb_ref[...], wgate_ref[...], preferred_element_type=F32) + gateb_ref[...])
    y = (gates[:, 0:d_model] * jnp.dot(hml_ref[...], wml_ref[...], preferred_element_type=F32)
         + gates[:, d_model:2 * d_model] * _dot(h_rw, wrw_ref[...])
         + gates[:, 2 * d_model:] * jnp.dot(hca_ref[...], wca_ref[...], preferred_element_type=F32))
    mixed = _dot(y, wo_ref[...])
    x1 = _layer_norm(alpha * x_ref[...] + mixed, lng_ref[...], lnb_ref[...])
    x1_ref[...] = x1
    x1b_ref[...] = x1.astype(BF16)

    hi = x1.astype(BF16)
    lo = (x1 - hi.astype(F32)).astype(BF16)
    d = functools.partial(jnp.dot, preferred_element_type=F32)
    logits = d(hi, rwhi_ref[...]) + d(lo, rwhi_ref[...]) + d(hi, rwlo_ref[...]) + rb_ref[...]

    lane = lax.broadcasted_iota(jnp.int32, (tile, LANES), 1)
    vals = logits
    tops, onehots = [], []
    for _ in range(TOP_K):
        m = jnp.max(vals, axis=1, keepdims=True)
        idx = jnp.min(jnp.where(vals == m, lane, LANES), axis=1, keepdims=True)
        sel = lane == idx
        tops.append((m, idx))
        onehots.append(sel.astype(F32))
        vals = jnp.where(sel, -jnp.inf, vals)
    exps = [jnp.exp(m - tops[0][0]) for m, _ in tops]
    den = exps[0] + exps[1] + exps[2] + exps[3]

    cnt = onehots[0] + onehots[1] + onehots[2] + onehots[3]
    before = jnp.dot(tri_ref[...], cnt.astype(BF16), preferred_element_type=F32) + carry_ref[0:1, :]
    route = jnp.zeros((tile, LANES), F32)
    for kq in range(TOP_K):
        rank = jnp.sum(onehots[kq] * before, axis=1, keepdims=True)
        route = jnp.where(lane == ROUTE_IDX + kq, tops[kq][1].astype(F32), route)
        route = jnp.where(lane == ROUTE_RANK + kq, rank, route)
        route = jnp.where(lane == ROUTE_GATE + kq, exps[kq] / den, route)
    route_ref[...] = route
    total = carry_ref[0:1, :] + jnp.sum(cnt, axis=0, keepdims=True)
    carry_ref[...] = jnp.broadcast_to(total, carry_ref.shape)
    cnt_ref[...] = jnp.broadcast_to(total, cnt_ref.shape)


def _merge(x, xb, hml, o, bonus, g, hca, consts, tile, alpha):
    n, d = x.shape
    tri = (jnp.arange(tile)[:, None] > jnp.arange(tile)[None, :]).astype(BF16)
    consts = tuple(consts) + (tri,)

    def rows(w):
        return pl.BlockSpec((tile, w), lambda i: (i, 0))

    return pl.pallas_call(
        functools.partial(_merge_kernel, tile=tile, alpha=alpha, d_model=d),
        grid=(n // tile,),
        in_specs=[rows(d), rows(d), rows(ML_W), rows(RW_W), rows(RW_W), rows(RW_W), rows(CA_W)]
        + [_const_spec(c.shape) for c in consts],
        out_specs=[rows(d), rows(d), rows(LANES), _const_spec((SUBLANES, LANES))],
        out_shape=[jax.ShapeDtypeStruct((n, d), F32), jax.ShapeDtypeStruct((n, d), BF16),
                   jax.ShapeDtypeStruct((n, LANES), F32), jax.ShapeDtypeStruct((SUBLANES, LANES), F32)],
        scratch_shapes=[pltpu.VMEM((SUBLANES, LANES), F32)],
        compiler_params=_params("arbitrary"),
        name="merge_route",
    )(x, xb, hml, o, bonus, g, hca, *consts)


def _row_copy(src_ref, src_row, dst_ref, dst_row, sem):
    return pltpu.make_async_copy(src_ref.at[pl.ds(src_row, 1)], dst_ref.at[pl.ds(dst_row, 1)], sem)


def _dispatch_kernel(dest_ref, x_ref, xd_in_ref, xd_ref, sem, *, tile):
    del xd_in_ref

    def issue(i, carry):
        for kq in range(TOP_K):
            _row_copy(x_ref, i, xd_ref, dest_ref[i * TOP_K + kq], sem).start()
        return carry

    lax.fori_loop(0, tile, issue, 0)
    for _ in range(TOP_K):
        pltpu.make_async_copy(x_ref, xd_ref.at[pl.ds(0, tile)], sem).wait()


def _dispatch(dest_flat, x1, xd_zero, tile):
    n, d = x1.shape
    return pl.pallas_call(
        functools.partial(_dispatch_kernel, tile=tile),
        grid=(n // tile,),
        in_specs=[pl.BlockSpec((tile * TOP_K,), lambda i: (i,), memory_space=pltpu.SMEM),
                  pl.BlockSpec((tile, d), lambda i: (i, 0)),
                  pl.BlockSpec(memory_space=pl.ANY)],
        out_specs=pl.BlockSpec(memory_space=pl.ANY),
        out_shape=jax.ShapeDtypeStruct(xd_zero.shape, xd_zero.dtype),
        scratch_shapes=[pltpu.SemaphoreType.DMA],
        input_output_aliases={2: 0},
        compiler_params=_params("arbitrary"),
        name="moe_dispatch",
    )(dest_flat, x1, xd_zero)


def _expert_kernel(blk_e_ref, nused_ref, xd_ref, wglu_ref, wlin_ref, bglu_ref, blin_ref, wdn_ref, bdn_ref, y_ref):
    del blk_e_ref
    live = pl.program_id(0) < nused_ref[0]

    @pl.when(live)
    def _():
        xb = xd_ref[...].astype(BF16)
        h_glu = jnp.dot(xb, wglu_ref[...], preferred_element_type=F32) + bglu_ref[...]
        h_lin = jnp.dot(xb, wlin_ref[...], preferred_element_type=F32) + blin_ref[...]
        x_glu = jnp.minimum(h_glu, SWIGLU_LIMIT)
        x_lin = jnp.clip(h_lin, -SWIGLU_LIMIT, SWIGLU_LIMIT)
        act = x_glu * _sigmoid(SWIGLU_ALPHA * x_glu) * (x_lin + 1.0)
        y_ref[...] = _dot(act, wdn_ref[...]) + bdn_ref[...]

    @pl.when(jnp.logical_not(live))
    def _():
        y_ref[...] = jnp.zeros_like(y_ref)


def _experts(blk_e, nused, xd, wglu, wlin, bglu, blin, wdn, bdn, block):
    n_rows, d = xd.shape
    dff = wglu.shape[2]
    grid_spec = pltpu.PrefetchScalarGridSpec(
        num_scalar_prefetch=2,
        grid=(n_rows // block,),
        in_specs=[pl.BlockSpec((block, d), lambda i, be, nu: (i, 0)),
                  pl.BlockSpec((None, d, dff), lambda i, be, nu: (be[i], 0, 0)),
                  pl.BlockSpec((None, d, dff), lambda i, be, nu: (be[i], 0, 0)),
                  pl.BlockSpec((None, 1, dff), lambda i, be, nu: (be[i], 0, 0)),
                  pl.BlockSpec((None, 1, dff), lambda i, be, nu: (be[i], 0, 0)),
                  pl.BlockSpec((None, dff, d), lambda i, be, nu: (be[i], 0, 0)),
                  pl.BlockSpec((None, 1, d), lambda i, be, nu: (be[i], 0, 0))],
        out_specs=pl.BlockSpec((block, d), lambda i, be, nu: (i, 0)),
    )
    return pl.pallas_call(
        _expert_kernel,
        grid_spec=grid_spec,
        out_shape=jax.ShapeDtypeStruct((n_rows, d), F32),
        compiler_params=_params("arbitrary"),
        name="moe_experts",
    )(blk_e, nused, xd, wglu, wlin, bglu, blin, wdn, bdn)


def _combine_kernel(dest_ref, route_ref, x1_ref, y_ref, lng_ref, lnb_ref, x2_ref, x2b_ref, ybuf, sem,
                    *, tile, alpha):
    def issue(i, carry):
        for kq in range(TOP_K):
            _row_copy(y_ref, dest_ref[i * TOP_K + kq], ybuf.at[kq], i, sem).start()
        return carry

    lax.fori_loop(0, tile, issue, 0)
    for kq in range(TOP_K):
        pltpu.make_async_copy(y_ref.at[pl.ds(0, tile)], ybuf.at[kq], sem).wait()
    route = route_ref[...]
    moe = route[:, ROUTE_GATE:ROUTE_GATE + 1] * ybuf[0]
    for kq in range(1, TOP_K):
        moe = moe + route[:, ROUTE_GATE + kq:ROUTE_GATE + kq + 1] * ybuf[kq]
    x2 = _layer_norm(alpha * x1_ref[...] + moe, lng_ref[...], lnb_ref[...])
    x2_ref[...] = x2
    x2b_ref[...] = x2.astype(BF16)


def _combine(dest_flat, route, x1, y_disp, lng, lnb, tile, alpha):
    n, d = x1.shape
    return pl.pallas_call(
        functools.partial(_combine_kernel, tile=tile, alpha=alpha),
        grid=(n // tile,),
        in_specs=[pl.BlockSpec((tile * TOP_K,), lambda i: (i,), memory_space=pltpu.SMEM),
                  pl.BlockSpec((tile, LANES), lambda i: (i, 0)),
                  pl.BlockSpec((tile, d), lambda i: (i, 0)),
                  pl.BlockSpec(memory_space=pl.ANY),
                  _const_spec((1, d)), _const_spec((1, d))],
        out_specs=[pl.BlockSpec((tile, d), lambda i: (i, 0)), pl.BlockSpec((tile, d), lambda i: (i, 0))],
        out_shape=[jax.ShapeDtypeStruct((n, d), F32), jax.ShapeDtypeStruct((n, d), BF16)],
        scratch_shapes=[pltpu.VMEM((TOP_K, tile, d), F32), pltpu.SemaphoreType.DMA],
        compiler_params=_params("arbitrary"),
        name="moe_combine",
    )(dest_flat, route, x1, y_disp, lng.reshape(1, d), lnb.reshape(1, d))


def _tiles(bsz, seq):
    n = bsz * seq
    return dict(
        ln=min(1024, n),
        mlstm=min(256, seq), mlstm_chunk=min(128, seq),
        rwkv_prep=min(256, seq),
        scan_steps=min(32, seq),
        xattn=min(512, seq),
        merge=min(256, n),
        moe_rows=min(256, n),
        moe_block=512,
    )


def _pad_cols(w, width):
    return jnp.pad(w, ((0, 0), (0, width - w.shape[1])))


def kernel(x, mem, ln_in_g, ln_in_b, mem_ln_g, mem_ln_b, w_in, ml_conv_w, ml_conv_b, ml_ig_b, ml_fg_b, ml_norm_g, rw_mu, rw_w0, rw_w_up, rw_a0, rw_a_up, rw_g_up, rw_kk, rw_ka, rw_rk, rw_ln_g, rw_ln_b, ca_w_kv, gate_b, w_br_ml, w_br_rw, w_br_ca, w_o, ln1_g, ln1_b, router_w, router_b, w_gu, b_gu, w_dn, b_dn, ln2_g, ln2_b):
    bsz, seq, d = x.shape
    mem_len = mem.shape[1]
    depth = w_in.shape[0]
    n = bsz * seq
    t = _tiles(bsz, seq)
    alpha = (2 * depth) ** 0.25
    d_ff = w_dn.shape[2]

    xf, xb = _ln_rows(x.reshape(n, d), ln_in_g, ln_in_b, t["ln"])
    _, memb = _ln_rows(mem.reshape(bsz * mem_len, d), mem_ln_g, mem_ln_b, min(t["ln"], bsz * mem_len))

    o_qk, o_v, o_og = 0, ML_QK_W, ML_QK_W + ML_W
    o_ig = o_og + ML_W
    o_fg = o_ig + ML_HEADS
    o_rw = o_fg + ML_HEADS
    o_ca = o_rw + 3 * RW_W + RW_DECAY_LORA + RW_AAA_LORA + RW_GATE_LORA
    o_gate = o_ca + CA_W

    bd = (jnp.arange(RW_W)[:, None] // RW_DH == jnp.arange(RW_W)[None, :] // RW_DH).astype(BF16)
    n_asg = n * TOP_K
    block = t["moe_block"]
    n_blocks = -(-n_asg // block) + N_EXPERTS
    n_rows = n_blocks * block

    for l in range(depth):
        w = w_in[l]
        wml = w[:, o_qk:o_ig].astype(BF16)
        wg = _pad_cols(w[:, o_ig:o_rw], LANES).astype(BF16)
        gb = _pad_cols(jnp.concatenate([ml_ig_b[l], ml_fg_b[l]])[None, :], LANES)
        h_ml = _mlstm(xb.reshape(bsz, seq, d), wml, wg, ml_conv_w[l], ml_conv_b[l][None, :], gb,
                      ml_norm_g[l][None, :], t["mlstm"], t["mlstm_chunk"])

        o_wd = o_rw + 3 * RW_W
        o_ad = o_wd + RW_DECAY_LORA
        o_gd = o_ad + RW_AAA_LORA
        wrw = jnp.concatenate([w[:, o_rw:o_wd], _pad_cols(w[:, o_wd:o_ad], LANES),
                               _pad_cols(w[:, o_ad:o_gd], LANES), w[:, o_gd:o_ca]], axis=1).astype(BF16)
        mu = rw_mu[l]
        mu_p = jnp.concatenate([mu[:3 * RW_W], jnp.pad(mu[3 * RW_W:3 * RW_W + RW_DECAY_LORA], (0, LANES - RW_DECAY_LORA)),
                                jnp.pad(mu[3 * RW_W + RW_DECAY_LORA:3 * RW_W + RW_DECAY_LORA + RW_AAA_LORA],
                                        (0, LANES - RW_AAA_LORA)),
                                mu[3 * RW_W + RW_DECAY_LORA + RW_AAA_LORA:]])[None, :]
        wup = jnp.pad(rw_w_up[l], ((0, LANES - RW_DECAY_LORA), (0, 0))).astype(BF16)
        aup = jnp.pad(rw_a_up[l], ((0, LANES - RW_AAA_LORA), (0, 0))).astype(BF16)
        r, wdec, k2, vr, kk, kka, g, bonus = _rwkv_prep(
            xb.reshape(bsz, seq, d), wrw, mu_p, rw_w0[l][None, :], wup, rw_a0[l][None, :], aup,
            rw_g_up[l].astype(BF16), rw_kk[l][None, :], rw_ka[l][None, :], rw_rk[l].reshape(1, RW_W), bd,
            t["rwkv_prep"])

        def to_scan(a):
            return a.reshape(bsz, seq, RW_HEADS, RW_DH).transpose(1, 3, 0, 2).reshape(seq, RW_DH, bsz * RW_HEADS)

        nl = bsz * RW_HEADS
        nl_pad = -(-nl // LANES) * LANES
        ops = [to_scan(a) for a in (r, wdec, k2, vr, kk, kka)]
        if nl_pad != nl:
            ops = [jnp.pad(a, ((0, 0), (0, 0), (0, nl_pad - nl))) for a in ops]
        o_scan = _rwkv_scan(*ops, t["scan_steps"])[:, :, :nl]
        o_rwkv = o_scan.reshape(seq, RW_DH, bsz, RW_HEADS).transpose(2, 0, 3, 1).reshape(n, RW_W)

        kv = _matmul(memb, ca_w_kv[l].astype(BF16), min(512, bsz * mem_len)).reshape(bsz, mem_len, 2 * CA_W)
        kt = kv[:, :, :CA_W].transpose(0, 2, 1).astype(BF16)
        vm = kv[:, :, CA_W:].astype(BF16)
        h_ca = _xattn(xb.reshape(bsz, seq, d), w[:, o_ca:o_gate].astype(BF16), kt, vm, t["xattn"])

        rw_pad = _pad_cols(router_w[l], LANES)
        rw_hi = rw_pad.astype(BF16)
        rw_lo = (rw_pad - rw_hi.astype(F32)).astype(BF16)
        rb = jnp.concatenate([router_b[l], jnp.full((LANES - N_EXPERTS,), NEG_BIG, F32)])[None, :]
        consts = (w[:, o_gate:].astype(BF16), gate_b[l][None, :], bd, rw_ln_g[l][None, :], rw_ln_b[l][None, :],
                  w_br_ml[l].astype(BF16), w_br_rw[l].astype(BF16), w_br_ca[l].astype(BF16), w_o[l].astype(BF16),
                  ln1_g[l][None, :], ln1_b[l][None, :], rw_hi, rw_lo, rb)
        x1, x1b, route, counts = _merge(xf, xb, h_ml.reshape(n, ML_W), o_rwkv, bonus.reshape(n, RW_W),
                                        g.reshape(n, RW_W), h_ca.reshape(n, CA_W), consts, t["merge"], alpha)
        del x1b

        cnt = counts[0, :N_EXPERTS].astype(jnp.int32)
        blocks_per = (cnt + block - 1) // block
        blk_end = jnp.cumsum(blocks_per)
        slot_start = (blk_end - blocks_per) * block
        e_idx = route[:, ROUTE_IDX:ROUTE_IDX + TOP_K].astype(jnp.int32)
        rank = route[:, ROUTE_RANK:ROUTE_RANK + TOP_K].astype(jnp.int32)
        dest = (slot_start[e_idx] + rank).reshape(n_asg)
        blk_e = jnp.minimum(jnp.searchsorted(blk_end, jnp.arange(n_blocks, dtype=jnp.int32), side="right"),
                            N_EXPERTS - 1).astype(jnp.int32)
        nused = blk_end[-1:].astype(jnp.int32)

        xd = _dispatch(dest, x1, jnp.zeros((n_rows, d), F32), t["moe_rows"])
        wgu = w_gu[l]
        y_disp = _experts(blk_e, nused, xd, wgu[:, :, 0::2].astype(BF16), wgu[:, :, 1::2].astype(BF16),
                          b_gu[l][:, None, 0::2], b_gu[l][:, None, 1::2], w_dn[l].astype(BF16),
                          b_dn[l][:, None, :], block)
        xf, xb = _combine(dest, route, x1, y_disp, ln2_g[l], ln2_b[l], t["moe_rows"], alpha)

    del d_ff
    return xf.reshape(bsz, seq, d)
```

```python
import functools

import jax
import jax.numpy as jnp
from jax import lax
from jax.experimental import pallas as pl
from jax.experimental.pallas import tpu as pltpu

ML_HEADS, ML_DQK, ML_DV, ML_CONV = 4, 64, 128, 4
ML_W = ML_HEADS * ML_DV
ML_QK_W = 2 * ML_HEADS * ML_DQK
RW_HEADS, RW_DH = 8, 64
RW_W = RW_HEADS * RW_DH
RW_DECAY_LORA, RW_AAA_LORA, RW_GATE_LORA = 64, 64, 128
RW_GN_EPS = 64e-5
CA_HEADS, CA_DH = 4, 128
CA_W = CA_HEADS * CA_DH
N_BRANCH = 3
N_EXPERTS, TOP_K = 32, 4
SWIGLU_LIMIT, SWIGLU_ALPHA = 7.0, 1.702
LN_EPS = 1e-5

LANES = 128
SUBLANES = 8
VMEM_LIMIT = 56 * 1024 * 1024

BF16 = jnp.bfloat16
F32 = jnp.float32
NEG_BIG = -1e30


def _dot(a, b):
    return jnp.dot(a.astype(BF16), b.astype(BF16), preferred_element_type=F32)


def _split3(a):
    hi = a.astype(BF16)
    r1 = a - hi.astype(F32)
    mid = r1.astype(BF16)
    lo = (r1 - mid.astype(F32)).astype(BF16)
    return hi, mid, lo


def _dot_exact_rhs(a, b_bf16):
    hi, mid, lo = _split3(a)
    d = functools.partial(jnp.dot, preferred_element_type=F32)
    return d(hi, b_bf16) + d(mid, b_bf16) + d(lo, b_bf16)


def _dot_exact_lhs(a_bf16, b):
    hi, mid, lo = _split3(b)
    d = functools.partial(jnp.dot, preferred_element_type=F32)
    return d(a_bf16, hi) + d(a_bf16, mid) + d(a_bf16, lo)


def _sigmoid(x):
    return 1.0 / (1.0 + jnp.exp(-x))


def _softplus(x):
    return jnp.maximum(x, 0.0) + jnp.log1p(jnp.exp(-jnp.abs(x)))


def _layer_norm(v, g, b, eps=LN_EPS):
    mu = jnp.mean(v, axis=-1, keepdims=True)
    c = v - mu
    var = jnp.mean(c * c, axis=-1, keepdims=True)
    return c * lax.rsqrt(var + eps) * g + b


def _params(*sem):
    return pltpu.CompilerParams(dimension_semantics=sem, vmem_limit_bytes=VMEM_LIMIT)


def _const_spec(shape):
    nd = len(shape)
    return pl.BlockSpec(shape, lambda *_: (0,) * nd)


def _ln_kernel(x_ref, g_ref, b_ref, o_ref, ob_ref):
    y = _layer_norm(x_ref[...], g_ref[...], b_ref[...])
    o_ref[...] = y
    ob_ref[...] = y.astype(BF16)


def _ln_rows(x2d, g, b, tile):
    n, d = x2d.shape
    return pl.pallas_call(
        _ln_kernel,
        grid=(n // tile,),
        in_specs=[pl.BlockSpec((tile, d), lambda i: (i, 0)), _const_spec((1, d)), _const_spec((1, d))],
        out_specs=[pl.BlockSpec((tile, d), lambda i: (i, 0)), pl.BlockSpec((tile, d), lambda i: (i, 0))],
        out_shape=[jax.ShapeDtypeStruct((n, d), F32), jax.ShapeDtypeStruct((n, d), BF16)],
        compiler_params=_params("parallel"),
        name="ln_rows",
    )(x2d, g.reshape(1, d), b.reshape(1, d))


def _mm_kernel(a_ref, b_ref, o_ref):
    o_ref[...] = jnp.dot(a_ref[...], b_ref[...], preferred_element_type=F32)


def _matmul(a, b, tile):
    m, k = a.shape
    n = b.shape[1]
    return pl.pallas_call(
        _mm_kernel,
        grid=(m // tile,),
        in_specs=[pl.BlockSpec((tile, k), lambda i: (i, 0)), _const_spec((k, n))],
        out_specs=pl.BlockSpec((tile, n), lambda i: (i, 0)),
        out_shape=jax.ShapeDtypeStruct((m, n), F32),
        compiler_params=_params("parallel"),
        name="matmul",
    )(a, b)


def _mlstm_kernel(xb_ref, wml_ref, wg_ref, convw_ref, convb_ref, gb_ref, normg_ref, tri_ref,
                  h_ref, ubuf, c_ref, n_ref, m_ref, *, tile, chunk):
    @pl.when(pl.program_id(1) == 0)
    def _():
        ubuf[0:SUBLANES, :] = jnp.zeros((SUBLANES, ML_QK_W), F32)
        c_ref[...] = jnp.zeros_like(c_ref)
        n_ref[...] = jnp.zeros_like(n_ref)
        m_ref[...] = jnp.zeros_like(m_ref)

    xb = xb_ref[...]
    u = jnp.dot(xb, wml_ref[...], preferred_element_type=F32)
    gates = jnp.dot(xb, wg_ref[...], preferred_element_type=F32) + gb_ref[...]

    ubuf[SUBLANES:SUBLANES + tile, :] = u[:, :ML_QK_W]
    acc = jnp.broadcast_to(convb_ref[...], (tile, ML_QK_W))
    for j in range(ML_CONV):
        acc = acc + convw_ref[j:j + 1, :] * ubuf[pl.ds(SUBLANES - ML_CONV + 1 + j, tile), :]
    ubuf[0:SUBLANES, :] = ubuf[tile:tile + SUBLANES, :]
    qk = acc * _sigmoid(acc)
    q = qk[:, :ML_HEADS * ML_DQK] * (ML_DQK ** -0.5)
    k = qk[:, ML_HEADS * ML_DQK:]
    v = u[:, ML_QK_W:ML_QK_W + ML_W]
    og = u[:, ML_QK_W + ML_W:]

    lane = lax.broadcasted_iota(jnp.int32, (chunk, LANES), 1)
    row = lax.broadcasted_iota(jnp.int32, (chunk, chunk), 0)
    col = lax.broadcasted_iota(jnp.int32, (chunk, chunk), 1)
    causal = row >= col
    log_f = -_softplus(-gates)

    for c in range(tile // chunk):
        rs = slice(c * chunk, (c + 1) * chunk)
        bcum = _dot_exact_lhs(tri_ref[...], log_f[rs])
        mcol = jnp.where(lane < ML_HEADS, gates[rs], bcum)
        mrow = mcol.T
        for h in range(ML_HEADS):
            i_col = mcol[:, h:h + 1]
            b_col = mcol[:, ML_HEADS + h:ML_HEADS + h + 1]
            i_row = mrow[h:h + 1, :]
            b_row = mrow[ML_HEADS + h:ML_HEADS + h + 1, :]
            m_prev = m_ref[h][0:1, 0:1]
            n_prev = n_ref[h][0:1, :]
            c_prev = c_ref[h]
            qh = q[rs, h * ML_DQK:(h + 1) * ML_DQK]
            kh = k[rs, h * ML_DQK:(h + 1) * ML_DQK]
            vh = v[rs, h * ML_DV:(h + 1) * ML_DV]

            dmat = jnp.where(causal, b_col - b_row + i_row, -jnp.inf)
            m_inter = b_col + m_prev
            m_t = jnp.maximum(m_inter, jnp.max(dmat, axis=1, keepdims=True))
            s = lax.dot_general(qh.astype(BF16), kh.astype(BF16), (((1,), (1,)), ((), ())),
                                preferred_element_type=F32)
            w_intra = jnp.exp(dmat - m_t) * s
            s_inter = jnp.exp(m_inter - m_t)
            num = s_inter * _dot(qh, c_prev) + _dot(w_intra, vh)
            den = (s_inter * jnp.sum(qh * n_prev, axis=1, keepdims=True)
                   + jnp.sum(w_intra, axis=1, keepdims=True))
            hh = num / jnp.maximum(jnp.abs(den), jnp.exp(-m_t))

            b_last = b_col[chunk - 1:chunk, :]
            g_col = b_last - b_col + i_col
            m_new = jnp.maximum(b_last + m_prev, jnp.max(g_col, axis=0, keepdims=True))
            carry = jnp.exp(b_last + m_prev - m_new)
            kw = kh * jnp.exp(g_col - m_new)
            c_ref[h] = carry * c_prev + lax.dot_general(
                kw.astype(BF16), vh.astype(BF16), (((0,), (0,)), ((), ())), preferred_element_type=F32)
            n_new = carry * n_prev + jnp.sum(kw, axis=0, keepdims=True)
            n_ref[h] = jnp.broadcast_to(n_new, (SUBLANES, ML_DQK))
            m_ref[h] = jnp.broadcast_to(m_new, (SUBLANES, LANES))

            mu = jnp.mean(hh, axis=1, keepdims=True)
            cen = hh - mu
            var = jnp.mean(cen * cen, axis=1, keepdims=True)
            y = cen * lax.rsqrt(var + LN_EPS) * normg_ref[:, h * ML_DV:(h + 1) * ML_DV]
            h_ref[rs, h * ML_DV:(h + 1) * ML_DV] = (
                _sigmoid(og[rs, h * ML_DV:(h + 1) * ML_DV]) * y).astype(BF16)


def _mlstm(xb, wml, wg, convw, convb, gb, normg, tile, chunk):
    bsz, s, d = xb.shape
    tri = (jnp.arange(chunk)[:, None] >= jnp.arange(chunk)[None, :]).astype(BF16)
    kern = functools.partial(_mlstm_kernel, tile=tile, chunk=chunk)
    return pl.pallas_call(
        kern,
        grid=(bsz, s // tile),
        in_specs=[pl.BlockSpec((None, tile, d), lambda b, j: (b, j, 0)),
                  _const_spec(wml.shape), _const_spec(wg.shape), _const_spec(convw.shape),
                  _const_spec(convb.shape), _const_spec(gb.shape), _const_spec(normg.shape),
                  _const_spec(tri.shape)],
        out_specs=pl.BlockSpec((None, tile, ML_W), lambda b, j: (b, j, 0)),
        out_shape=jax.ShapeDtypeStruct((bsz, s, ML_W), BF16),
        scratch_shapes=[pltpu.VMEM((tile + SUBLANES, ML_QK_W), F32),
                        pltpu.VMEM((ML_HEADS, ML_DQK, ML_DV), F32),
                        pltpu.VMEM((ML_HEADS, SUBLANES, ML_DQK), F32),
                        pltpu.VMEM((ML_HEADS, SUBLANES, LANES), F32)],
        compiler_params=_params("parallel", "arbitrary"),
        name="mlstm",
    )(xb, wml, wg, convw, convb, gb, normg, tri)


RW_PAD_IN = 3 * RW_W + 3 * LANES


def _rwkv_prep_kernel(xb_ref, wrw_ref, mu_ref, w0_ref, wup_ref, a0_ref, aup_ref, gup_ref, kkw_ref,
                      ka_ref, rk_ref, bd_ref,
                      r_ref, w_ref, k_ref, v_ref, kk_ref, kka_ref, g_ref, bonus_ref, ubuf, *, tile):
    @pl.when(pl.program_id(1) == 0)
    def _():
        ubuf[0:SUBLANES, :] = jnp.zeros((SUBLANES, RW_PAD_IN), F32)

    u = jnp.dot(xb_ref[...], wrw_ref[...], preferred_element_type=F32)
    ubuf[SUBLANES:SUBLANES + tile, :] = u
    u_prev = ubuf[pl.ds(SUBLANES - 1, tile), :]
    ubuf[0:SUBLANES, :] = ubuf[tile:tile + SUBLANES, :]
    us = u + (u_prev - u) * mu_ref[...]
    r = us[:, 0:RW_W]
    kr = us[:, RW_W:2 * RW_W]
    vr = us[:, 2 * RW_W:3 * RW_W]
    wd = us[:, 3 * RW_W:3 * RW_W + LANES]
    ad = us[:, 3 * RW_W + LANES:3 * RW_W + 2 * LANES]
    gd = us[:, 3 * RW_W + 2 * LANES:]

    w_log = -_softplus(-(w0_ref[...] + _dot(jnp.tanh(wd), wup_ref[...]))) - 0.5
    decay = jnp.exp(-jnp.exp(w_log))
    a = _sigmoid(a0_ref[...] + _dot(ad, aup_ref[...]))
    g = _dot(_sigmoid(gd), gup_ref[...])

    bd = bd_ref[...]
    kk = kr * kkw_ref[...]
    kk = kk * lax.rsqrt(jnp.maximum(_dot_exact_rhs(kk * kk, bd), 1e-24))
    k2 = kr * (1.0 + (a - 1.0) * ka_ref[...])
    bonus = _dot_exact_rhs(r * k2 * rk_ref[...], bd) * vr

    r_ref[...] = r
    w_ref[...] = decay
    k_ref[...] = k2
    v_ref[...] = vr
    kk_ref[...] = kk
    kka_ref[...] = kk * a
    g_ref[...] = g
    bonus_ref[...] = bonus


def _rwkv_prep(xb, wrw, mu, w0, wup, a0, aup, gup, kkw, ka, rk, bd, tile):
    bsz, s, d = xb.shape
    consts = (wrw, mu, w0, wup, a0, aup, gup, kkw, ka, rk, bd)
    spec = pl.BlockSpec((None, tile, RW_W), lambda b, j: (b, j, 0))
    return pl.pallas_call(
        functools.partial(_rwkv_prep_kernel, tile=tile),
        grid=(bsz, s // tile),
        in_specs=[pl.BlockSpec((None, tile, d), lambda b, j: (b, j, 0))] + [_const_spec(c.shape) for c in consts],
        out_specs=[spec] * 8,
        out_shape=[jax.ShapeDtypeStruct((bsz, s, RW_W), F32)] * 8,
        scratch_shapes=[pltpu.VMEM((tile + SUBLANES, RW_PAD_IN), F32)],
        compiler_params=_params("parallel", "arbitrary"),
        name="rwkv_prep",
    )(xb, *consts)


def _rwkv_scan_kernel(r_ref, w_ref, k_ref, v_ref, kk_ref, kka_ref, o_ref, st_ref, *, steps):
    @pl.when(pl.program_id(1) == 0)
    def _():
        st_ref[...] = jnp.zeros_like(st_ref)

    zeros = jnp.zeros((RW_DH, LANES), F32)

    def first(kc, sa):
        return sa + st_ref[kc] * kk_ref[0, pl.ds(kc, 1), :]

    sa0 = lax.fori_loop(0, RW_DH, first, zeros, unroll=8)

    def step(t, sa):
        vt = v_ref[t]
        tn = jnp.minimum(t + 1, steps - 1)

        def body(kc, carry):
            out, sa_next = carry
            row = pl.ds(kc, 1)
            new = st_ref[kc] * w_ref[t, row, :] - sa * kka_ref[t, row, :] + vt * k_ref[t, row, :]
            st_ref[kc] = new
            return out + new * r_ref[t, row, :], sa_next + new * kk_ref[tn, row, :]

        out, sa_next = lax.fori_loop(0, RW_DH, body, (zeros, zeros), unroll=8)
        o_ref[t] = out
        return sa_next

    lax.fori_loop(0, steps, step, sa0)


def _rwkv_scan(r, w, k, v, kk, kka, steps):
    s, dh, nl = r.shape
    spec = pl.BlockSpec((steps, dh, LANES), lambda g, j: (j, 0, g))
    return pl.pallas_call(
        functools.partial(_rwkv_scan_kernel, steps=steps),
        grid=(nl // LANES, s // steps),
        in_specs=[spec] * 6,
        out_specs=spec,
        out_shape=jax.ShapeDtypeStruct((s, dh, nl), F32),
        scratch_shapes=[pltpu.VMEM((dh, dh, LANES), F32)],
        compiler_params=_params("parallel", "arbitrary"),
        name="rwkv_scan",
    )(r, w, k, v, kk, kka)


def _xattn_kernel(xb_ref, wq_ref, kt_ref, v_ref, o_ref):
    q = jnp.dot(xb_ref[...], wq_ref[...], preferred_element_type=F32)
    for h in range(CA_HEADS):
        hs = slice(h * CA_DH, (h + 1) * CA_DH)
        s = jnp.dot(q[:, hs].astype(BF16), kt_ref[hs, :], preferred_element_type=F32) * (CA_DH ** -0.5)
        p = jnp.exp(s - jnp.max(s, axis=1, keepdims=True))
        den = jnp.sum(p, axis=1, keepdims=True)
        o = jnp.dot(p.astype(BF16), v_ref[:, hs], preferred_element_type=F32) / den
        o_ref[:, hs] = o.astype(BF16)


def _xattn(xb, wq, kt, v, tile):
    bsz, s, d = xb.shape
    m = v.shape[1]
    return pl.pallas_call(
        _xattn_kernel,
        grid=(bsz, s // tile),
        in_specs=[pl.BlockSpec((None, tile, d), lambda b, j: (b, j, 0)), _const_spec(wq.shape),
                  pl.BlockSpec((None, CA_W, m), lambda b, j: (b, 0, 0)),
                  pl.BlockSpec((None, m, CA_W), lambda b, j: (b, 0, 0))],
        out_specs=pl.BlockSpec((None, tile, CA_W), lambda b, j: (b, j, 0)),
        out_shape=jax.ShapeDtypeStruct((bsz, s, CA_W), BF16),
        compiler_params=_params("parallel", "parallel"),
        name="xattn",
    )(xb, wq, kt, v)


ROUTE_IDX, ROUTE_RANK, ROUTE_GATE = 0, TOP_K, 2 * TOP_K


def _merge_kernel(x_ref, xb_ref, hml_ref, o_ref, bonus_ref, g_ref, hca_ref,
                  wgate_ref, gateb_ref, bd_ref, rwg_ref, rwb_ref, wml_ref, wrw_ref, wca_ref, wo_ref,
                  lng_ref, lnb_ref, rwhi_ref, rwlo_ref, rb_ref, tri_ref,
                  x1_ref, x1b_ref, route_ref, cnt_ref, carry_ref, *, tile, alpha, d_model):
    @pl.when(pl.program_id(0) == 0)
    def _():
        carry_ref[...] = jnp.zeros_like(carry_ref)

    bd = bd_ref[...]
    o = o_ref[...]
    mu = _dot_exact_rhs(o, bd) * (1.0 / RW_DH)
    cen = o - mu
    var = _dot_exact_rhs(cen * cen, bd) * (1.0 / RW_DH)
    h_rw = (cen * lax.rsqrt(var + RW_GN_EPS) * rwg_ref[...] + rwb_ref[...] + bonus_ref[...]) * g_ref[...]

    gates = _sigmoid(jnp.dot(xb_ref[...], wgate_ref[...], preferred_element_type=F32) + gateb_ref[...])
    y = (gates[:, 0:d_model] * jnp.dot(hml_ref[...], wml_ref[...], preferred_element_type=F32)
         + gates[:, d_model:2 * d_model] * _dot(h_rw, wrw_ref[...])
         + gates[:, 2 * d_model:] * jnp.dot(hca_ref[...], wca_ref[...], preferred_element_type=F32))
    mixed = _dot(y, wo_ref[...])
    x1 = _layer_norm(alpha * x_ref[...] + mixed, lng_ref[...], lnb_ref[...])
    x1_ref[...] = x1
    x1b_ref[...] = x1.astype(BF16)

    hi = x1.astype(BF16)
    lo = (x1 - hi.astype(F32)).astype(BF16)
    d = functools.partial(jnp.dot, preferred_element_type=F32)
    logits = d(hi, rwhi_ref[...]) + d(lo, rwhi_ref[...]) + d(hi, rwlo_ref[...]) + rb_ref[...]

    lane = lax.broadcasted_iota(jnp.int32, (tile, LANES), 1)
    vals = logits
    tops, onehots = [], []
    for _ in range(TOP_K):
        m = jnp.max(vals, axis=1, keepdims=True)
        idx = jnp.min(jnp.where(vals == m, lane, LANES), axis=1, keepdims=True)
        sel = lane == idx
        tops.append((m, idx))
        onehots.append(sel.astype(F32))
        vals = jnp.where(sel, -jnp.inf, vals)
    exps = [jnp.exp(m - tops[0][0]) for m, _ in tops]
    den = exps[0] + exps[1] + exps[2] + exps[3]

    cnt = onehots[0] + onehots[1] + onehots[2] + onehots[3]
    before = jnp.dot(tri_ref[...], cnt.astype(BF16), preferred_element_type=F32) + carry_ref[0:1, :]
    route = jnp.zeros((tile, LANES), F32)
    for kq in range(TOP_K):
        rank = jnp.sum(onehots[kq] * before, axis=1, keepdims=True)
        route = jnp.where(lane == ROUTE_IDX + kq, tops[kq][1].astype(F32), route)
        route = jnp.where(lane == ROUTE_RANK + kq, rank, route)
        route = jnp.where(lane == ROUTE_GATE + kq, exps[kq] / den, route)
    route_ref[...] = route
    total = carry_ref[0:1, :] + jnp.sum(cnt, axis=0, keepdims=True)
    carry_ref[...] = jnp.broadcast_to(total, carry_ref.shape)
    cnt_ref[...] = jnp.broadcast_to(total, cnt_ref.shape)


def _merge(x, xb, hml, o, bonus, g, hca, consts, tile, alpha):
    n, d = x.shape
    tri = (jnp.arange(tile)[:, None] > jnp.arange(tile)[None, :]).astype(BF16)
    consts = tuple(consts) + (tri,)

    def rows(w):
        return pl.BlockSpec((tile, w), lambda i: (i, 0))

    return pl.pallas_call(
        functools.partial(_merge_kernel, tile=tile, alpha=alpha, d_model=d),
        grid=(n // tile,),
        in_specs=[rows(d), rows(d), rows(ML_W), rows(RW_W), rows(RW_W), rows(RW_W), rows(CA_W)]
        + [_const_spec(c.shape) for c in consts],
        out_specs=[rows(d), rows(d), rows(LANES), _const_spec((SUBLANES, LANES))],
        out_shape=[jax.ShapeDtypeStruct((n, d), F32), jax.ShapeDtypeStruct((n, d), BF16),
                   jax.ShapeDtypeStruct((n, LANES), F32), jax.ShapeDtypeStruct((SUBLANES, LANES), F32)],
        scratch_shapes=[pltpu.VMEM((SUBLANES, LANES), F32)],
        compiler_params=_params("arbitrary"),
        name="merge_route",
    )(x, xb, hml, o, bonus, g, hca, *consts)


def _row_copy(src_ref, src_row, dst_ref, dst_row, sem):
    return pltpu.make_async_copy(src_ref.at[pl.ds(src_row, 1)], dst_ref.at[pl.ds(dst_row, 1)], sem)


def _dispatch_kernel(dest_ref, x_ref, xd_in_ref, xd_ref, sem, *, tile):
    del xd_in_ref

    def issue(i, carry):
        for kq in range(TOP_K):
            _row_copy(x_ref, i, xd_ref, dest_ref[i * TOP_K + kq], sem).start()
        return carry

    lax.fori_loop(0, tile, issue, 0)
    for _ in range(TOP_K):
        pltpu.make_async_copy(x_ref, xd_ref.at[pl.ds(0, tile)], sem).wait()


def _dispatch(dest_flat, x1, xd_zero, tile):
    n, d = x1.shape
    return pl.pallas_call(
        functools.partial(_dispatch_kernel, tile=tile),
        grid=(n // tile,),
        in_specs=[pl.BlockSpec((tile * TOP_K,), lambda i: (i,), memory_space=pltpu.SMEM),
                  pl.BlockSpec((tile, d), lambda i: (i, 0)),
                  pl.BlockSpec(memory_space=pl.ANY)],
        out_specs=pl.BlockSpec(memory_space=pl.ANY),
        out_shape=jax.ShapeDtypeStruct(xd_zero.shape, xd_zero.dtype),
        scratch_shapes=[pltpu.SemaphoreType.DMA],
        input_output_aliases={2: 0},
        compiler_params=_params("arbitrary"),
        name="moe_dispatch",
    )(dest_flat, x1, xd_zero)


DEINT_COLS = 512


def _deinterleave_kernel(w_ref, pe_ref, po_ref, glu_ref, lin_ref):
    wb = w_ref[...].astype(BF16)
    glu_ref[...] = jnp.dot(wb, pe_ref[...], preferred_element_type=F32).astype(BF16)
    lin_ref[...] = jnp.dot(wb, po_ref[...], preferred_element_type=F32).astype(BF16)


def _deinterleave(w_gu):
    e, d, two_ff = w_gu.shape
    half = DEINT_COLS // 2
    src = jnp.arange(DEINT_COLS)[:, None]
    dst = jnp.arange(half)[None, :]
    pe = (src == 2 * dst).astype(BF16)
    po = (src == 2 * dst + 1).astype(BF16)
    out = jax.ShapeDtypeStruct((e, d, two_ff // 2), BF16)
    return pl.pallas_call(
        _deinterleave_kernel,
        grid=(e, two_ff // DEINT_COLS),
        in_specs=[pl.BlockSpec((None, d, DEINT_COLS), lambda i, c: (i, 0, c)),
                  _const_spec(pe.shape), _const_spec(po.shape)],
        out_specs=[pl.BlockSpec((None, d, half), lambda i, c: (i, 0, c))] * 2,
        out_shape=[out, out],
        compiler_params=_params("parallel", "parallel"),
        name="deinterleave_w",
    )(w_gu, pe, po)


def _expert_kernel(blk_e_ref, nused_ref, xd_ref, wglu_ref, wlin_ref, bglu_ref, blin_ref, wdn_ref, bdn_ref, y_ref):
    del blk_e_ref
    live = pl.program_id(0) < nused_ref[0]

    @pl.when(live)
    def _():
        xb = xd_ref[...].astype(BF16)
        h_glu = jnp.dot(xb, wglu_ref[...], preferred_element_type=F32) + bglu_ref[...]
        h_lin = jnp.dot(xb, wlin_ref[...], preferred_element_type=F32) + blin_ref[...]
        x_glu = jnp.minimum(h_glu, SWIGLU_LIMIT)
        x_lin = jnp.clip(h_lin, -SWIGLU_LIMIT, SWIGLU_LIMIT)
        act = x_glu * _sigmoid(SWIGLU_ALPHA * x_glu) * (x_lin + 1.0)
        y_ref[...] = _dot(act, wdn_ref[...]) + bdn_ref[...]

    @pl.when(jnp.logical_not(live))
    def _():
        y_ref[...] = jnp.zeros_like(y_ref)


def _experts(blk_e, nused, xd, wglu, wlin, bglu, blin, wdn, bdn, block):
    n_rows, d = xd.shape
    dff = wglu.shape[2]
    grid_spec = pltpu.PrefetchScalarGridSpec(
        num_scalar_prefetch=2,
        grid=(n_rows // block,),
        in_specs=[pl.BlockSpec((block, d), lambda i, be, nu: (i, 0)),
                  pl.BlockSpec((None, d, dff), lambda i, be, nu: (be[i], 0, 0)),
                  pl.BlockSpec((None, d, dff), lambda i, be, nu: (be[i], 0, 0)),
                  pl.BlockSpec((None, 1, dff), lambda i, be, nu: (be[i], 0, 0)),
                  pl.BlockSpec((None, 1, dff), lambda i, be, nu: (be[i], 0, 0)),
                  pl.BlockSpec((None, dff, d), lambda i, be, nu: (be[i], 0, 0)),
                  pl.BlockSpec((None, 1, d), lambda i, be, nu: (be[i], 0, 0))],
        out_specs=pl.BlockSpec((block, d), lambda i, be, nu: (i, 0)),
    )
    return pl.pallas_call(
        _expert_kernel,
        grid_spec=grid_spec,
        out_shape=jax.ShapeDtypeStruct((n_rows, d), F32),
        compiler_params=_params("arbitrary"),
        name="moe_experts",
    )(blk_e, nused, xd, wglu, wlin, bglu, blin, wdn, bdn)


def _combine_kernel(dest_ref, route_ref, x1_ref, y_ref, lng_ref, lnb_ref, x2_ref, x2b_ref, ybuf, sem,
                    *, tile, alpha):
    def issue(i, carry):
        for kq in range(TOP_K):
            _row_copy(y_ref, dest_ref[i * TOP_K + kq], ybuf.at[kq], i, sem).start()
        return carry

    lax.fori_loop(0, tile, issue, 0)
    for kq in range(TOP_K):
        pltpu.make_async_copy(y_ref.at[pl.ds(0, tile)], ybuf.at[kq], sem).wait()
    route = route_ref[...]
    moe = route[:, ROUTE_GATE:ROUTE_GATE + 1] * ybuf[0]
    for kq in range(1, TOP_K):
        moe = moe + route[:, ROUTE_GATE + kq:ROUTE_GATE + kq + 1] * ybuf[kq]
    x2 = _layer_norm(alpha * x1_ref[...] + moe, lng_ref[...], lnb_ref[...])
    x2_ref[...] = x2
    x2b_ref[...] = x2.astype(BF16)


def _combine(dest_flat, route, x1, y_disp, lng, lnb, tile, alpha):
    n, d = x1.shape
    return pl.pallas_call(
        functools.partial(_combine_kernel, tile=tile, alpha=alpha),
        grid=(n // tile,),
        in_specs=[pl.BlockSpec((tile * TOP_K,), lambda i: (i,), memory_space=pltpu.SMEM),
                  pl.BlockSpec((tile, LANES), lambda i: (i, 0)),
                  pl.BlockSpec((tile, d), lambda i: (i, 0)),
                  pl.BlockSpec(memory_space=pl.ANY),
                  _const_spec((1, d)), _const_spec((1, d))],
        out_specs=[pl.BlockSpec((tile, d), lambda i: (i, 0)), pl.BlockSpec((tile, d), lambda i: (i, 0))],
        out_shape=[jax.ShapeDtypeStruct((n, d), F32), jax.ShapeDtypeStruct((n, d), BF16)],
        scratch_shapes=[pltpu.VMEM((TOP_K, tile, d), F32), pltpu.SemaphoreType.DMA],
        compiler_params=_params("arbitrary"),
        name="moe_combine",
    )(dest_flat, route, x1, y_disp, lng.reshape(1, d), lnb.reshape(1, d))


def _tiles(bsz, seq):
    n = bsz * seq
    return dict(
        ln=min(1024, n),
        mlstm=min(256, seq), mlstm_chunk=min(128, seq),
        rwkv_prep=min(256, seq),
        scan_steps=min(32, seq),
        xattn=min(512, seq),
        merge=min(256, n),
        moe_rows=min(256, n),
        moe_block=512,
    )


def _pad_cols(w, width):
    return jnp.pad(w, ((0, 0), (0, width - w.shape[1])))


def kernel(x, mem, ln_in_g, ln_in_b, mem_ln_g, mem_ln_b, w_in, ml_conv_w, ml_conv_b, ml_ig_b, ml_fg_b, ml_norm_g, rw_mu, rw_w0, rw_w_up, rw_a0, rw_a_up, rw_g_up, rw_kk, rw_ka, rw_rk, rw_ln_g, rw_ln_b, ca_w_kv, gate_b, w_br_ml, w_br_rw, w_br_ca, w_o, ln1_g, ln1_b, router_w, router_b, w_gu, b_gu, w_dn, b_dn, ln2_g, ln2_b):
    bsz, seq, d = x.shape
    mem_len = mem.shape[1]
    depth = w_in.shape[0]
    n = bsz * seq
    t = _tiles(bsz, seq)
    alpha = (2 * depth) ** 0.25
    d_ff = w_dn.shape[2]

    xf, xb = _ln_rows(x.reshape(n, d), ln_in_g, ln_in_b, t["ln"])
    _, memb = _ln_rows(mem.reshape(bsz * mem_len, d), mem_ln_g, mem_ln_b, min(t["ln"], bsz * mem_len))

    o_qk, o_v, o_og = 0, ML_QK_W, ML_QK_W + ML_W
    o_ig = o_og + ML_W
    o_fg = o_ig + ML_HEADS
    o_rw = o_fg + ML_HEADS
    o_ca = o_rw + 3 * RW_W + RW_DECAY_LORA + RW_AAA_LORA + RW_GATE_LORA
    o_gate = o_ca + CA_W

    bd = (jnp.arange(RW_W)[:, None] // RW_DH == jnp.arange(RW_W)[None, :] // RW_DH).astype(BF16)
    n_asg = n * TOP_K
    block = t["moe_block"]
    n_blocks = -(-n_asg // block) + N_EXPERTS
    n_rows = n_blocks * block

    for l in range(depth):
        w = w_in[l]
        wml = w[:, o_qk:o_ig].astype(BF16)
        wg = _pad_cols(w[:, o_ig:o_rw], LANES).astype(BF16)
        gb = _pad_cols(jnp.concatenate([ml_ig_b[l], ml_fg_b[l]])[None, :], LANES)
        h_ml = _mlstm(xb.reshape(bsz, seq, d), wml, wg, ml_conv_w[l], ml_conv_b[l][None, :], gb,
                      ml_norm_g[l][None, :], t["mlstm"], t["mlstm_chunk"])

        o_wd = o_rw + 3 * RW_W
        o_ad = o_wd + RW_DECAY_LORA
        o_gd = o_ad + RW_AAA_LORA
        wrw = jnp.concatenate([w[:, o_rw:o_wd], _pad_cols(w[:, o_wd:o_ad], LANES),
                               _pad_cols(w[:, o_ad:o_gd], LANES), w[:, o_gd:o_ca]], axis=1).astype(BF16)
        mu = rw_mu[l]
        mu_p = jnp.concatenate([mu[:3 * RW_W], jnp.pad(mu[3 * RW_W:3 * RW_W + RW_DECAY_LORA], (0, LANES - RW_DECAY_LORA)),
                                jnp.pad(mu[3 * RW_W + RW_DECAY_LORA:3 * RW_W + RW_DECAY_LORA + RW_AAA_LORA],
                                        (0, LANES - RW_AAA_LORA)),
                                mu[3 * RW_W + RW_DECAY_LORA + RW_AAA_LORA:]])[None, :]
        wup = jnp.pad(rw_w_up[l], ((0, LANES - RW_DECAY_LORA), (0, 0))).astype(BF16)
        aup = jnp.pad(rw_a_up[l], ((0, LANES - RW_AAA_LORA), (0, 0))).astype(BF16)
        r, wdec, k2, vr, kk, kka, g, bonus = _rwkv_prep(
            xb.reshape(bsz, seq, d), wrw, mu_p, rw_w0[l][None, :], wup, rw_a0[l][None, :], aup,
            rw_g_up[l].astype(BF16), rw_kk[l][None, :], rw_ka[l][None, :], rw_rk[l].reshape(1, RW_W), bd,
            t["rwkv_prep"])

        def to_scan(a):
            return a.reshape(bsz, seq, RW_HEADS, RW_DH).transpose(1, 3, 0, 2).reshape(seq, RW_DH, bsz * RW_HEADS)

        nl = bsz * RW_HEADS
        nl_pad = -(-nl // LANES) * LANES
        ops = [to_scan(a) for a in (r, wdec, k2, vr, kk, kka)]
        if nl_pad != nl:
            ops = [jnp.pad(a, ((0, 0), (0, 0), (0, nl_pad - nl))) for a in ops]
        o_scan = _rwkv_scan(*ops, t["scan_steps"])[:, :, :nl]
        o_rwkv = o_scan.reshape(seq, RW_DH, bsz, RW_HEADS).transpose(2, 0, 3, 1).reshape(n, RW_W)

        kv = _matmul(memb, ca_w_kv[l].astype(BF16), min(512, bsz * mem_len)).reshape(bsz, mem_len, 2 * CA_W)
        kt = kv[:, :, :CA_W].transpose(0, 2, 1).astype(BF16)
        vm = kv[:, :, CA_W:].astype(BF16)
        h_ca = _xattn(xb.reshape(bsz, seq, d), w[:, o_ca:o_gate].astype(BF16), kt, vm, t["xattn"])

        rw_pad = _pad_cols(router_w[l], LANES)
        rw_hi = rw_pad.astype(BF16)
        rw_lo = (rw_pad - rw_hi.astype(F32)).astype(BF16)
        rb = jnp.concatenate([router_b[l], jnp.full((LANES - N_EXPERTS,), NEG_BIG, F32)])[None, :]
        consts = (w[:, o_gate:].astype(BF16), gate_b[l][None, :], bd, rw_ln_g[l][None, :], rw_ln_b[l][None, :],
                  w_br_ml[l].astype(BF16), w_br_rw[l].astype(BF16), w_br_ca[l].astype(BF16), w_o[l].astype(BF16),
                  ln1_g[l][None, :], ln1_b[l][None, :], rw_hi, rw_lo, rb)
        x1, x1b, route, counts = _merge(xf, xb, h_ml.reshape(n, ML_W), o_rwkv, bonus.reshape(n, RW_W),
                                        g.reshape(n, RW_W), h_ca.reshape(n, CA_W), consts, t["merge"], alpha)
        del x1b

        cnt = counts[0, :N_EXPERTS].astype(jnp.int32)
        blocks_per = (cnt + block - 1) // block
        blk_end = jnp.cumsum(blocks_per)
        slot_start = (blk_end - blocks_per) * block
        e_idx = route[:, ROUTE_IDX:ROUTE_IDX + TOP_K].astype(jnp.int32)
        rank = route[:, ROUTE_RANK:ROUTE_RANK + TOP_K].astype(jnp.int32)
        dest = (slot_start[e_idx] + rank).reshape(n_asg)
        blk_ids = jnp.arange(n_blocks, dtype=jnp.int32)
        blk_e = jnp.minimum(jnp.sum((blk_ids[:, None] >= blk_end[None, :]).astype(jnp.int32), axis=1),
                            N_EXPERTS - 1)
        nused = blk_end[-1:].astype(jnp.int32)

        xd = _dispatch(dest, x1, jnp.zeros((n_rows, d), F32), t["moe_rows"])
        w_glu, w_lin = _deinterleave(w_gu[l])
        y_disp = _experts(blk_e, nused, xd, w_glu, w_lin,
                          b_gu[l][:, None, 0::2], b_gu[l][:, None, 1::2], w_dn[l].astype(BF16),
                          b_dn[l][:, None, :], block)
        xf, xb = _combine(dest, route, x1, y_disp, ln2_g[l], ln2_b[l], t["moe_rows"], alpha)

    del d_ff
    return xf.reshape(bsz, seq, d)
```

```python
import functools

import jax
import jax.numpy as jnp
from jax import lax
from jax.experimental import pallas as pl
from jax.experimental.pallas import tpu as pltpu

ML_HEADS, ML_DQK, ML_DV, ML_CONV = 4, 64, 128, 4
ML_W = ML_HEADS * ML_DV
ML_QK_W = 2 * ML_HEADS * ML_DQK
RW_HEADS, RW_DH = 8, 64
RW_W = RW_HEADS * RW_DH
RW_DECAY_LORA, RW_AAA_LORA, RW_GATE_LORA = 64, 64, 128
RW_GN_EPS = 64e-5
CA_HEADS, CA_DH = 4, 128
CA_W = CA_HEADS * CA_DH
N_BRANCH = 3
N_EXPERTS, TOP_K = 32, 4
SWIGLU_LIMIT, SWIGLU_ALPHA = 7.0, 1.702
LN_EPS = 1e-5

LANES = 128
SUBLANES = 8
VMEM_LIMIT = 56 * 1024 * 1024

BF16 = jnp.bfloat16
F32 = jnp.float32
NEG_BIG = -1e30


def _dot(a, b):
    return jnp.dot(a.astype(BF16), b.astype(BF16), preferred_element_type=F32)


def _split3(a):
    hi = a.astype(BF16)
    r1 = a - hi.astype(F32)
    mid = r1.astype(BF16)
    lo = (r1 - mid.astype(F32)).astype(BF16)
    return hi, mid, lo


def _dot2_rhs(a, b_bf16):
    hi = a.astype(BF16)
    lo = (a - hi.astype(F32)).astype(BF16)
    d = functools.partial(jnp.dot, preferred_element_type=F32)
    return d(hi, b_bf16) + d(lo, b_bf16)


def _head_sum(x, red, bc):
    return _dot2_rhs(_dot2_rhs(x, red), bc)


def _dot_exact_lhs(a_bf16, b):
    hi, mid, lo = _split3(b)
    d = functools.partial(jnp.dot, preferred_element_type=F32)
    return d(a_bf16, hi) + d(a_bf16, mid) + d(a_bf16, lo)


def _sigmoid(x):
    return 1.0 / (1.0 + jnp.exp(-x))


def _softplus(x):
    return jnp.maximum(x, 0.0) + jnp.log1p(jnp.exp(-jnp.abs(x)))


def _layer_norm(v, g, b, eps=LN_EPS):
    mu = jnp.mean(v, axis=-1, keepdims=True)
    c = v - mu
    var = jnp.mean(c * c, axis=-1, keepdims=True)
    return c * lax.rsqrt(var + eps) * g + b


def _params(*sem):
    return pltpu.CompilerParams(dimension_semantics=sem, vmem_limit_bytes=VMEM_LIMIT)


def _const_spec(shape):
    nd = len(shape)
    return pl.BlockSpec(shape, lambda *_: (0,) * nd, pipeline_mode=pl.Buffered(1))


def _ln_kernel(x_ref, g_ref, b_ref, o_ref, ob_ref):
    y = _layer_norm(x_ref[...], g_ref[...], b_ref[...])
    o_ref[...] = y
    ob_ref[...] = y.astype(BF16)


def _ln_rows(x2d, g, b, tile):
    n, d = x2d.shape
    return pl.pallas_call(
        _ln_kernel,
        grid=(n // tile,),
        in_specs=[pl.BlockSpec((tile, d), lambda i: (i, 0)), _const_spec((1, d)), _const_spec((1, d))],
        out_specs=[pl.BlockSpec((tile, d), lambda i: (i, 0)), pl.BlockSpec((tile, d), lambda i: (i, 0))],
        out_shape=[jax.ShapeDtypeStruct((n, d), F32), jax.ShapeDtypeStruct((n, d), BF16)],
        compiler_params=_params("parallel"),
        name="ln_rows",
    )(x2d, g.reshape(1, d), b.reshape(1, d))


def _mm_kernel(a_ref, b_ref, o_ref):
    o_ref[...] = jnp.dot(a_ref[...], b_ref[...], preferred_element_type=F32)


def _matmul(a, b, tile):
    m, k = a.shape
    n = b.shape[1]
    return pl.pallas_call(
        _mm_kernel,
        grid=(m // tile,),
        in_specs=[pl.BlockSpec((tile, k), lambda i: (i, 0)), _const_spec((k, n))],
        out_specs=pl.BlockSpec((tile, n), lambda i: (i, 0)),
        out_shape=jax.ShapeDtypeStruct((m, n), F32),
        compiler_params=_params("parallel"),
        name="matmul",
    )(a, b)


def _mlstm_kernel(xb_ref, wml_ref, wg_ref, convw_ref, convb_ref, gb_ref, normg_ref, tri_ref,
                  h_ref, ubuf, c_ref, n_ref, m_ref, *, tile, chunk):
    @pl.when(pl.program_id(1) == 0)
    def _():
        ubuf[0:SUBLANES, :] = jnp.zeros((SUBLANES, ML_QK_W), F32)
        c_ref[...] = jnp.zeros_like(c_ref)
        n_ref[...] = jnp.zeros_like(n_ref)
        m_ref[...] = jnp.zeros_like(m_ref)

    xb = xb_ref[...]
    u = jnp.dot(xb, wml_ref[...], preferred_element_type=F32)
    gates = jnp.dot(xb, wg_ref[...], preferred_element_type=F32) + gb_ref[...]

    ubuf[SUBLANES:SUBLANES + tile, :] = u[:, :ML_QK_W]
    acc = jnp.broadcast_to(convb_ref[...], (tile, ML_QK_W))
    for j in range(ML_CONV):
        acc = acc + convw_ref[j:j + 1, :] * ubuf[pl.ds(SUBLANES - ML_CONV + 1 + j, tile), :]
    ubuf[0:SUBLANES, :] = ubuf[tile:tile + SUBLANES, :]
    qk = acc * _sigmoid(acc)
    q = qk[:, :ML_HEADS * ML_DQK] * (ML_DQK ** -0.5)
    k = qk[:, ML_HEADS * ML_DQK:]
    v = u[:, ML_QK_W:ML_QK_W + ML_W]
    og = u[:, ML_QK_W + ML_W:]

    lane = lax.broadcasted_iota(jnp.int32, (chunk, LANES), 1)
    row = lax.broadcasted_iota(jnp.int32, (chunk, chunk), 0)
    col = lax.broadcasted_iota(jnp.int32, (chunk, chunk), 1)
    causal = row >= col
    log_f = -_softplus(-gates)

    for c in range(tile // chunk):
        rs = slice(c * chunk, (c + 1) * chunk)
        bcum = _dot_exact_lhs(tri_ref[...], log_f[rs])
        mcol = jnp.where(lane < ML_HEADS, gates[rs], bcum)
        mrow = mcol.T
        for h in range(ML_HEADS):
            i_col = mcol[:, h:h + 1]
            b_col = mcol[:, ML_HEADS + h:ML_HEADS + h + 1]
            i_row = mrow[h:h + 1, :]
            b_row = mrow[ML_HEADS + h:ML_HEADS + h + 1, :]
            m_prev = m_ref[h][0:1, 0:1]
            n_prev = n_ref[h][0:1, :]
            c_prev = c_ref[h]
            qh = q[rs, h * ML_DQK:(h + 1) * ML_DQK]
            kh = k[rs, h * ML_DQK:(h + 1) * ML_DQK]
            vh = v[rs, h * ML_DV:(h + 1) * ML_DV]

            dmat = jnp.where(causal, b_col - b_row + i_row, -jnp.inf)
            m_inter = b_col + m_prev
            m_t = jnp.maximum(m_inter, jnp.max(dmat, axis=1, keepdims=True))
            s = lax.dot_general(qh.astype(BF16), kh.astype(BF16), (((1,), (1,)), ((), ())),
                                preferred_element_type=F32)
            w_intra = jnp.exp(dmat - m_t) * s
            s_inter = jnp.exp(m_inter - m_t)
            num = s_inter * _dot(qh, c_prev) + _dot(w_intra, vh)
            den = (s_inter * jnp.sum(qh * n_prev, axis=1, keepdims=True)
                   + jnp.sum(w_intra, axis=1, keepdims=True))
            hh = num / jnp.maximum(jnp.abs(den), jnp.exp(-m_t))

            b_last = b_col[chunk - 1:chunk, :]
            g_col = b_last - b_col + i_col
            m_new = jnp.maximum(b_last + m_prev, jnp.max(g_col, axis=0, keepdims=True))
            carry = jnp.exp(b_last + m_prev - m_new)
            kw = kh * jnp.exp(g_col - m_new)
            c_ref[h] = carry * c_prev + lax.dot_general(
                kw.astype(BF16), vh.astype(BF16), (((0,), (0,)), ((), ())), preferred_element_type=F32)
            n_new = carry * n_prev + jnp.sum(kw, axis=0, keepdims=True)
            n_ref[h] = jnp.broadcast_to(n_new, (SUBLANES, ML_DQK))
            m_ref[h] = jnp.broadcast_to(m_new, (SUBLANES, LANES))

            mu = jnp.mean(hh, axis=1, keepdims=True)
            cen = hh - mu
            var = jnp.mean(cen * cen, axis=1, keepdims=True)
            y = cen * lax.rsqrt(var + LN_EPS) * normg_ref[:, h * ML_DV:(h + 1) * ML_DV]
            h_ref[rs, h * ML_DV:(h + 1) * ML_DV] = (
                _sigmoid(og[rs, h * ML_DV:(h + 1) * ML_DV]) * y).astype(BF16)


def _mlstm(xb, wml, wg, convw, convb, gb, normg, tile, chunk):
    bsz, s, d = xb.shape
    tri = (jnp.arange(chunk)[:, None] >= jnp.arange(chunk)[None, :]).astype(BF16)
    kern = functools.partial(_mlstm_kernel, tile=tile, chunk=chunk)
    return pl.pallas_call(
        kern,
        grid=(bsz, s // tile),
        in_specs=[pl.BlockSpec((None, tile, d), lambda b, j: (b, j, 0)),
                  _const_spec(wml.shape), _const_spec(wg.shape), _const_spec(convw.shape),
                  _const_spec(convb.shape), _const_spec(gb.shape), _const_spec(normg.shape),
                  _const_spec(tri.shape)],
        out_specs=pl.BlockSpec((None, tile, ML_W), lambda b, j: (b, j, 0)),
        out_shape=jax.ShapeDtypeStruct((bsz, s, ML_W), BF16),
        scratch_shapes=[pltpu.VMEM((tile + SUBLANES, ML_QK_W), F32),
                        pltpu.VMEM((ML_HEADS, ML_DQK, ML_DV), F32),
                        pltpu.VMEM((ML_HEADS, SUBLANES, ML_DQK), F32),
                        pltpu.VMEM((ML_HEADS, SUBLANES, LANES), F32)],
        compiler_params=_params("parallel", "arbitrary"),
        name="mlstm",
    )(xb, wml, wg, convw, convb, gb, normg, tri)


RW_PAD_IN = 3 * RW_W + 3 * LANES


def _rwkv_prep_kernel(xb_ref, wrw_ref, mu_ref, w0_ref, wup_ref, a0_ref, aup_ref, gup_ref, kkw_ref,
                      ka_ref, rk_ref, red_ref, bc_ref,
                      r_ref, w_ref, k_ref, v_ref, kk_ref, kka_ref, g_ref, bonus_ref, ubuf, *, tile):
    @pl.when(pl.program_id(1) == 0)
    def _():
        ubuf[0:SUBLANES, :] = jnp.zeros((SUBLANES, RW_PAD_IN), F32)

    u = jnp.dot(xb_ref[...], wrw_ref[...], preferred_element_type=F32)
    ubuf[SUBLANES:SUBLANES + tile, :] = u
    u_prev = ubuf[pl.ds(SUBLANES - 1, tile), :]
    ubuf[0:SUBLANES, :] = ubuf[tile:tile + SUBLANES, :]
    us = u + (u_prev - u) * mu_ref[...]
    r = us[:, 0:RW_W]
    kr = us[:, RW_W:2 * RW_W]
    vr = us[:, 2 * RW_W:3 * RW_W]
    wd = us[:, 3 * RW_W:3 * RW_W + LANES]
    ad = us[:, 3 * RW_W + LANES:3 * RW_W + 2 * LANES]
    gd = us[:, 3 * RW_W + 2 * LANES:]

    w_log = -_softplus(-(w0_ref[...] + _dot(jnp.tanh(wd), wup_ref[...]))) - 0.5
    decay = jnp.exp(-jnp.exp(w_log))
    a = _sigmoid(a0_ref[...] + _dot(ad, aup_ref[...]))
    g = _dot(_sigmoid(gd), gup_ref[...])

    red, bc = red_ref[...], bc_ref[...]
    kk = kr * kkw_ref[...]
    kk = kk * lax.rsqrt(jnp.maximum(_head_sum(kk * kk, red, bc), 1e-24))
    k2 = kr * (1.0 + (a - 1.0) * ka_ref[...])
    bonus = _head_sum(r * k2 * rk_ref[...], red, bc) * vr

    r_ref[...] = r
    w_ref[...] = decay
    k_ref[...] = k2
    v_ref[...] = vr
    kk_ref[...] = kk
    kka_ref[...] = kk * a
    g_ref[...] = g
    bonus_ref[...] = bonus


def _rwkv_prep(xb, wrw, mu, w0, wup, a0, aup, gup, kkw, ka, rk, red, bc, tile):
    bsz, s, d = xb.shape
    consts = (wrw, mu, w0, wup, a0, aup, gup, kkw, ka, rk, red, bc)
    spec = pl.BlockSpec((None, tile, RW_W), lambda b, j: (b, j, 0))
    tspec = pl.BlockSpec((tile, RW_W), lambda b, j: (j, b))
    return pl.pallas_call(
        functools.partial(_rwkv_prep_kernel, tile=tile),
        grid=(bsz, s // tile),
        in_specs=[pl.BlockSpec((None, tile, d), lambda b, j: (b, j, 0))] + [_const_spec(c.shape) for c in consts],
        out_specs=[tspec] * 6 + [spec] * 2,
        out_shape=[jax.ShapeDtypeStruct((s, bsz * RW_W), F32)] * 6 + [jax.ShapeDtypeStruct((bsz, s, RW_W), F32)] * 2,
        scratch_shapes=[pltpu.VMEM((tile + SUBLANES, RW_PAD_IN), F32)],
        compiler_params=_params("parallel", "arbitrary"),
        name="rwkv_prep",
    )(xb, *consts)


def _rwkv_scan_kernel(r_ref, w_ref, k_ref, v_ref, kk_ref, kka_ref, o_ref, st_ref, *, steps):
    @pl.when(pl.program_id(1) == 0)
    def _():
        st_ref[...] = jnp.zeros_like(st_ref)

    zeros = jnp.zeros((RW_DH, LANES), F32)

    def first(kc, sa):
        return sa + st_ref[kc] * kk_ref[0, pl.ds(kc, 1), :]

    sa0 = lax.fori_loop(0, RW_DH, first, zeros, unroll=8)

    def step(t, sa):
        vt = v_ref[t]
        tn = jnp.minimum(t + 1, steps - 1)

        def body(kc, carry):
            out, sa_next = carry
            row = pl.ds(kc, 1)
            new = st_ref[kc] * w_ref[t, row, :] - sa * kka_ref[t, row, :] + vt * k_ref[t, row, :]
            st_ref[kc] = new
            return out + new * r_ref[t, row, :], sa_next + new * kk_ref[tn, row, :]

        out, sa_next = lax.fori_loop(0, RW_DH, body, (zeros, zeros), unroll=16)
        o_ref[t] = out
        return sa_next

    lax.fori_loop(0, steps, step, sa0)


def _rwkv_scan(r, w, k, v, kk, kka, steps):
    s, dh, nl = r.shape
    spec = pl.BlockSpec((steps, dh, LANES), lambda g, j: (j, 0, g))
    return pl.pallas_call(
        functools.partial(_rwkv_scan_kernel, steps=steps),
        grid=(nl // LANES, s // steps),
        in_specs=[spec] * 6,
        out_specs=spec,
        out_shape=jax.ShapeDtypeStruct((s, dh, nl), F32),
        scratch_shapes=[pltpu.VMEM((dh, dh, LANES), F32)],
        compiler_params=_params("parallel", "arbitrary"),
        name="rwkv_scan",
    )(r, w, k, v, kk, kka)


def _xattn_kernel(xb_ref, wq_ref, kt_ref, v_ref, o_ref):
    q = jnp.dot(xb_ref[...], wq_ref[...], preferred_element_type=F32)
    for h in range(CA_HEADS):
        hs = slice(h * CA_DH, (h + 1) * CA_DH)
        s = jnp.dot(q[:, hs].astype(BF16), kt_ref[hs, :], preferred_element_type=F32) * (CA_DH ** -0.5)
        p = jnp.exp(s - jnp.max(s, axis=1, keepdims=True))
        den = jnp.sum(p, axis=1, keepdims=True)
        o = jnp.dot(p.astype(BF16), v_ref[:, hs], preferred_element_type=F32) / den
        o_ref[:, hs] = o.astype(BF16)


def _xattn(xb, wq, kt, v, tile):
    bsz, s, d = xb.shape
    m = v.shape[1]
    return pl.pallas_call(
        _xattn_kernel,
        grid=(bsz, s // tile),
        in_specs=[pl.BlockSpec((None, tile, d), lambda b, j: (b, j, 0)), _const_spec(wq.shape),
                  pl.BlockSpec((None, CA_W, m), lambda b, j: (b, 0, 0)),
                  pl.BlockSpec((None, m, CA_W), lambda b, j: (b, 0, 0))],
        out_specs=pl.BlockSpec((None, tile, CA_W), lambda b, j: (b, j, 0)),
        out_shape=jax.ShapeDtypeStruct((bsz, s, CA_W), BF16),
        compiler_params=_params("parallel", "parallel"),
        name="xattn",
    )(xb, wq, kt, v)


ROUTE_IDX, ROUTE_RANK, ROUTE_GATE = 0, TOP_K, 2 * TOP_K


def _merge_kernel(x_ref, xb_ref, hml_ref, o_ref, bonus_ref, g_ref, hca_ref,
                  wgate_ref, gateb_ref, red_ref, bc_ref, rwg_ref, rwb_ref, wml_ref, wrw_ref, wca_ref, wo_ref,
                  lng_ref, lnb_ref, rwhi_ref, rwlo_ref, rb_ref, tri_ref,
                  x1_ref, x1b_ref, route_ref, cnt_ref, carry_ref, *, tile, alpha, d_model):
    @pl.when(pl.program_id(0) == 0)
    def _():
        carry_ref[...] = jnp.zeros_like(carry_ref)

    red, bc = red_ref[...], bc_ref[...]
    o = o_ref[...]
    mu = _head_sum(o, red, bc) * (1.0 / RW_DH)
    cen = o - mu
    var = _head_sum(cen * cen, red, bc) * (1.0 / RW_DH)
    h_rw = (cen * lax.rsqrt(var + RW_GN_EPS) * rwg_ref[...] + rwb_ref[...] + bonus_ref[...]) * g_ref[...]

    gates = _sigmoid(jnp.dot(xb_ref[...], wgate_ref[...], preferred_element_type=F32) + gateb_ref[...])
    y = (gates[:, 0:d_model] * jnp.dot(hml_ref[...], wml_ref[...], preferred_element_type=F32)
         + gates[:, d_model:2 * d_model] * _dot(h_rw, wrw_ref[...])
         + gates[:, 2 * d_model:] * jnp.dot(hca_ref[...], wca_ref[...], preferred_element_type=F32))
    mixed = _dot(y, wo_ref[...])
    x1 = _layer_norm(alpha * x_ref[...] + mixed, lng_ref[...], lnb_ref[...])
    x1_ref[...] = x1
    x1b_ref[...] = x1.astype(BF16)

    hi = x1.astype(BF16)
    lo = (x1 - hi.astype(F32)).astype(BF16)
    d = functools.partial(jnp.dot, preferred_element_type=F32)
    logits = d(hi, rwhi_ref[...]) + d(lo, rwhi_ref[...]) + d(hi, rwlo_ref[...]) + rb_ref[...]

    lane = lax.broadcasted_iota(jnp.int32, (tile, LANES), 1)
    vals = logits
    tops, onehots = [], []
    for _ in range(TOP_K):
        m = jnp.max(vals, axis=1, keepdims=True)
        idx = jnp.min(jnp.where(vals == m, lane, LANES), axis=1, keepdims=True)
        sel = lane == idx
        tops.append((m, idx))
        onehots.append(sel.astype(F32))
        vals = jnp.where(sel, -jnp.inf, vals)
    exps = [jnp.exp(m - tops[0][0]) for m, _ in tops]
    den = exps[0] + exps[1] + exps[2] + exps[3]

    cnt = onehots[0] + onehots[1] + onehots[2] + onehots[3]
    before = jnp.dot(tri_ref[...], cnt.astype(BF16), preferred_element_type=F32) + carry_ref[0:1, :]
    route = jnp.zeros((tile, LANES), F32)
    for kq in range(TOP_K):
        rank = jnp.sum(onehots[kq] * before, axis=1, keepdims=True)
        route = jnp.where(lane == ROUTE_IDX + kq, tops[kq][1].astype(F32), route)
        route = jnp.where(lane == ROUTE_RANK + kq, rank, route)
        route = jnp.where(lane == ROUTE_GATE + kq, exps[kq] / den, route)
    route_ref[...] = route
    total = carry_ref[0:1, :] + jnp.sum(cnt, axis=0, keepdims=True)
    carry_ref[...] = jnp.broadcast_to(total, carry_ref.shape)
    cnt_ref[...] = jnp.broadcast_to(total, cnt_ref.shape)


def _merge(x, xb, hml, o_tm, bonus, g, hca, consts, tile, alpha):
    n, d = x.shape
    tiles_per_seq = o_tm.shape[0] // tile
    tri = (jnp.arange(tile)[:, None] > jnp.arange(tile)[None, :]).astype(BF16)
    consts = tuple(consts) + (tri,)

    def rows(w):
        return pl.BlockSpec((tile, w), lambda i: (i, 0))

    o_spec = pl.BlockSpec((tile, RW_W), lambda i: (i % tiles_per_seq, i // tiles_per_seq))
    return pl.pallas_call(
        functools.partial(_merge_kernel, tile=tile, alpha=alpha, d_model=d),
        grid=(n // tile,),
        in_specs=[rows(d), rows(d), rows(ML_W), o_spec, rows(RW_W), rows(RW_W), rows(CA_W)]
        + [_const_spec(c.shape) for c in consts],
        out_specs=[rows(d), rows(d), rows(LANES), _const_spec((SUBLANES, LANES))],
        out_shape=[jax.ShapeDtypeStruct((n, d), F32), jax.ShapeDtypeStruct((n, d), BF16),
                   jax.ShapeDtypeStruct((n, LANES), F32), jax.ShapeDtypeStruct((SUBLANES, LANES), F32)],
        scratch_shapes=[pltpu.VMEM((SUBLANES, LANES), F32)],
        compiler_params=_params("arbitrary"),
        name="merge_route",
    )(x, xb, hml, o_tm, bonus, g, hca, *consts)


def _row_copy(src_ref, src_row, dst_ref, dst_row, sem):
    return pltpu.make_async_copy(src_ref.at[pl.ds(src_row, 1)], dst_ref.at[pl.ds(dst_row, 1)], sem)


def _dispatch_kernel(dest_ref, zero_blk_ref, x_ref, xd_ref, xbuf, zbuf, sems, zsem, *, tile, nsteps, block):
    i = pl.program_id(0)
    slot = i % 2

    @pl.when(i == 0)
    def _():
        zbuf[...] = jnp.zeros_like(zbuf)
        n_zero = zero_blk_ref.shape[0]

        def fresh(j):
            return jnp.logical_or(j == 0, zero_blk_ref[j] != zero_blk_ref[jnp.maximum(j - 1, 0)])

        def zero_copy(j):
            return pltpu.make_async_copy(zbuf, xd_ref.at[pl.ds(zero_blk_ref[j] * block, block)], zsem)

        def start(j, carry):
            @pl.when(fresh(j))
            def _():
                zero_copy(j).start()
            return carry

        def wait(j, carry):
            @pl.when(fresh(j))
            def _():
                zero_copy(j).wait()
            return carry

        lax.fori_loop(0, n_zero, start, 0)
        lax.fori_loop(0, n_zero, wait, 0)

    def retire(s):
        for _ in range(TOP_K):
            pltpu.make_async_copy(xbuf.at[s], xd_ref.at[pl.ds(0, tile)], sems.at[s]).wait()

    @pl.when(i >= 2)
    def _():
        retire(slot)

    xbuf[slot] = x_ref[...]

    def issue(r, carry):
        for kq in range(TOP_K):
            _row_copy(xbuf.at[slot], r, xd_ref, dest_ref[r * TOP_K + kq], sems.at[slot]).start()
        return carry

    lax.fori_loop(0, tile, issue, 0)

    @pl.when(i == nsteps - 1)
    def _():
        retire(slot)
        if nsteps > 1:
            retire(1 - slot)


def _dispatch(dest_flat, zero_blk, x1, n_rows, tile, block):
    n, d = x1.shape
    nsteps = n // tile
    return pl.pallas_call(
        functools.partial(_dispatch_kernel, tile=tile, nsteps=nsteps, block=block),
        grid=(nsteps,),
        in_specs=[pl.BlockSpec((tile * TOP_K,), lambda i: (i,), memory_space=pltpu.SMEM),
                  pl.BlockSpec(memory_space=pltpu.SMEM),
                  pl.BlockSpec((tile, d), lambda i: (i, 0))],
        out_specs=pl.BlockSpec(memory_space=pl.ANY),
        out_shape=jax.ShapeDtypeStruct((n_rows, d), F32),
        scratch_shapes=[pltpu.VMEM((2, tile, d), F32), pltpu.VMEM((block, d), F32),
                        pltpu.SemaphoreType.DMA((2,)), pltpu.SemaphoreType.DMA],
        compiler_params=_params("arbitrary"),
        name="moe_dispatch",
    )(dest_flat, zero_blk, x1)


DEINT_COLS = 512


def _deinterleave_kernel(w_ref, pe_ref, po_ref, glu_ref, lin_ref):
    wb = w_ref[...].astype(BF16)
    glu_ref[...] = jnp.dot(wb, pe_ref[...], preferred_element_type=F32).astype(BF16)
    lin_ref[...] = jnp.dot(wb, po_ref[...], preferred_element_type=F32).astype(BF16)


def _deinterleave(w_gu):
    e, d, two_ff = w_gu.shape
    half = DEINT_COLS // 2
    src = jnp.arange(DEINT_COLS)[:, None]
    dst = jnp.arange(half)[None, :]
    pe = (src == 2 * dst).astype(BF16)
    po = (src == 2 * dst + 1).astype(BF16)
    out = jax.ShapeDtypeStruct((e, d, two_ff // 2), BF16)
    return pl.pallas_call(
        _deinterleave_kernel,
        grid=(e, two_ff // DEINT_COLS),
        in_specs=[pl.BlockSpec((None, d, DEINT_COLS), lambda i, c: (i, 0, c)),
                  _const_spec(pe.shape), _const_spec(po.shape)],
        out_specs=[pl.BlockSpec((None, d, half), lambda i, c: (i, 0, c))] * 2,
        out_shape=[out, out],
        compiler_params=_params("parallel", "parallel"),
        name="deinterleave_w",
    )(w_gu, pe, po)


def _expert_kernel(blk_e_ref, nused_ref, xd_ref, wglu_ref, wlin_ref, bglu_ref, blin_ref, wdn_ref, bdn_ref, y_ref):
    del blk_e_ref
    live = pl.program_id(0) < nused_ref[0]

    @pl.when(live)
    def _():
        xb = xd_ref[...].astype(BF16)
        h_glu = jnp.dot(xb, wglu_ref[...], preferred_element_type=F32) + bglu_ref[...]
        h_lin = jnp.dot(xb, wlin_ref[...], preferred_element_type=F32) + blin_ref[...]
        x_glu = jnp.minimum(h_glu, SWIGLU_LIMIT)
        x_lin = jnp.clip(h_lin, -SWIGLU_LIMIT, SWIGLU_LIMIT)
        act = x_glu * _sigmoid(SWIGLU_ALPHA * x_glu) * (x_lin + 1.0)
        y_ref[...] = _dot(act, wdn_ref[...]) + bdn_ref[...]

    @pl.when(jnp.logical_not(live))
    def _():
        y_ref[...] = jnp.zeros_like(y_ref)


def _experts(blk_e, nused, xd, wglu, wlin, bglu, blin, wdn, bdn, block):
    n_rows, d = xd.shape
    dff = wglu.shape[2]
    grid_spec = pltpu.PrefetchScalarGridSpec(
        num_scalar_prefetch=2,
        grid=(n_rows // block,),
        in_specs=[pl.BlockSpec((block, d), lambda i, be, nu: (jnp.minimum(i, nu[0] - 1), 0)),
                  pl.BlockSpec((None, d, dff), lambda i, be, nu: (be[i], 0, 0)),
                  pl.BlockSpec((None, d, dff), lambda i, be, nu: (be[i], 0, 0)),
                  pl.BlockSpec((None, 1, dff), lambda i, be, nu: (be[i], 0, 0)),
                  pl.BlockSpec((None, 1, dff), lambda i, be, nu: (be[i], 0, 0)),
                  pl.BlockSpec((None, dff, d), lambda i, be, nu: (be[i], 0, 0)),
                  pl.BlockSpec((None, 1, d), lambda i, be, nu: (be[i], 0, 0))],
        out_specs=pl.BlockSpec((block, d), lambda i, be, nu: (i, 0)),
    )
    return pl.pallas_call(
        _expert_kernel,
        grid_spec=grid_spec,
        out_shape=jax.ShapeDtypeStruct((n_rows, d), F32),
        compiler_params=_params("arbitrary"),
        name="moe_experts",
    )(blk_e, nused, xd, wglu, wlin, bglu, blin, wdn, bdn)


def _combine_kernel(dest_ref, dest_next_ref, route_ref, x1_ref, y_ref, lng_ref, lnb_ref, x2_ref, x2b_ref,
                    ybuf, sems, *, tile, alpha, nsteps):
    i = pl.program_id(0)
    slot = i % 2

    def fetch(d_ref, s):
        def issue(r, carry):
            for kq in range(TOP_K):
                _row_copy(y_ref, d_ref[r * TOP_K + kq], ybuf.at[s, kq], r, sems.at[s]).start()
            return carry

        lax.fori_loop(0, tile, issue, 0)

    @pl.when(i == 0)
    def _():
        fetch(dest_ref, slot)

    @pl.when(i + 1 < nsteps)
    def _():
        fetch(dest_next_ref, 1 - slot)

    for kq in range(TOP_K):
        pltpu.make_async_copy(y_ref.at[pl.ds(0, tile)], ybuf.at[slot, kq], sems.at[slot]).wait()
    route = route_ref[...]
    moe = route[:, ROUTE_GATE:ROUTE_GATE + 1] * ybuf[slot, 0]
    for kq in range(1, TOP_K):
        moe = moe + route[:, ROUTE_GATE + kq:ROUTE_GATE + kq + 1] * ybuf[slot, kq]
    x2 = _layer_norm(alpha * x1_ref[...] + moe, lng_ref[...], lnb_ref[...])
    x2_ref[...] = x2
    x2b_ref[...] = x2.astype(BF16)


def _combine(dest_flat, route, x1, y_disp, lng, lnb, tile, alpha):
    n, d = x1.shape
    nsteps = n // tile
    return pl.pallas_call(
        functools.partial(_combine_kernel, tile=tile, alpha=alpha, nsteps=nsteps),
        grid=(nsteps,),
        in_specs=[pl.BlockSpec((tile * TOP_K,), lambda i: (i,), memory_space=pltpu.SMEM),
                  pl.BlockSpec((tile * TOP_K,), lambda i: (jnp.minimum(i + 1, nsteps - 1),),
                               memory_space=pltpu.SMEM),
                  pl.BlockSpec((tile, LANES), lambda i: (i, 0)),
                  pl.BlockSpec((tile, d), lambda i: (i, 0)),
                  pl.BlockSpec(memory_space=pl.ANY),
                  _const_spec((1, d)), _const_spec((1, d))],
        out_specs=[pl.BlockSpec((tile, d), lambda i: (i, 0)), pl.BlockSpec((tile, d), lambda i: (i, 0))],
        out_shape=[jax.ShapeDtypeStruct((n, d), F32), jax.ShapeDtypeStruct((n, d), BF16)],
        scratch_shapes=[pltpu.VMEM((2, TOP_K, tile, d), F32), pltpu.SemaphoreType.DMA((2,))],
        compiler_params=_params("arbitrary"),
        name="moe_combine",
    )(dest_flat, dest_flat, route, x1, y_disp, lng.reshape(1, d), lnb.reshape(1, d))


def _tiles(bsz, seq):
    n = bsz * seq
    return dict(
        ln=min(1024, n),
        mlstm=min(256, seq), mlstm_chunk=min(128, seq),
        rwkv_prep=min(256, seq),
        scan_steps=min(32, seq),
        xattn=min(512, seq),
        merge=min(512, seq),
        moe_rows=min(256, n),
        moe_block=512,
    )


def _pad_cols(w, width):
    return jnp.pad(w, ((0, 0), (0, width - w.shape[1])))


def kernel(x, mem, ln_in_g, ln_in_b, mem_ln_g, mem_ln_b, w_in, ml_conv_w, ml_conv_b, ml_ig_b, ml_fg_b, ml_norm_g, rw_mu, rw_w0, rw_w_up, rw_a0, rw_a_up, rw_g_up, rw_kk, rw_ka, rw_rk, rw_ln_g, rw_ln_b, ca_w_kv, gate_b, w_br_ml, w_br_rw, w_br_ca, w_o, ln1_g, ln1_b, router_w, router_b, w_gu, b_gu, w_dn, b_dn, ln2_g, ln2_b):
    bsz, seq, d = x.shape
    mem_len = mem.shape[1]
    depth = w_in.shape[0]
    n = bsz * seq
    t = _tiles(bsz, seq)
    alpha = (2 * depth) ** 0.25
    d_ff = w_dn.shape[2]

    xf, xb = _ln_rows(x.reshape(n, d), ln_in_g, ln_in_b, t["ln"])
    _, memb = _ln_rows(mem.reshape(bsz * mem_len, d), mem_ln_g, mem_ln_b, min(t["ln"], bsz * mem_len))

    o_qk, o_v, o_og = 0, ML_QK_W, ML_QK_W + ML_W
    o_ig = o_og + ML_W
    o_fg = o_ig + ML_HEADS
    o_rw = o_fg + ML_HEADS
    o_ca = o_rw + 3 * RW_W + RW_DECAY_LORA + RW_AAA_LORA + RW_GATE_LORA
    o_gate = o_ca + CA_W

    head_of_lane = jnp.arange(RW_W) // RW_DH
    red = (head_of_lane[:, None] == jnp.arange(LANES)[None, :]).astype(BF16)
    bc = red.T
    n_asg = n * TOP_K
    block = t["moe_block"]
    n_blocks = -(-n_asg // block) + N_EXPERTS
    n_rows = n_blocks * block

    for l in range(depth):
        w = w_in[l]
        wml = w[:, o_qk:o_ig].astype(BF16)
        wg = _pad_cols(w[:, o_ig:o_rw], LANES).astype(BF16)
        gb = _pad_cols(jnp.concatenate([ml_ig_b[l], ml_fg_b[l]])[None, :], LANES)
        h_ml = _mlstm(xb.reshape(bsz, seq, d), wml, wg, ml_conv_w[l], ml_conv_b[l][None, :], gb,
                      ml_norm_g[l][None, :], t["mlstm"], t["mlstm_chunk"])

        o_wd = o_rw + 3 * RW_W
        o_ad = o_wd + RW_DECAY_LORA
        o_gd = o_ad + RW_AAA_LORA
        wrw = jnp.concatenate([w[:, o_rw:o_wd], _pad_cols(w[:, o_wd:o_ad], LANES),
                               _pad_cols(w[:, o_ad:o_gd], LANES), w[:, o_gd:o_ca]], axis=1).astype(BF16)
        mu = rw_mu[l]
        mu_p = jnp.concatenate([mu[:3 * RW_W], jnp.pad(mu[3 * RW_W:3 * RW_W + RW_DECAY_LORA], (0, LANES - RW_DECAY_LORA)),
                                jnp.pad(mu[3 * RW_W + RW_DECAY_LORA:3 * RW_W + RW_DECAY_LORA + RW_AAA_LORA],
                                        (0, LANES - RW_AAA_LORA)),
                                mu[3 * RW_W + RW_DECAY_LORA + RW_AAA_LORA:]])[None, :]
        wup = jnp.pad(rw_w_up[l], ((0, LANES - RW_DECAY_LORA), (0, 0))).astype(BF16)
        aup = jnp.pad(rw_a_up[l], ((0, LANES - RW_AAA_LORA), (0, 0))).astype(BF16)
        r, wdec, k2, vr, kk, kka, g, bonus = _rwkv_prep(
            xb.reshape(bsz, seq, d), wrw, mu_p, rw_w0[l][None, :], wup, rw_a0[l][None, :], aup,
            rw_g_up[l].astype(BF16), rw_kk[l][None, :], rw_ka[l][None, :], rw_rk[l].reshape(1, RW_W), red, bc,
            t["rwkv_prep"])

        def to_scan(a):
            return a.reshape(seq, bsz * RW_HEADS, RW_DH).transpose(0, 2, 1)

        nl = bsz * RW_HEADS
        nl_pad = -(-nl // LANES) * LANES
        ops = [to_scan(a) for a in (r, wdec, k2, vr, kk, kka)]
        if nl_pad != nl:
            ops = [jnp.pad(a, ((0, 0), (0, 0), (0, nl_pad - nl))) for a in ops]
        o_scan = _rwkv_scan(*ops, t["scan_steps"])[:, :, :nl]
        o_rwkv = o_scan.transpose(0, 2, 1).reshape(seq, bsz * RW_W)

        kv = _matmul(memb, ca_w_kv[l].astype(BF16), min(512, bsz * mem_len)).reshape(bsz, mem_len, 2 * CA_W)
        kt = kv[:, :, :CA_W].transpose(0, 2, 1).astype(BF16)
        vm = kv[:, :, CA_W:].astype(BF16)
        h_ca = _xattn(xb.reshape(bsz, seq, d), w[:, o_ca:o_gate].astype(BF16), kt, vm, t["xattn"])

        rw_pad = _pad_cols(router_w[l], LANES)
        rw_hi = rw_pad.astype(BF16)
        rw_lo = (rw_pad - rw_hi.astype(F32)).astype(BF16)
        rb = jnp.concatenate([router_b[l], jnp.full((LANES - N_EXPERTS,), NEG_BIG, F32)])[None, :]
        consts = (w[:, o_gate:].astype(BF16), gate_b[l][None, :], red, bc, rw_ln_g[l][None, :], rw_ln_b[l][None, :],
                  w_br_ml[l].astype(BF16), w_br_rw[l].astype(BF16), w_br_ca[l].astype(BF16), w_o[l].astype(BF16),
                  ln1_g[l][None, :], ln1_b[l][None, :], rw_hi, rw_lo, rb)
        x1, x1b, route, counts = _merge(xf, xb, h_ml.reshape(n, ML_W), o_rwkv, bonus.reshape(n, RW_W),
                                        g.reshape(n, RW_W), h_ca.reshape(n, CA_W), consts, t["merge"], alpha)
        del x1b

        cnt = counts[0, :N_EXPERTS].astype(jnp.int32)
        blocks_per = (cnt + block - 1) // block
        blk_end = jnp.cumsum(blocks_per)
        slot_start = (blk_end - blocks_per) * block
        e_idx = route[:, ROUTE_IDX:ROUTE_IDX + TOP_K].astype(jnp.int32)
        rank = route[:, ROUTE_RANK:ROUTE_RANK + TOP_K].astype(jnp.int32)
        dest = (slot_start[e_idx] + rank).reshape(n_asg)
        blk_ids = jnp.arange(n_blocks, dtype=jnp.int32)
        blk_e = jnp.minimum(jnp.sum((blk_ids[:, None] >= blk_end[None, :]).astype(jnp.int32), axis=1),
                            N_EXPERTS - 1)
        nused = blk_end[-1:].astype(jnp.int32)

        last_blk = jnp.maximum(blk_end - 1, 0).astype(jnp.int32)
        trailing = jnp.minimum(nused[0] + jnp.arange(N_EXPERTS, dtype=jnp.int32), n_blocks - 1)
        zero_blk = jnp.concatenate([last_blk, trailing])
        xd = _dispatch(dest, zero_blk, x1, n_rows, t["moe_rows"], block)
        w_glu, w_lin = _deinterleave(w_gu[l])
        y_disp = _experts(blk_e, nused, xd, w_glu, w_lin,
                          b_gu[l][:, None, 0::2], b_gu[l][:, None, 1::2], w_dn[l].astype(BF16),
                          b_dn[l][:, None, :], block)
        xf, xb = _combine(dest, route, x1, y_disp, ln2_g[l], ln2_b[l], t["moe_rows"], alpha)

    del d_ff
    return xf.reshape(bsz, seq, d)
```

```python
import functools

import jax
import jax.numpy as jnp
from jax import lax
from jax.experimental import pallas as pl
from jax.experimental.pallas import tpu as pltpu

ML_HEADS, ML_DQK, ML_DV, ML_CONV = 4, 64, 128, 4
ML_W = ML_HEADS * ML_DV
ML_QK_W = 2 * ML_HEADS * ML_DQK
RW_HEADS, RW_DH = 8, 64
RW_W = RW_HEADS * RW_DH
RW_DECAY_LORA, RW_AAA_LORA, RW_GATE_LORA = 64, 64, 128
RW_GN_EPS = 64e-5
CA_HEADS, CA_DH = 4, 128
CA_W = CA_HEADS * CA_DH
N_BRANCH = 3
N_EXPERTS, TOP_K = 32, 4
SWIGLU_LIMIT, SWIGLU_ALPHA = 7.0, 1.702
LN_EPS = 1e-5

LANES = 128
SUBLANES = 8
VMEM_LIMIT = 56 * 1024 * 1024

BF16 = jnp.bfloat16
F32 = jnp.float32
NEG_BIG = -1e30


def _dot(a, b):
    return jnp.dot(a.astype(BF16), b.astype(BF16), preferred_element_type=F32)


def _split3(a):
    hi = a.astype(BF16)
    r1 = a - hi.astype(F32)
    mid = r1.astype(BF16)
    lo = (r1 - mid.astype(F32)).astype(BF16)
    return hi, mid, lo


def _dot2_rhs(a, b_bf16):
    hi = a.astype(BF16)
    lo = (a - hi.astype(F32)).astype(BF16)
    d = functools.partial(jnp.dot, preferred_element_type=F32)
    return d(hi, b_bf16) + d(lo, b_bf16)


def _head_sum(x, red, bc):
    return _dot2_rhs(_dot2_rhs(x, red), bc)


def _dot_exact_lhs(a_bf16, b):
    hi, mid, lo = _split3(b)
    d = functools.partial(jnp.dot, preferred_element_type=F32)
    return d(a_bf16, hi) + d(a_bf16, mid) + d(a_bf16, lo)


def _sigmoid(x):
    return 1.0 / (1.0 + jnp.exp(-x))


def _softplus(x):
    return jnp.maximum(x, 0.0) + jnp.log1p(jnp.exp(-jnp.abs(x)))


def _layer_norm(v, g, b, eps=LN_EPS):
    mu = jnp.mean(v, axis=-1, keepdims=True)
    c = v - mu
    var = jnp.mean(c * c, axis=-1, keepdims=True)
    return c * lax.rsqrt(var + eps) * g + b


def _params(*sem):
    return pltpu.CompilerParams(dimension_semantics=sem, vmem_limit_bytes=VMEM_LIMIT)


def _ordered_after(kernel, first, count):
    def wrapped(*refs):
        return kernel(*refs[:first], *refs[first + count:])
    return wrapped


def _after_specs(after):
    return [pl.BlockSpec(memory_space=pl.ANY)] * len(after)


def _const_spec(shape):
    nd = len(shape)
    return pl.BlockSpec(shape, lambda *_: (0,) * nd, pipeline_mode=pl.Buffered(1))


def _ln_kernel(x_ref, g_ref, b_ref, o_ref, ob_ref):
    y = _layer_norm(x_ref[...], g_ref[...], b_ref[...])
    o_ref[...] = y
    ob_ref[...] = y.astype(BF16)


def _ln_rows(x2d, g, b, tile):
    n, d = x2d.shape
    return pl.pallas_call(
        _ln_kernel,
        grid=(n // tile,),
        in_specs=[pl.BlockSpec((tile, d), lambda i: (i, 0)), _const_spec((1, d)), _const_spec((1, d))],
        out_specs=[pl.BlockSpec((tile, d), lambda i: (i, 0)), pl.BlockSpec((tile, d), lambda i: (i, 0))],
        out_shape=[jax.ShapeDtypeStruct((n, d), F32), jax.ShapeDtypeStruct((n, d), BF16)],
        compiler_params=_params("parallel"),
        name="ln_rows",
    )(x2d, g.reshape(1, d), b.reshape(1, d))


def _mm_kernel(a_ref, b_ref, o_ref):
    o_ref[...] = jnp.dot(a_ref[...], b_ref[...], preferred_element_type=F32)


def _matmul(a, b, tile):
    m, k = a.shape
    n = b.shape[1]
    return pl.pallas_call(
        _mm_kernel,
        grid=(m // tile,),
        in_specs=[pl.BlockSpec((tile, k), lambda i: (i, 0)), _const_spec((k, n))],
        out_specs=pl.BlockSpec((tile, n), lambda i: (i, 0)),
        out_shape=jax.ShapeDtypeStruct((m, n), F32),
        compiler_params=_params("parallel"),
        name="matmul",
    )(a, b)


def _mlstm_kernel(xb_ref, wml_ref, wg_ref, convw_ref, convb_ref, gb_ref, normg_ref, tri_ref,
                  h_ref, ubuf, c_ref, n_ref, m_ref, *, tile, chunk):
    @pl.when(pl.program_id(1) == 0)
    def _():
        ubuf[0:SUBLANES, :] = jnp.zeros((SUBLANES, ML_QK_W), F32)
        c_ref[...] = jnp.zeros_like(c_ref)
        n_ref[...] = jnp.zeros_like(n_ref)
        m_ref[...] = jnp.zeros_like(m_ref)

    xb = xb_ref[...]
    u = jnp.dot(xb, wml_ref[...], preferred_element_type=F32)
    gates = jnp.dot(xb, wg_ref[...], preferred_element_type=F32) + gb_ref[...]

    ubuf[SUBLANES:SUBLANES + tile, :] = u[:, :ML_QK_W]
    acc = jnp.broadcast_to(convb_ref[...], (tile, ML_QK_W))
    for j in range(ML_CONV):
        acc = acc + convw_ref[j:j + 1, :] * ubuf[pl.ds(SUBLANES - ML_CONV + 1 + j, tile), :]
    ubuf[0:SUBLANES, :] = ubuf[tile:tile + SUBLANES, :]
    qk = acc * _sigmoid(acc)
    q = qk[:, :ML_HEADS * ML_DQK] * (ML_DQK ** -0.5)
    k = qk[:, ML_HEADS * ML_DQK:]
    v = u[:, ML_QK_W:ML_QK_W + ML_W]
    og = u[:, ML_QK_W + ML_W:]

    lane = lax.broadcasted_iota(jnp.int32, (chunk, LANES), 1)
    row = lax.broadcasted_iota(jnp.int32, (chunk, chunk), 0)
    col = lax.broadcasted_iota(jnp.int32, (chunk, chunk), 1)
    causal = row >= col
    log_f = -_softplus(-gates)

    for c in range(tile // chunk):
        rs = slice(c * chunk, (c + 1) * chunk)
        bcum = _dot_exact_lhs(tri_ref[...], log_f[rs])
        mcol = jnp.where(lane < ML_HEADS, gates[rs], bcum)
        mrow = mcol.T
        for h in range(ML_HEADS):
            i_col = mcol[:, h:h + 1]
            b_col = mcol[:, ML_HEADS + h:ML_HEADS + h + 1]
            i_row = mrow[h:h + 1, :]
            b_row = mrow[ML_HEADS + h:ML_HEADS + h + 1, :]
            m_prev = m_ref[h][0:1, 0:1]
            n_prev = n_ref[h][0:1, :]
            c_prev = c_ref[h]
            qh = q[rs, h * ML_DQK:(h + 1) * ML_DQK]
            kh = k[rs, h * ML_DQK:(h + 1) * ML_DQK]
            vh = v[rs, h * ML_DV:(h + 1) * ML_DV]

            dmat = jnp.where(causal, b_col - b_row + i_row, -jnp.inf)
            m_inter = b_col + m_prev
            m_t = jnp.maximum(m_inter, jnp.max(dmat, axis=1, keepdims=True))
            s = lax.dot_general(qh.astype(BF16), kh.astype(BF16), (((1,), (1,)), ((), ())),
                                preferred_element_type=F32)
            w_intra = jnp.exp(dmat - m_t) * s
            s_inter = jnp.exp(m_inter - m_t)
            num = s_inter * _dot(qh, c_prev) + _dot(w_intra, vh)
            den = (s_inter * jnp.sum(qh * n_prev, axis=1, keepdims=True)
                   + jnp.sum(w_intra, axis=1, keepdims=True))
            hh = num / jnp.maximum(jnp.abs(den), jnp.exp(-m_t))

            b_last = b_col[chunk - 1:chunk, :]
            g_col = b_last - b_col + i_col
            m_new = jnp.maximum(b_last + m_prev, jnp.max(g_col, axis=0, keepdims=True))
            carry = jnp.exp(b_last + m_prev - m_new)
            kw = kh * jnp.exp(g_col - m_new)
            c_ref[h] = carry * c_prev + lax.dot_general(
                kw.astype(BF16), vh.astype(BF16), (((0,), (0,)), ((), ())), preferred_element_type=F32)
            n_new = carry * n_prev + jnp.sum(kw, axis=0, keepdims=True)
            n_ref[h] = jnp.broadcast_to(n_new, (SUBLANES, ML_DQK))
            m_ref[h] = jnp.broadcast_to(m_new, (SUBLANES, LANES))

            mu = jnp.mean(hh, axis=1, keepdims=True)
            cen = hh - mu
            var = jnp.mean(cen * cen, axis=1, keepdims=True)
            y = cen * lax.rsqrt(var + LN_EPS) * normg_ref[:, h * ML_DV:(h + 1) * ML_DV]
            h_ref[rs, h * ML_DV:(h + 1) * ML_DV] = (
                _sigmoid(og[rs, h * ML_DV:(h + 1) * ML_DV]) * y).astype(BF16)


def _mlstm(xb, wml, wg, convw, convb, gb, normg, tile, chunk, after=()):
    bsz, s, d = xb.shape
    tri = (jnp.arange(chunk)[:, None] >= jnp.arange(chunk)[None, :]).astype(BF16)
    kern = _ordered_after(functools.partial(_mlstm_kernel, tile=tile, chunk=chunk), 8, len(after))
    return pl.pallas_call(
        kern,
        grid=(bsz, s // tile),
        in_specs=[pl.BlockSpec((None, tile, d), lambda b, j: (b, j, 0)),
                  _const_spec(wml.shape), _const_spec(wg.shape), _const_spec(convw.shape),
                  _const_spec(convb.shape), _const_spec(gb.shape), _const_spec(normg.shape),
                  _const_spec(tri.shape)] + _after_specs(after),
        out_specs=pl.BlockSpec((None, tile, ML_W), lambda b, j: (b, j, 0)),
        out_shape=jax.ShapeDtypeStruct((bsz, s, ML_W), BF16),
        scratch_shapes=[pltpu.VMEM((tile + SUBLANES, ML_QK_W), F32),
                        pltpu.VMEM((ML_HEADS, ML_DQK, ML_DV), F32),
                        pltpu.VMEM((ML_HEADS, SUBLANES, ML_DQK), F32),
                        pltpu.VMEM((ML_HEADS, SUBLANES, LANES), F32)],
        compiler_params=_params("parallel", "arbitrary"),
        name="mlstm",
    )(xb, wml, wg, convw, convb, gb, normg, tri, *after)


RW_PAD_IN = 3 * RW_W + 3 * LANES


def _rwkv_prep_kernel(xb_ref, wrw_ref, mu_ref, w0_ref, wup_ref, a0_ref, aup_ref, gup_ref, kkw_ref,
                      ka_ref, rk_ref, red_ref, bc_ref,
                      r_ref, w_ref, k_ref, v_ref, kk_ref, kka_ref, g_ref, bonus_ref, ubuf, *, tile):
    @pl.when(pl.program_id(1) == 0)
    def _():
        ubuf[0:SUBLANES, :] = jnp.zeros((SUBLANES, RW_PAD_IN), F32)

    u = jnp.dot(xb_ref[...], wrw_ref[...], preferred_element_type=F32)
    ubuf[SUBLANES:SUBLANES + tile, :] = u
    u_prev = ubuf[pl.ds(SUBLANES - 1, tile), :]
    ubuf[0:SUBLANES, :] = ubuf[tile:tile + SUBLANES, :]
    us = u + (u_prev - u) * mu_ref[...]
    r = us[:, 0:RW_W]
    kr = us[:, RW_W:2 * RW_W]
    vr = us[:, 2 * RW_W:3 * RW_W]
    wd = us[:, 3 * RW_W:3 * RW_W + LANES]
    ad = us[:, 3 * RW_W + LANES:3 * RW_W + 2 * LANES]
    gd = us[:, 3 * RW_W + 2 * LANES:]

    w_log = -_softplus(-(w0_ref[...] + _dot(jnp.tanh(wd), wup_ref[...]))) - 0.5
    decay = jnp.exp(-jnp.exp(w_log))
    a = _sigmoid(a0_ref[...] + _dot(ad, aup_ref[...]))
    g = _dot(_sigmoid(gd), gup_ref[...])

    red, bc = red_ref[...], bc_ref[...]
    kk = kr * kkw_ref[...]
    kk = kk * lax.rsqrt(jnp.maximum(_head_sum(kk * kk, red, bc), 1e-24))
    k2 = kr * (1.0 + (a - 1.0) * ka_ref[...])
    bonus = _head_sum(r * k2 * rk_ref[...], red, bc) * vr

    r_ref[...] = r
    w_ref[...] = decay
    k_ref[...] = k2
    v_ref[...] = vr
    kk_ref[...] = kk
    kka_ref[...] = kk * a
    g_ref[...] = g
    bonus_ref[...] = bonus


def _rwkv_prep(xb, wrw, mu, w0, wup, a0, aup, gup, kkw, ka, rk, red, bc, tile):
    bsz, s, d = xb.shape
    consts = (wrw, mu, w0, wup, a0, aup, gup, kkw, ka, rk, red, bc)
    spec = pl.BlockSpec((None, tile, RW_W), lambda b, j: (b, j, 0))
    tspec = pl.BlockSpec((tile, RW_W), lambda b, j: (j, b))
    return pl.pallas_call(
        functools.partial(_rwkv_prep_kernel, tile=tile),
        grid=(bsz, s // tile),
        in_specs=[pl.BlockSpec((None, tile, d), lambda b, j: (b, j, 0))] + [_const_spec(c.shape) for c in consts],
        out_specs=[tspec] * 6 + [spec] * 2,
        out_shape=[jax.ShapeDtypeStruct((s, bsz * RW_W), F32)] * 6 + [jax.ShapeDtypeStruct((bsz, s, RW_W), F32)] * 2,
        scratch_shapes=[pltpu.VMEM((tile + SUBLANES, RW_PAD_IN), F32)],
        compiler_params=_params("parallel", "arbitrary"),
        name="rwkv_prep",
    )(xb, *consts)


def _rwkv_scan_kernel(r_ref, w_ref, k_ref, v_ref, kk_ref, kka_ref, o_ref, st_ref, *, steps):
    @pl.when(pl.program_id(1) == 0)
    def _():
        st_ref[...] = jnp.zeros_like(st_ref)

    zeros = jnp.zeros((RW_DH, LANES), F32)

    def first(kc, sa):
        return sa + st_ref[kc] * kk_ref[0, pl.ds(kc, 1), :]

    sa0 = lax.fori_loop(0, RW_DH, first, zeros, unroll=8)

    def step(t, sa):
        vt = v_ref[t]
        tn = jnp.minimum(t + 1, steps - 1)

        def body(kc, carry):
            out, sa_next = carry
            row = pl.ds(kc, 1)
            new = st_ref[kc] * w_ref[t, row, :] - sa * kka_ref[t, row, :] + vt * k_ref[t, row, :]
            st_ref[kc] = new
            return out + new * r_ref[t, row, :], sa_next + new * kk_ref[tn, row, :]

        out, sa_next = lax.fori_loop(0, RW_DH, body, (zeros, zeros), unroll=16)
        o_ref[t] = out
        return sa_next

    lax.fori_loop(0, steps, step, sa0)


def _rwkv_scan(r, w, k, v, kk, kka, steps, after=()):
    s, dh, nl = r.shape
    spec = pl.BlockSpec((steps, dh, LANES), lambda g, j: (j, 0, g))
    return pl.pallas_call(
        _ordered_after(functools.partial(_rwkv_scan_kernel, steps=steps), 6, len(after)),
        grid=(nl // LANES, s // steps),
        in_specs=[spec] * 6 + _after_specs(after),
        out_specs=spec,
        out_shape=jax.ShapeDtypeStruct((s, dh, nl), F32),
        scratch_shapes=[pltpu.VMEM((dh, dh, LANES), F32)],
        compiler_params=_params("parallel", "arbitrary"),
        name="rwkv_scan",
    )(r, w, k, v, kk, kka, *after)


def _xattn_kernel(xb_ref, wq_ref, kt_ref, v_ref, o_ref):
    q = jnp.dot(xb_ref[...], wq_ref[...], preferred_element_type=F32)
    for h in range(CA_HEADS):
        hs = slice(h * CA_DH, (h + 1) * CA_DH)
        s = jnp.dot(q[:, hs].astype(BF16), kt_ref[hs, :], preferred_element_type=F32) * (CA_DH ** -0.5)
        p = jnp.exp(s - jnp.max(s, axis=1, keepdims=True))
        den = jnp.sum(p, axis=1, keepdims=True)
        o = jnp.dot(p.astype(BF16), v_ref[:, hs], preferred_element_type=F32) / den
        o_ref[:, hs] = o.astype(BF16)


def _xattn(xb, wq, kt, v, tile, after=()):
    bsz, s, d = xb.shape
    m = v.shape[1]
    return pl.pallas_call(
        _ordered_after(_xattn_kernel, 4, len(after)),
        grid=(bsz, s // tile),
        in_specs=[pl.BlockSpec((None, tile, d), lambda b, j: (b, j, 0)), _const_spec(wq.shape),
                  pl.BlockSpec((None, CA_W, m), lambda b, j: (b, 0, 0)),
                  pl.BlockSpec((None, m, CA_W), lambda b, j: (b, 0, 0))] + _after_specs(after),
        out_specs=pl.BlockSpec((None, tile, CA_W), lambda b, j: (b, j, 0)),
        out_shape=jax.ShapeDtypeStruct((bsz, s, CA_W), BF16),
        compiler_params=_params("parallel", "parallel"),
        name="xattn",
    )(xb, wq, kt, v, *after)


ROUTE_IDX, ROUTE_RANK, ROUTE_GATE = 0, TOP_K, 2 * TOP_K


def _merge_kernel(x_ref, xb_ref, hml_ref, o_ref, bonus_ref, g_ref, hca_ref,
                  wgate_ref, gateb_ref, red_ref, bc_ref, rwg_ref, rwb_ref, wml_ref, wrw_ref, wca_ref, wo_ref,
                  lng_ref, lnb_ref, rwhi_ref, rwlo_ref, rb_ref, tri_ref,
                  x1_ref, x1b_ref, route_ref, cnt_ref, carry_ref, *, tile, alpha, d_model):
    @pl.when(pl.program_id(0) == 0)
    def _():
        carry_ref[...] = jnp.zeros_like(carry_ref)

    red, bc = red_ref[...], bc_ref[...]
    o = o_ref[...]
    mu = _head_sum(o, red, bc) * (1.0 / RW_DH)
    cen = o - mu
    var = _head_sum(cen * cen, red, bc) * (1.0 / RW_DH)
    h_rw = (cen * lax.rsqrt(var + RW_GN_EPS) * rwg_ref[...] + rwb_ref[...] + bonus_ref[...]) * g_ref[...]

    xb = xb_ref[...]
    y = None
    for br, (h_br, w_ref) in enumerate(((hml_ref[...], wml_ref), (h_rw.astype(BF16), wrw_ref),
                                        (hca_ref[...], wca_ref))):
        cols = slice(br * d_model, (br + 1) * d_model)
        gate = _sigmoid(jnp.dot(xb, wgate_ref[:, cols], preferred_element_type=F32) + gateb_ref[:, cols])
        term = gate * jnp.dot(h_br, w_ref[...], preferred_element_type=F32)
        y = term if y is None else y + term
    mixed = _dot(y, wo_ref[...])
    x1 = _layer_norm(alpha * x_ref[...] + mixed, lng_ref[...], lnb_ref[...])
    x1_ref[...] = x1
    x1b_ref[...] = x1.astype(BF16)

    hi = x1.astype(BF16)
    lo = (x1 - hi.astype(F32)).astype(BF16)
    d = functools.partial(jnp.dot, preferred_element_type=F32)
    logits = d(hi, rwhi_ref[...]) + d(lo, rwhi_ref[...]) + d(hi, rwlo_ref[...]) + rb_ref[...]

    lane = lax.broadcasted_iota(jnp.int32, (tile, LANES), 1)
    vals = logits
    tops, onehots = [], []
    for _ in range(TOP_K):
        m = jnp.max(vals, axis=1, keepdims=True)
        idx = jnp.min(jnp.where(vals == m, lane, LANES), axis=1, keepdims=True)
        sel = lane == idx
        tops.append((m, idx))
        onehots.append(sel.astype(F32))
        vals = jnp.where(sel, -jnp.inf, vals)
    exps = [jnp.exp(m - tops[0][0]) for m, _ in tops]
    den = exps[0] + exps[1] + exps[2] + exps[3]

    cnt = onehots[0] + onehots[1] + onehots[2] + onehots[3]
    before = jnp.dot(tri_ref[...], cnt.astype(BF16), preferred_element_type=F32) + carry_ref[0:1, :]
    route = jnp.zeros((tile, LANES), F32)
    for kq in range(TOP_K):
        rank = jnp.sum(onehots[kq] * before, axis=1, keepdims=True)
        route = jnp.where(lane == ROUTE_IDX + kq, tops[kq][1].astype(F32), route)
        route = jnp.where(lane == ROUTE_RANK + kq, rank, route)
        route = jnp.where(lane == ROUTE_GATE + kq, exps[kq] / den, route)
    route_ref[...] = route
    total = carry_ref[0:1, :] + jnp.sum(cnt, axis=0, keepdims=True)
    carry_ref[...] = jnp.broadcast_to(total, carry_ref.shape)
    cnt_ref[...] = jnp.broadcast_to(total, cnt_ref.shape)


def _merge(x, xb, hml, o_tm, bonus, g, hca, consts, tile, alpha):
    n, d = x.shape
    tiles_per_seq = o_tm.shape[0] // tile
    tri = (jnp.arange(tile)[:, None] > jnp.arange(tile)[None, :]).astype(BF16)
    consts = tuple(consts) + (tri,)

    def rows(w):
        return pl.BlockSpec((tile, w), lambda i: (i, 0))

    o_spec = pl.BlockSpec((tile, RW_W), lambda i: (i % tiles_per_seq, i // tiles_per_seq))
    return pl.pallas_call(
        functools.partial(_merge_kernel, tile=tile, alpha=alpha, d_model=d),
        grid=(n // tile,),
        in_specs=[rows(d), rows(d), rows(ML_W), o_spec, rows(RW_W), rows(RW_W), rows(CA_W)]
        + [_const_spec(c.shape) for c in consts],
        out_specs=[rows(d), rows(d), rows(LANES), _const_spec((SUBLANES, LANES))],
        out_shape=[jax.ShapeDtypeStruct((n, d), F32), jax.ShapeDtypeStruct((n, d), BF16),
                   jax.ShapeDtypeStruct((n, LANES), F32), jax.ShapeDtypeStruct((SUBLANES, LANES), F32)],
        scratch_shapes=[pltpu.VMEM((SUBLANES, LANES), F32)],
        compiler_params=_params("arbitrary"),
        name="merge_route",
    )(x, xb, hml, o_tm, bonus, g, hca, *consts)


def _row_copy(src_ref, src_row, dst_ref, dst_row, sem):
    return pltpu.make_async_copy(src_ref.at[pl.ds(src_row, 1)], dst_ref.at[pl.ds(dst_row, 1)], sem)


def _dispatch_kernel(dest_ref, zero_blk_ref, x_ref, xd_ref, xbuf, zbuf, sems, zsem, *, tile, nsteps, block):
    i = pl.program_id(0)
    slot = i % 2

    @pl.when(i == 0)
    def _():
        zbuf[...] = jnp.zeros_like(zbuf)
        n_zero = zero_blk_ref.shape[0]

        def fresh(j):
            return jnp.logical_or(j == 0, zero_blk_ref[j] != zero_blk_ref[jnp.maximum(j - 1, 0)])

        def zero_copy(j):
            return pltpu.make_async_copy(zbuf, xd_ref.at[pl.ds(zero_blk_ref[j] * block, block)], zsem)

        def start(j, carry):
            @pl.when(fresh(j))
            def _():
                zero_copy(j).start()
            return carry

        def wait(j, carry):
            @pl.when(fresh(j))
            def _():
                zero_copy(j).wait()
            return carry

        lax.fori_loop(0, n_zero, start, 0)
        lax.fori_loop(0, n_zero, wait, 0)

    def retire(s):
        for _ in range(TOP_K):
            pltpu.make_async_copy(xbuf.at[s], xd_ref.at[pl.ds(0, tile)], sems.at[s]).wait()

    @pl.when(i >= 2)
    def _():
        retire(slot)

    xbuf[slot] = x_ref[...]

    def issue(r, carry):
        for kq in range(TOP_K):
            _row_copy(xbuf.at[slot], r, xd_ref, dest_ref[r * TOP_K + kq], sems.at[slot]).start()
        return carry

    lax.fori_loop(0, tile, issue, 0)

    @pl.when(i == nsteps - 1)
    def _():
        retire(slot)
        if nsteps > 1:
            retire(1 - slot)


def _dispatch(dest_flat, zero_blk, x1, n_rows, tile, block):
    n, d = x1.shape
    nsteps = n // tile
    return pl.pallas_call(
        functools.partial(_dispatch_kernel, tile=tile, nsteps=nsteps, block=block),
        grid=(nsteps,),
        in_specs=[pl.BlockSpec((tile * TOP_K,), lambda i: (i,), memory_space=pltpu.SMEM),
                  pl.BlockSpec(memory_space=pltpu.SMEM),
                  pl.BlockSpec((tile, d), lambda i: (i, 0))],
        out_specs=pl.BlockSpec(memory_space=pl.ANY),
        out_shape=jax.ShapeDtypeStruct((n_rows, d), F32),
        scratch_shapes=[pltpu.VMEM((2, tile, d), F32), pltpu.VMEM((block, d), F32),
                        pltpu.SemaphoreType.DMA((2,)), pltpu.SemaphoreType.DMA],
        compiler_params=_params("arbitrary"),
        name="moe_dispatch",
    )(dest_flat, zero_blk, x1)


DEINT_COLS = 512


def _deinterleave_kernel(w_ref, pe_ref, po_ref, glu_ref, lin_ref):
    wb = w_ref[...].astype(BF16)
    glu_ref[...] = jnp.dot(wb, pe_ref[...], preferred_element_type=F32).astype(BF16)
    lin_ref[...] = jnp.dot(wb, po_ref[...], preferred_element_type=F32).astype(BF16)


def _deinterleave(w_gu):
    e, d, two_ff = w_gu.shape
    half = DEINT_COLS // 2
    src = jnp.arange(DEINT_COLS)[:, None]
    dst = jnp.arange(half)[None, :]
    pe = (src == 2 * dst).astype(BF16)
    po = (src == 2 * dst + 1).astype(BF16)
    out = jax.ShapeDtypeStruct((e, d, two_ff // 2), BF16)
    return pl.pallas_call(
        _deinterleave_kernel,
        grid=(e, two_ff // DEINT_COLS),
        in_specs=[pl.BlockSpec((None, d, DEINT_COLS), lambda i, c: (i, 0, c)),
                  _const_spec(pe.shape), _const_spec(po.shape)],
        out_specs=[pl.BlockSpec((None, d, half), lambda i, c: (i, 0, c))] * 2,
        out_shape=[out, out],
        compiler_params=_params("parallel", "parallel"),
        name="deinterleave_w",
    )(w_gu, pe, po)


def _expert_kernel(blk_e_ref, nused_ref, xd_ref, wglu_ref, wlin_ref, bglu_ref, blin_ref, wdn_ref, bdn_ref, y_ref):
    del blk_e_ref
    live = pl.program_id(0) < nused_ref[0]

    @pl.when(live)
    def _():
        xb = xd_ref[...].astype(BF16)
        h_glu = jnp.dot(xb, wglu_ref[...], preferred_element_type=F32) + bglu_ref[...]
        h_lin = jnp.dot(xb, wlin_ref[...], preferred_element_type=F32) + blin_ref[...]
        x_glu = jnp.minimum(h_glu, SWIGLU_LIMIT)
        x_lin = jnp.clip(h_lin, -SWIGLU_LIMIT, SWIGLU_LIMIT)
        act = x_glu * _sigmoid(SWIGLU_ALPHA * x_glu) * (x_lin + 1.0)
        y_ref[...] = _dot(act, wdn_ref[...]) + bdn_ref[...]

    @pl.when(jnp.logical_not(live))
    def _():
        y_ref[...] = jnp.zeros_like(y_ref)


def _experts(blk_e, nused, xd, wglu, wlin, bglu, blin, wdn, bdn, block):
    n_rows, d = xd.shape
    dff = wglu.shape[2]
    grid_spec = pltpu.PrefetchScalarGridSpec(
        num_scalar_prefetch=2,
        grid=(n_rows // block,),
        in_specs=[pl.BlockSpec((block, d), lambda i, be, nu: (jnp.minimum(i, nu[0] - 1), 0)),
                  pl.BlockSpec((None, d, dff), lambda i, be, nu: (be[i], 0, 0)),
                  pl.BlockSpec((None, d, dff), lambda i, be, nu: (be[i], 0, 0)),
                  pl.BlockSpec((None, 1, dff), lambda i, be, nu: (be[i], 0, 0)),
                  pl.BlockSpec((None, 1, dff), lambda i, be, nu: (be[i], 0, 0)),
                  pl.BlockSpec((None, dff, d), lambda i, be, nu: (be[i], 0, 0)),
                  pl.BlockSpec((None, 1, d), lambda i, be, nu: (be[i], 0, 0))],
        out_specs=pl.BlockSpec((block, d), lambda i, be, nu: (i, 0)),
    )
    return pl.pallas_call(
        _expert_kernel,
        grid_spec=grid_spec,
        out_shape=jax.ShapeDtypeStruct((n_rows, d), F32),
        compiler_params=_params("arbitrary"),
        name="moe_experts",
    )(blk_e, nused, xd, wglu, wlin, bglu, blin, wdn, bdn)


def _combine_kernel(dest_ref, dest_next_ref, route_ref, x1_ref, y_ref, lng_ref, lnb_ref, x2_ref, x2b_ref,
                    ybuf, sems, *, tile, alpha, nsteps):
    i = pl.program_id(0)
    slot = i % 2

    def fetch(d_ref, s):
        def issue(r, carry):
            for kq in range(TOP_K):
                _row_copy(y_ref, d_ref[r * TOP_K + kq], ybuf.at[s, kq], r, sems.at[s]).start()
            return carry

        lax.fori_loop(0, tile, issue, 0, unroll=4)

    @pl.when(i == 0)
    def _():
        fetch(dest_ref, slot)

    @pl.when(i + 1 < nsteps)
    def _():
        fetch(dest_next_ref, 1 - slot)

    for kq in range(TOP_K):
        pltpu.make_async_copy(y_ref.at[pl.ds(0, tile)], ybuf.at[slot, kq], sems.at[slot]).wait()
    route = route_ref[...]
    moe = route[:, ROUTE_GATE:ROUTE_GATE + 1] * ybuf[slot, 0]
    for kq in range(1, TOP_K):
        moe = moe + route[:, ROUTE_GATE + kq:ROUTE_GATE + kq + 1] * ybuf[slot, kq]
    x2 = _layer_norm(alpha * x1_ref[...] + moe, lng_ref[...], lnb_ref[...])
    x2_ref[...] = x2
    x2b_ref[...] = x2.astype(BF16)


def _combine(dest_flat, route, x1, y_disp, lng, lnb, tile, alpha):
    n, d = x1.shape
    nsteps = n // tile
    return pl.pallas_call(
        functools.partial(_combine_kernel, tile=tile, alpha=alpha, nsteps=nsteps),
        grid=(nsteps,),
        in_specs=[pl.BlockSpec((tile * TOP_K,), lambda i: (i,), memory_space=pltpu.SMEM),
                  pl.BlockSpec((tile * TOP_K,), lambda i: (jnp.minimum(i + 1, nsteps - 1),),
                               memory_space=pltpu.SMEM),
                  pl.BlockSpec((tile, LANES), lambda i: (i, 0)),
                  pl.BlockSpec((tile, d), lambda i: (i, 0)),
                  pl.BlockSpec(memory_space=pl.ANY),
                  _const_spec((1, d)), _const_spec((1, d))],
        out_specs=[pl.BlockSpec((tile, d), lambda i: (i, 0)), pl.BlockSpec((tile, d), lambda i: (i, 0))],
        out_shape=[jax.ShapeDtypeStruct((n, d), F32), jax.ShapeDtypeStruct((n, d), BF16)],
        scratch_shapes=[pltpu.VMEM((2, TOP_K, tile, d), F32), pltpu.SemaphoreType.DMA((2,))],
        compiler_params=_params("arbitrary"),
        name="moe_combine",
    )(dest_flat, dest_flat, route, x1, y_disp, lng.reshape(1, d), lnb.reshape(1, d))


def _tiles(bsz, seq):
    n = bsz * seq
    return dict(
        ln=min(1024, n),
        mlstm=min(256, seq), mlstm_chunk=min(128, seq),
        rwkv_prep=min(256, seq),
        scan_steps=min(32, seq),
        xattn=min(512, seq),
        merge=min(512, seq),
        moe_rows=min(256, n),
        moe_block=512,
    )


def _pad_cols(w, width):
    return jnp.pad(w, ((0, 0), (0, width - w.shape[1])))


def kernel(x, mem, ln_in_g, ln_in_b, mem_ln_g, mem_ln_b, w_in, ml_conv_w, ml_conv_b, ml_ig_b, ml_fg_b, ml_norm_g, rw_mu, rw_w0, rw_w_up, rw_a0, rw_a_up, rw_g_up, rw_kk, rw_ka, rw_rk, rw_ln_g, rw_ln_b, ca_w_kv, gate_b, w_br_ml, w_br_rw, w_br_ca, w_o, ln1_g, ln1_b, router_w, router_b, w_gu, b_gu, w_dn, b_dn, ln2_g, ln2_b):
    bsz, seq, d = x.shape
    mem_len = mem.shape[1]
    depth = w_in.shape[0]
    n = bsz * seq
    t = _tiles(bsz, seq)
    alpha = (2 * depth) ** 0.25
    d_ff = w_dn.shape[2]

    xf, xb = _ln_rows(x.reshape(n, d), ln_in_g, ln_in_b, t["ln"])
    _, memb = _ln_rows(mem.reshape(bsz * mem_len, d), mem_ln_g, mem_ln_b, min(t["ln"], bsz * mem_len))

    o_qk, o_v, o_og = 0, ML_QK_W, ML_QK_W + ML_W
    o_ig = o_og + ML_W
    o_fg = o_ig + ML_HEADS
    o_rw = o_fg + ML_HEADS
    o_ca = o_rw + 3 * RW_W + RW_DECAY_LORA + RW_AAA_LORA + RW_GATE_LORA
    o_gate = o_ca + CA_W

    head_of_lane = jnp.arange(RW_W) // RW_DH
    red = (head_of_lane[:, None] == jnp.arange(LANES)[None, :]).astype(BF16)
    bc = red.T
    n_asg = n * TOP_K
    block = t["moe_block"]
    n_blocks = -(-n_asg // block) + N_EXPERTS
    n_rows = n_blocks * block

    for l in range(depth):
        w = w_in[l]
        o_wd = o_rw + 3 * RW_W
        o_ad = o_wd + RW_DECAY_LORA
        o_gd = o_ad + RW_AAA_LORA
        wrw = jnp.concatenate([w[:, o_rw:o_wd], _pad_cols(w[:, o_wd:o_ad], LANES),
                               _pad_cols(w[:, o_ad:o_gd], LANES), w[:, o_gd:o_ca]], axis=1).astype(BF16)
        mu = rw_mu[l]
        mu_p = jnp.concatenate([mu[:3 * RW_W], jnp.pad(mu[3 * RW_W:3 * RW_W + RW_DECAY_LORA], (0, LANES - RW_DECAY_LORA)),
                                jnp.pad(mu[3 * RW_W + RW_DECAY_LORA:3 * RW_W + RW_DECAY_LORA + RW_AAA_LORA],
                                        (0, LANES - RW_AAA_LORA)),
                                mu[3 * RW_W + RW_DECAY_LORA + RW_AAA_LORA:]])[None, :]
        wup = jnp.pad(rw_w_up[l], ((0, LANES - RW_DECAY_LORA), (0, 0))).astype(BF16)
        aup = jnp.pad(rw_a_up[l], ((0, LANES - RW_AAA_LORA), (0, 0))).astype(BF16)
        r, wdec, k2, vr, kk, kka, g, bonus = _rwkv_prep(
            xb.reshape(bsz, seq, d), wrw, mu_p, rw_w0[l][None, :], wup, rw_a0[l][None, :], aup,
            rw_g_up[l].astype(BF16), rw_kk[l][None, :], rw_ka[l][None, :], rw_rk[l].reshape(1, RW_W), red, bc,
            t["rwkv_prep"])

        def to_scan(a):
            return a.reshape(seq, bsz * RW_HEADS, RW_DH).transpose(0, 2, 1)

        nl = bsz * RW_HEADS
        nl_pad = -(-nl // LANES) * LANES
        ops = [to_scan(a) for a in (r, wdec, k2, vr, kk, kka)]
        if nl_pad != nl:
            ops = [jnp.pad(a, ((0, 0), (0, 0), (0, nl_pad - nl))) for a in ops]

        wml = w[:, o_qk:o_ig].astype(BF16)
        wg = _pad_cols(w[:, o_ig:o_rw], LANES).astype(BF16)
        gb = _pad_cols(jnp.concatenate([ml_ig_b[l], ml_fg_b[l]])[None, :], LANES)
        h_ml = _mlstm(xb.reshape(bsz, seq, d), wml, wg, ml_conv_w[l], ml_conv_b[l][None, :], gb,
                      ml_norm_g[l][None, :], t["mlstm"], t["mlstm_chunk"], after=(g,))

        kv = _matmul(memb, ca_w_kv[l].astype(BF16), min(512, bsz * mem_len)).reshape(bsz, mem_len, 2 * CA_W)
        kt = kv[:, :, :CA_W].transpose(0, 2, 1).astype(BF16)
        vm = kv[:, :, CA_W:].astype(BF16)
        h_ca = _xattn(xb.reshape(bsz, seq, d), w[:, o_ca:o_gate].astype(BF16), kt, vm, t["xattn"], after=(g,))

        o_scan = _rwkv_scan(*ops, t["scan_steps"], after=(h_ml, h_ca))[:, :, :nl]
        o_rwkv = o_scan.transpose(0, 2, 1).reshape(seq, bsz * RW_W)

        rw_pad = _pad_cols(router_w[l], LANES)
        rw_hi = rw_pad.astype(BF16)
        rw_lo = (rw_pad - rw_hi.astype(F32)).astype(BF16)
        rb = jnp.concatenate([router_b[l], jnp.full((LANES - N_EXPERTS,), NEG_BIG, F32)])[None, :]
        consts = (w[:, o_gate:].astype(BF16), gate_b[l][None, :], red, bc, rw_ln_g[l][None, :], rw_ln_b[l][None, :],
                  w_br_ml[l].astype(BF16), w_br_rw[l].astype(BF16), w_br_ca[l].astype(BF16), w_o[l].astype(BF16),
                  ln1_g[l][None, :], ln1_b[l][None, :], rw_hi, rw_lo, rb)
        x1, x1b, route, counts = _merge(xf, xb, h_ml.reshape(n, ML_W), o_rwkv, bonus.reshape(n, RW_W),
                                        g.reshape(n, RW_W), h_ca.reshape(n, CA_W), consts, t["merge"], alpha)
        del x1b

        cnt = counts[0, :N_EXPERTS].astype(jnp.int32)
        blocks_per = (cnt + block - 1) // block
        blk_end = jnp.cumsum(blocks_per)
        slot_start = (blk_end - blocks_per) * block
        e_idx = route[:, ROUTE_IDX:ROUTE_IDX + TOP_K].astype(jnp.int32)
        rank = route[:, ROUTE_RANK:ROUTE_RANK + TOP_K].astype(jnp.int32)
        dest = (slot_start[e_idx] + rank).reshape(n_asg)
        blk_ids = jnp.arange(n_blocks, dtype=jnp.int32)
        blk_e = jnp.minimum(jnp.sum((blk_ids[:, None] >= blk_end[None, :]).astype(jnp.int32), axis=1),
                            N_EXPERTS - 1)
        nused = blk_end[-1:].astype(jnp.int32)

        last_blk = jnp.maximum(blk_end - 1, 0).astype(jnp.int32)
        trailing = jnp.minimum(nused[0] + jnp.arange(N_EXPERTS, dtype=jnp.int32), n_blocks - 1)
        zero_blk = jnp.concatenate([last_blk, trailing])
        xd = _dispatch(dest, zero_blk, x1, n_rows, t["moe_rows"], block)
        w_glu, w_lin = _deinterleave(w_gu[l])
        y_disp = _experts(blk_e, nused, xd, w_glu, w_lin,
                          b_gu[l][:, None, 0::2], b_gu[l][:, None, 1::2], w_dn[l].astype(BF16),
                          b_dn[l][:, None, :], block)
        xf, xb = _combine(dest, route, x1, y_disp, ln2_g[l], ln2_b[l], t["moe_rows"], alpha)

    del d_ff
    return xf.reshape(bsz, seq, d)
```

```python
import functools

import jax
import jax.numpy as jnp
from jax import lax
from jax.experimental import pallas as pl
from jax.experimental.pallas import tpu as pltpu

ML_HEADS, ML_DQK, ML_DV, ML_CONV = 4, 64, 128, 4
ML_W = ML_HEADS * ML_DV
ML_QK_W = 2 * ML_HEADS * ML_DQK
RW_HEADS, RW_DH = 8, 64
RW_W = RW_HEADS * RW_DH
RW_DECAY_LORA, RW_AAA_LORA, RW_GATE_LORA = 64, 64, 128
RW_GN_EPS = 64e-5
CA_HEADS, CA_DH = 4, 128
CA_W = CA_HEADS * CA_DH
N_BRANCH = 3
N_EXPERTS, TOP_K = 32, 4
SWIGLU_LIMIT, SWIGLU_ALPHA = 7.0, 1.702
LN_EPS = 1e-5

LANES = 128
SUBLANES = 8
VMEM_LIMIT = 56 * 1024 * 1024

BF16 = jnp.bfloat16
F32 = jnp.float32
NEG_BIG = -1e30


def _dot(a, b):
    return jnp.dot(a.astype(BF16), b.astype(BF16), preferred_element_type=F32)


def _split3(a):
    hi = a.astype(BF16)
    r1 = a - hi.astype(F32)
    mid = r1.astype(BF16)
    lo = (r1 - mid.astype(F32)).astype(BF16)
    return hi, mid, lo


def _dot2_rhs(a, b_bf16):
    hi = a.astype(BF16)
    lo = (a - hi.astype(F32)).astype(BF16)
    d = functools.partial(jnp.dot, preferred_element_type=F32)
    return d(hi, b_bf16) + d(lo, b_bf16)


def _head_sum(x, red, bc):
    return _dot2_rhs(_dot2_rhs(x, red), bc)


def _dot_exact_lhs(a_bf16, b):
    hi, mid, lo = _split3(b)
    d = functools.partial(jnp.dot, preferred_element_type=F32)
    return d(a_bf16, hi) + d(a_bf16, mid) + d(a_bf16, lo)


def _sigmoid(x):
    return 1.0 / (1.0 + jnp.exp(-x))


def _softplus(x):
    return jnp.maximum(x, 0.0) + jnp.log1p(jnp.exp(-jnp.abs(x)))


def _layer_norm(v, g, b, eps=LN_EPS):
    mu = jnp.mean(v, axis=-1, keepdims=True)
    c = v - mu
    var = jnp.mean(c * c, axis=-1, keepdims=True)
    return c * lax.rsqrt(var + eps) * g + b


def _params(*sem):
    return pltpu.CompilerParams(dimension_semantics=sem, vmem_limit_bytes=VMEM_LIMIT)


def _ordered_after(kernel, first, count):
    def wrapped(*refs):
        return kernel(*refs[:first], *refs[first + count:])
    return wrapped


def _after_specs(after):
    return [pl.BlockSpec(memory_space=pl.ANY)] * len(after)


def _const_spec(shape):
    nd = len(shape)
    return pl.BlockSpec(shape, lambda *_: (0,) * nd, pipeline_mode=pl.Buffered(1))


def _ln_kernel(x_ref, g_ref, b_ref, o_ref, ob_ref):
    y = _layer_norm(x_ref[...], g_ref[...], b_ref[...])
    o_ref[...] = y
    ob_ref[...] = y.astype(BF16)


def _ln_rows(x2d, g, b, tile):
    n, d = x2d.shape
    return pl.pallas_call(
        _ln_kernel,
        grid=(n // tile,),
        in_specs=[pl.BlockSpec((tile, d), lambda i: (i, 0)), _const_spec((1, d)), _const_spec((1, d))],
        out_specs=[pl.BlockSpec((tile, d), lambda i: (i, 0)), pl.BlockSpec((tile, d), lambda i: (i, 0))],
        out_shape=[jax.ShapeDtypeStruct((n, d), F32), jax.ShapeDtypeStruct((n, d), BF16)],
        compiler_params=_params("parallel"),
        name="ln_rows",
    )(x2d, g.reshape(1, d), b.reshape(1, d))


def _mm_kernel(a_ref, b_ref, o_ref):
    o_ref[...] = jnp.dot(a_ref[...], b_ref[...], preferred_element_type=F32)


def _matmul(a, b, tile):
    m, k = a.shape
    n = b.shape[1]
    return pl.pallas_call(
        _mm_kernel,
        grid=(m // tile,),
        in_specs=[pl.BlockSpec((tile, k), lambda i: (i, 0)), _const_spec((k, n))],
        out_specs=pl.BlockSpec((tile, n), lambda i: (i, 0)),
        out_shape=jax.ShapeDtypeStruct((m, n), F32),
        compiler_params=_params("parallel"),
        name="matmul",
    )(a, b)


def _mlstm_kernel(xb_ref, wml_ref, wg_ref, convw_ref, convb_ref, gb_ref, normg_ref, tri_ref,
                  h_ref, ubuf, c_ref, n_ref, m_ref, *, tile, chunk):
    @pl.when(pl.program_id(1) == 0)
    def _():
        ubuf[0:SUBLANES, :] = jnp.zeros((SUBLANES, ML_QK_W), F32)
        c_ref[...] = jnp.zeros_like(c_ref)
        n_ref[...] = jnp.zeros_like(n_ref)
        m_ref[...] = jnp.zeros_like(m_ref)

    xb = xb_ref[...]
    u = jnp.dot(xb, wml_ref[...], preferred_element_type=F32)
    gates = jnp.dot(xb, wg_ref[...], preferred_element_type=F32) + gb_ref[...]

    ubuf[SUBLANES:SUBLANES + tile, :] = u[:, :ML_QK_W]
    acc = jnp.broadcast_to(convb_ref[...], (tile, ML_QK_W))
    for j in range(ML_CONV):
        acc = acc + convw_ref[j:j + 1, :] * ubuf[pl.ds(SUBLANES - ML_CONV + 1 + j, tile), :]
    ubuf[0:SUBLANES, :] = ubuf[tile:tile + SUBLANES, :]
    qk = acc * _sigmoid(acc)
    q = qk[:, :ML_HEADS * ML_DQK] * (ML_DQK ** -0.5)
    k = qk[:, ML_HEADS * ML_DQK:]
    v = u[:, ML_QK_W:ML_QK_W + ML_W]
    og = u[:, ML_QK_W + ML_W:]

    lane = lax.broadcasted_iota(jnp.int32, (chunk, LANES), 1)
    row = lax.broadcasted_iota(jnp.int32, (chunk, chunk), 0)
    col = lax.broadcasted_iota(jnp.int32, (chunk, chunk), 1)
    causal = row >= col
    log_f = -_softplus(-gates)

    for c in range(tile // chunk):
        rs = slice(c * chunk, (c + 1) * chunk)
        bcum = _dot_exact_lhs(tri_ref[...], log_f[rs])
        mcol = jnp.where(lane < ML_HEADS, gates[rs], bcum)
        mrow = mcol.T
        for h in range(ML_HEADS):
            i_col = mcol[:, h:h + 1]
            b_col = mcol[:, ML_HEADS + h:ML_HEADS + h + 1]
            i_row = mrow[h:h + 1, :]
            b_row = mrow[ML_HEADS + h:ML_HEADS + h + 1, :]
            m_prev = m_ref[h][0:1, 0:1]
            n_prev = n_ref[h][0:1, :]
            c_prev = c_ref[h]
            qh = q[rs, h * ML_DQK:(h + 1) * ML_DQK]
            kh = k[rs, h * ML_DQK:(h + 1) * ML_DQK]
            vh = v[rs, h * ML_DV:(h + 1) * ML_DV]

            dmat = jnp.where(causal, b_col - b_row + i_row, -jnp.inf)
            m_inter = b_col + m_prev
            m_t = jnp.maximum(m_inter, jnp.max(dmat, axis=1, keepdims=True))
            s = lax.dot_general(qh.astype(BF16), kh.astype(BF16), (((1,), (1,)), ((), ())),
                                preferred_element_type=F32)
            w_intra = jnp.exp(dmat - m_t) * s
            s_inter = jnp.exp(m_inter - m_t)
            num = s_inter * _dot(qh, c_prev) + _dot(w_intra, vh)
            den = (s_inter * jnp.sum(qh * n_prev, axis=1, keepdims=True)
                   + jnp.sum(w_intra, axis=1, keepdims=True))
            hh = num / jnp.maximum(jnp.abs(den), jnp.exp(-m_t))

            b_last = b_col[chunk - 1:chunk, :]
            g_col = b_last - b_col + i_col
            m_new = jnp.maximum(b_last + m_prev, jnp.max(g_col, axis=0, keepdims=True))
            carry = jnp.exp(b_last + m_prev - m_new)
            kw = kh * jnp.exp(g_col - m_new)
            c_ref[h] = carry * c_prev + lax.dot_general(
                kw.astype(BF16), vh.astype(BF16), (((0,), (0,)), ((), ())), preferred_element_type=F32)
            n_new = carry * n_prev + jnp.sum(kw, axis=0, keepdims=True)
            n_ref[h] = jnp.broadcast_to(n_new, (SUBLANES, ML_DQK))
            m_ref[h] = jnp.broadcast_to(m_new, (SUBLANES, LANES))

            mu = jnp.mean(hh, axis=1, keepdims=True)
            cen = hh - mu
            var = jnp.mean(cen * cen, axis=1, keepdims=True)
            y = cen * lax.rsqrt(var + LN_EPS) * normg_ref[:, h * ML_DV:(h + 1) * ML_DV]
            h_ref[rs, h * ML_DV:(h + 1) * ML_DV] = (
                _sigmoid(og[rs, h * ML_DV:(h + 1) * ML_DV]) * y).astype(BF16)


def _mlstm(xb, wml, wg, convw, convb, gb, normg, tile, chunk, after=()):
    bsz, s, d = xb.shape
    tri = (jnp.arange(chunk)[:, None] >= jnp.arange(chunk)[None, :]).astype(BF16)
    kern = _ordered_after(functools.partial(_mlstm_kernel, tile=tile, chunk=chunk), 8, len(after))
    return pl.pallas_call(
        kern,
        grid=(bsz, s // tile),
        in_specs=[pl.BlockSpec((None, tile, d), lambda b, j: (b, j, 0)),
                  _const_spec(wml.shape), _const_spec(wg.shape), _const_spec(convw.shape),
                  _const_spec(convb.shape), _const_spec(gb.shape), _const_spec(normg.shape),
                  _const_spec(tri.shape)] + _after_specs(after),
        out_specs=pl.BlockSpec((None, tile, ML_W), lambda b, j: (b, j, 0)),
        out_shape=jax.ShapeDtypeStruct((bsz, s, ML_W), BF16),
        scratch_shapes=[pltpu.VMEM((tile + SUBLANES, ML_QK_W), F32),
                        pltpu.VMEM((ML_HEADS, ML_DQK, ML_DV), F32),
                        pltpu.VMEM((ML_HEADS, SUBLANES, ML_DQK), F32),
                        pltpu.VMEM((ML_HEADS, SUBLANES, LANES), F32)],
        compiler_params=_params("parallel", "arbitrary"),
        name="mlstm",
    )(xb, wml, wg, convw, convb, gb, normg, tri, *after)


RW_PAD_IN = 3 * RW_W + 3 * LANES
SCAN_R, SCAN_W, SCAN_K, SCAN_V, SCAN_KK, SCAN_KKA = range(6)
SCAN_OPERANDS = 6


def _rwkv_prep_kernel(xb_ref, wrw_ref, mu_ref, w0_ref, wup_ref, a0_ref, aup_ref, gup_ref, kkw_ref,
                      ka_ref, rk_ref, red_ref, bc_ref,
                      scan_ref, g_ref, bonus_ref, ubuf, *, tile):
    @pl.when(pl.program_id(1) == 0)
    def _():
        ubuf[0:SUBLANES, :] = jnp.zeros((SUBLANES, RW_PAD_IN), F32)

    u = jnp.dot(xb_ref[...], wrw_ref[...], preferred_element_type=F32)
    ubuf[SUBLANES:SUBLANES + tile, :] = u
    u_prev = ubuf[pl.ds(SUBLANES - 1, tile), :]
    ubuf[0:SUBLANES, :] = ubuf[tile:tile + SUBLANES, :]
    us = u + (u_prev - u) * mu_ref[...]
    r = us[:, 0:RW_W]
    kr = us[:, RW_W:2 * RW_W]
    vr = us[:, 2 * RW_W:3 * RW_W]
    wd = us[:, 3 * RW_W:3 * RW_W + LANES]
    ad = us[:, 3 * RW_W + LANES:3 * RW_W + 2 * LANES]
    gd = us[:, 3 * RW_W + 2 * LANES:]

    w_log = -_softplus(-(w0_ref[...] + _dot(jnp.tanh(wd), wup_ref[...]))) - 0.5
    decay = jnp.exp(-jnp.exp(w_log))
    a = _sigmoid(a0_ref[...] + _dot(ad, aup_ref[...]))
    g = _dot(_sigmoid(gd), gup_ref[...])

    red, bc = red_ref[...], bc_ref[...]
    kk = kr * kkw_ref[...]
    kk = kk * lax.rsqrt(jnp.maximum(_head_sum(kk * kk, red, bc), 1e-24))
    k2 = kr * (1.0 + (a - 1.0) * ka_ref[...])
    bonus = _head_sum(r * k2 * rk_ref[...], red, bc) * vr

    scan_ref[SCAN_R] = r
    scan_ref[SCAN_W] = decay
    scan_ref[SCAN_K] = k2
    scan_ref[SCAN_V] = vr
    scan_ref[SCAN_KK] = kk
    scan_ref[SCAN_KKA] = kk * a
    g_ref[...] = g
    bonus_ref[...] = bonus


def _rwkv_prep(xb, wrw, mu, w0, wup, a0, aup, gup, kkw, ka, rk, red, bc, tile):
    bsz, s, d = xb.shape
    consts = (wrw, mu, w0, wup, a0, aup, gup, kkw, ka, rk, red, bc)
    spec = pl.BlockSpec((None, tile, RW_W), lambda b, j: (b, j, 0))
    tspec = pl.BlockSpec((SCAN_OPERANDS, tile, RW_W), lambda b, j: (0, j, b))
    return pl.pallas_call(
        functools.partial(_rwkv_prep_kernel, tile=tile),
        grid=(bsz, s // tile),
        in_specs=[pl.BlockSpec((None, tile, d), lambda b, j: (b, j, 0))] + [_const_spec(c.shape) for c in consts],
        out_specs=[tspec, spec, spec],
        out_shape=[jax.ShapeDtypeStruct((SCAN_OPERANDS, s, bsz * RW_W), F32)]
        + [jax.ShapeDtypeStruct((bsz, s, RW_W), F32)] * 2,
        scratch_shapes=[pltpu.VMEM((tile + SUBLANES, RW_PAD_IN), F32)],
        compiler_params=_params("parallel", "arbitrary"),
        name="rwkv_prep",
    )(xb, *consts)


def _rwkv_scan_kernel(x_ref, o_ref, st_ref, *, steps):
    @pl.when(pl.program_id(1) == 0)
    def _():
        st_ref[...] = jnp.zeros_like(st_ref)

    r_ref, w_ref, k_ref, v_ref, kk_ref, kka_ref = (
        x_ref.at[i] for i in (SCAN_R, SCAN_W, SCAN_K, SCAN_V, SCAN_KK, SCAN_KKA))
    zeros = jnp.zeros((RW_DH, LANES), F32)

    def first(kc, sa):
        return sa + st_ref[kc] * kk_ref[0, pl.ds(kc, 1), :]

    sa0 = lax.fori_loop(0, RW_DH, first, zeros, unroll=8)

    def step(t, sa):
        vt = v_ref[t]
        tn = jnp.minimum(t + 1, steps - 1)

        def body(kc, carry):
            out, sa_next = carry
            row = pl.ds(kc, 1)
            new = st_ref[kc] * w_ref[t, row, :] - sa * kka_ref[t, row, :] + vt * k_ref[t, row, :]
            st_ref[kc] = new
            return out + new * r_ref[t, row, :], sa_next + new * kk_ref[tn, row, :]

        out, sa_next = lax.fori_loop(0, RW_DH, body, (zeros, zeros), unroll=16)
        o_ref[t] = out
        return sa_next

    lax.fori_loop(0, steps, step, sa0)


def _rwkv_scan(x, steps, after=()):
    _, s, dh, nl = x.shape
    spec = pl.BlockSpec((steps, dh, LANES), lambda g, j: (j, 0, g))
    return pl.pallas_call(
        _ordered_after(functools.partial(_rwkv_scan_kernel, steps=steps), 1, len(after)),
        grid=(nl // LANES, s // steps),
        in_specs=[pl.BlockSpec((SCAN_OPERANDS, steps, dh, LANES), lambda g, j: (0, j, 0, g))]
        + _after_specs(after),
        out_specs=spec,
        out_shape=jax.ShapeDtypeStruct((s, dh, nl), F32),
        scratch_shapes=[pltpu.VMEM((dh, dh, LANES), F32)],
        compiler_params=_params("parallel", "arbitrary"),
        name="rwkv_scan",
    )(x, *after)


def _xattn_kernel(xb_ref, wq_ref, kt_ref, v_ref, o_ref):
    q = jnp.dot(xb_ref[...], wq_ref[...], preferred_element_type=F32)
    for h in range(CA_HEADS):
        hs = slice(h * CA_DH, (h + 1) * CA_DH)
        s = jnp.dot(q[:, hs].astype(BF16), kt_ref[hs, :], preferred_element_type=F32) * (CA_DH ** -0.5)
        p = jnp.exp(s - jnp.max(s, axis=1, keepdims=True))
        den = jnp.sum(p, axis=1, keepdims=True)
        o = jnp.dot(p.astype(BF16), v_ref[:, hs], preferred_element_type=F32) / den
        o_ref[:, hs] = o.astype(BF16)


def _xattn(xb, wq, kt, v, tile, after=()):
    bsz, s, d = xb.shape
    m = v.shape[1]
    return pl.pallas_call(
        _ordered_after(_xattn_kernel, 4, len(after)),
        grid=(bsz, s // tile),
        in_specs=[pl.BlockSpec((None, tile, d), lambda b, j: (b, j, 0)), _const_spec(wq.shape),
                  pl.BlockSpec((None, CA_W, m), lambda b, j: (b, 0, 0)),
                  pl.BlockSpec((None, m, CA_W), lambda b, j: (b, 0, 0))] + _after_specs(after),
        out_specs=pl.BlockSpec((None, tile, CA_W), lambda b, j: (b, j, 0)),
        out_shape=jax.ShapeDtypeStruct((bsz, s, CA_W), BF16),
        compiler_params=_params("parallel", "parallel"),
        name="xattn",
    )(xb, wq, kt, v, *after)


ROUTE_IDX, ROUTE_RANK, ROUTE_GATE = 0, TOP_K, 2 * TOP_K


def _merge_kernel(x_ref, xb_ref, hml_ref, o_ref, bonus_ref, g_ref, hca_ref,
                  wgate_ref, gateb_ref, red_ref, bc_ref, rwg_ref, rwb_ref, wml_ref, wrw_ref, wca_ref, wo_ref,
                  lng_ref, lnb_ref, rwhi_ref, rwlo_ref, rb_ref, tri_ref,
                  x1_ref, x1b_ref, route_ref, cnt_ref, carry_ref, *, tile, alpha, d_model):
    @pl.when(pl.program_id(0) == 0)
    def _():
        carry_ref[...] = jnp.zeros_like(carry_ref)

    red, bc = red_ref[...], bc_ref[...]
    o = o_ref[...]
    mu = _head_sum(o, red, bc) * (1.0 / RW_DH)
    cen = o - mu
    var = _head_sum(cen * cen, red, bc) * (1.0 / RW_DH)
    h_rw = (cen * lax.rsqrt(var + RW_GN_EPS) * rwg_ref[...] + rwb_ref[...] + bonus_ref[...]) * g_ref[...]

    xb = xb_ref[...]
    y = None
    for br, (h_br, w_ref) in enumerate(((hml_ref[...], wml_ref), (h_rw.astype(BF16), wrw_ref),
                                        (hca_ref[...], wca_ref))):
        cols = slice(br * d_model, (br + 1) * d_model)
        gate = _sigmoid(jnp.dot(xb, wgate_ref[:, cols], preferred_element_type=F32) + gateb_ref[:, cols])
        term = gate * jnp.dot(h_br, w_ref[...], preferred_element_type=F32)
        y = term if y is None else y + term
    mixed = _dot(y, wo_ref[...])
    x1 = _layer_norm(alpha * x_ref[...] + mixed, lng_ref[...], lnb_ref[...])
    x1_ref[...] = x1
    x1b_ref[...] = x1.astype(BF16)

    hi = x1.astype(BF16)
    lo = (x1 - hi.astype(F32)).astype(BF16)
    d = functools.partial(jnp.dot, preferred_element_type=F32)
    logits = d(hi, rwhi_ref[...]) + d(lo, rwhi_ref[...]) + d(hi, rwlo_ref[...]) + rb_ref[...]

    lane = lax.broadcasted_iota(jnp.int32, (tile, LANES), 1)
    vals = logits
    tops, onehots = [], []
    for _ in range(TOP_K):
        m = jnp.max(vals, axis=1, keepdims=True)
        idx = jnp.min(jnp.where(vals == m, lane, LANES), axis=1, keepdims=True)
        sel = lane == idx
        tops.append((m, idx))
        onehots.append(sel.astype(F32))
        vals = jnp.where(sel, -jnp.inf, vals)
    exps = [jnp.exp(m - tops[0][0]) for m, _ in tops]
    den = exps[0] + exps[1] + exps[2] + exps[3]

    cnt = onehots[0] + onehots[1] + onehots[2] + onehots[3]
    before = jnp.dot(tri_ref[...], cnt.astype(BF16), preferred_element_type=F32) + carry_ref[0:1, :]
    route = jnp.zeros((tile, LANES), F32)
    for kq in range(TOP_K):
        rank = jnp.sum(onehots[kq] * before, axis=1, keepdims=True)
        route = jnp.where(lane == ROUTE_IDX + kq, tops[kq][1].astype(F32), route)
        route = jnp.where(lane == ROUTE_RANK + kq, rank, route)
        route = jnp.where(lane == ROUTE_GATE + kq, exps[kq] / den, route)
    route_ref[...] = route
    total = carry_ref[0:1, :] + jnp.sum(cnt, axis=0, keepdims=True)
    carry_ref[...] = jnp.broadcast_to(total, carry_ref.shape)
    cnt_ref[...] = jnp.broadcast_to(total, cnt_ref.shape)


def _merge(x, xb, hml, o_tm, bonus, g, hca, consts, tile, alpha):
    n, d = x.shape
    tiles_per_seq = o_tm.shape[0] // tile
    tri = (jnp.arange(tile)[:, None] > jnp.arange(tile)[None, :]).astype(BF16)
    consts = tuple(consts) + (tri,)

    def rows(w):
        return pl.BlockSpec((tile, w), lambda i: (i, 0))

    o_spec = pl.BlockSpec((tile, RW_W), lambda i: (i % tiles_per_seq, i // tiles_per_seq))
    return pl.pallas_call(
        functools.partial(_merge_kernel, tile=tile, alpha=alpha, d_model=d),
        grid=(n // tile,),
        in_specs=[rows(d), rows(d), rows(ML_W), o_spec, rows(RW_W), rows(RW_W), rows(CA_W)]
        + [_const_spec(c.shape) for c in consts],
        out_specs=[rows(d), rows(d), rows(LANES), _const_spec((SUBLANES, LANES))],
        out_shape=[jax.ShapeDtypeStruct((n, d), F32), jax.ShapeDtypeStruct((n, d), BF16),
                   jax.ShapeDtypeStruct((n, LANES), F32), jax.ShapeDtypeStruct((SUBLANES, LANES), F32)],
        scratch_shapes=[pltpu.VMEM((SUBLANES, LANES), F32)],
        compiler_params=_params("arbitrary"),
        name="merge_route",
    )(x, xb, hml, o_tm, bonus, g, hca, *consts)


def _row_copy(src_ref, src_row, dst_ref, dst_row, sem):
    return pltpu.make_async_copy(src_ref.at[pl.ds(src_row, 1)], dst_ref.at[pl.ds(dst_row, 1)], sem)


def _dispatch_kernel(dest_ref, zero_blk_ref, x_ref, xd_ref, xbuf, zbuf, sems, zsem, *, tile, nsteps, block):
    i = pl.program_id(0)
    slot = i % 2

    @pl.when(i == 0)
    def _():
        zbuf[...] = jnp.zeros_like(zbuf)
        n_zero = zero_blk_ref.shape[0]

        def fresh(j):
            return jnp.logical_or(j == 0, zero_blk_ref[j] != zero_blk_ref[jnp.maximum(j - 1, 0)])

        def zero_copy(j):
            return pltpu.make_async_copy(zbuf, xd_ref.at[pl.ds(zero_blk_ref[j] * block, block)], zsem)

        def start(j, carry):
            @pl.when(fresh(j))
            def _():
                zero_copy(j).start()
            return carry

        def wait(j, carry):
            @pl.when(fresh(j))
            def _():
                zero_copy(j).wait()
            return carry

        lax.fori_loop(0, n_zero, start, 0)
        lax.fori_loop(0, n_zero, wait, 0)

    def retire(s):
        for _ in range(TOP_K):
            pltpu.make_async_copy(xbuf.at[s], xd_ref.at[pl.ds(0, tile)], sems.at[s]).wait()

    @pl.when(i >= 2)
    def _():
        retire(slot)

    xbuf[slot] = x_ref[...]

    def issue(r, carry):
        for kq in range(TOP_K):
            _row_copy(xbuf.at[slot], r, xd_ref, dest_ref[r * TOP_K + kq], sems.at[slot]).start()
        return carry

    lax.fori_loop(0, tile, issue, 0)

    @pl.when(i == nsteps - 1)
    def _():
        retire(slot)
        if nsteps > 1:
            retire(1 - slot)


def _dispatch(dest_flat, zero_blk, x1, n_rows, tile, block):
    n, d = x1.shape
    nsteps = n // tile
    return pl.pallas_call(
        functools.partial(_dispatch_kernel, tile=tile, nsteps=nsteps, block=block),
        grid=(nsteps,),
        in_specs=[pl.BlockSpec((tile * TOP_K,), lambda i: (i,), memory_space=pltpu.SMEM),
                  pl.BlockSpec(memory_space=pltpu.SMEM),
                  pl.BlockSpec((tile, d), lambda i: (i, 0))],
        out_specs=pl.BlockSpec(memory_space=pl.ANY),
        out_shape=jax.ShapeDtypeStruct((n_rows, d), F32),
        scratch_shapes=[pltpu.VMEM((2, tile, d), F32), pltpu.VMEM((block, d), F32),
                        pltpu.SemaphoreType.DMA((2,)), pltpu.SemaphoreType.DMA],
        compiler_params=_params("arbitrary"),
        name="moe_dispatch",
    )(dest_flat, zero_blk, x1)


DEINT_COLS = 512


def _deinterleave_kernel(w_ref, pe_ref, po_ref, glu_ref, lin_ref):
    wb = w_ref[...].astype(BF16)
    glu_ref[...] = jnp.dot(wb, pe_ref[...], preferred_element_type=F32).astype(BF16)
    lin_ref[...] = jnp.dot(wb, po_ref[...], preferred_element_type=F32).astype(BF16)


def _deinterleave(w_gu_all, layer):
    _, e, d, two_ff = w_gu_all.shape
    half = DEINT_COLS // 2
    src = jnp.arange(DEINT_COLS)[:, None]
    dst = jnp.arange(half)[None, :]
    pe = (src == 2 * dst).astype(BF16)
    po = (src == 2 * dst + 1).astype(BF16)
    out = jax.ShapeDtypeStruct((e, d, two_ff // 2), BF16)
    return pl.pallas_call(
        _deinterleave_kernel,
        grid=(e, two_ff // DEINT_COLS),
        in_specs=[pl.BlockSpec((None, None, d, DEINT_COLS), lambda i, c: (layer, i, 0, c)),
                  _const_spec(pe.shape), _const_spec(po.shape)],
        out_specs=[pl.BlockSpec((None, d, half), lambda i, c: (i, 0, c))] * 2,
        out_shape=[out, out],
        compiler_params=_params("parallel", "parallel"),
        name="deinterleave_w",
    )(w_gu_all, pe, po)


def _expert_kernel(blk_e_ref, nused_ref, xd_ref, wglu_ref, wlin_ref, bglu_ref, blin_ref, wdn_ref, bdn_ref, y_ref):
    del blk_e_ref
    live = pl.program_id(0) < nused_ref[0]

    @pl.when(live)
    def _():
        xb = xd_ref[...].astype(BF16)
        h_glu = jnp.dot(xb, wglu_ref[...], preferred_element_type=F32) + bglu_ref[...]
        h_lin = jnp.dot(xb, wlin_ref[...], preferred_element_type=F32) + blin_ref[...]
        x_glu = jnp.minimum(h_glu, SWIGLU_LIMIT)
        x_lin = jnp.clip(h_lin, -SWIGLU_LIMIT, SWIGLU_LIMIT)
        act = x_glu * _sigmoid(SWIGLU_ALPHA * x_glu) * (x_lin + 1.0)
        y_ref[...] = _dot(act, wdn_ref[...]) + bdn_ref[...]

    @pl.when(jnp.logical_not(live))
    def _():
        y_ref[...] = jnp.zeros_like(y_ref)


def _experts(blk_e, nused, xd, wglu, wlin, bglu, blin, wdn, bdn, block):
    n_rows, d = xd.shape
    dff = wglu.shape[2]
    grid_spec = pltpu.PrefetchScalarGridSpec(
        num_scalar_prefetch=2,
        grid=(n_rows // block,),
        in_specs=[pl.BlockSpec((block, d), lambda i, be, nu: (jnp.minimum(i, nu[0] - 1), 0)),
                  pl.BlockSpec((None, d, dff), lambda i, be, nu: (be[i], 0, 0)),
                  pl.BlockSpec((None, d, dff), lambda i, be, nu: (be[i], 0, 0)),
                  pl.BlockSpec((None, 1, dff), lambda i, be, nu: (be[i], 0, 0)),
                  pl.BlockSpec((None, 1, dff), lambda i, be, nu: (be[i], 0, 0)),
                  pl.BlockSpec((None, dff, d), lambda i, be, nu: (be[i], 0, 0)),
                  pl.BlockSpec((None, 1, d), lambda i, be, nu: (be[i], 0, 0))],
        out_specs=pl.BlockSpec((block, d), lambda i, be, nu: (i, 0)),
    )
    return pl.pallas_call(
        _expert_kernel,
        grid_spec=grid_spec,
        out_shape=jax.ShapeDtypeStruct((n_rows, d), F32),
        compiler_params=_params("arbitrary"),
        name="moe_experts",
    )(blk_e, nused, xd, wglu, wlin, bglu, blin, wdn, bdn)


def _combine_kernel(dest_ref, dest_next_ref, route_ref, x1_ref, y_ref, lng_ref, lnb_ref, x2_ref, x2b_ref,
                    ybuf, sems, *, tile, alpha, nsteps):
    i = pl.program_id(0)
    slot = i % 2

    def fetch(d_ref, s):
        def issue(r, carry):
            for kq in range(TOP_K):
                _row_copy(y_ref, d_ref[r * TOP_K + kq], ybuf.at[s, kq], r, sems.at[s]).start()
            return carry

        lax.fori_loop(0, tile, issue, 0, unroll=4)

    @pl.when(i == 0)
    def _():
        fetch(dest_ref, slot)

    @pl.when(i + 1 < nsteps)
    def _():
        fetch(dest_next_ref, 1 - slot)

    for kq in range(TOP_K):
        pltpu.make_async_copy(y_ref.at[pl.ds(0, tile)], ybuf.at[slot, kq], sems.at[slot]).wait()
    route = route_ref[...]
    moe = route[:, ROUTE_GATE:ROUTE_GATE + 1] * ybuf[slot, 0]
    for kq in range(1, TOP_K):
        moe = moe + route[:, ROUTE_GATE + kq:ROUTE_GATE + kq + 1] * ybuf[slot, kq]
    x2 = _layer_norm(alpha * x1_ref[...] + moe, lng_ref[...], lnb_ref[...])
    x2_ref[...] = x2
    x2b_ref[...] = x2.astype(BF16)


def _combine(dest_flat, route, x1, y_disp, lng, lnb, tile, alpha):
    n, d = x1.shape
    nsteps = n // tile
    return pl.pallas_call(
        functools.partial(_combine_kernel, tile=tile, alpha=alpha, nsteps=nsteps),
        grid=(nsteps,),
        in_specs=[pl.BlockSpec((tile * TOP_K,), lambda i: (i,), memory_space=pltpu.SMEM),
                  pl.BlockSpec((tile * TOP_K,), lambda i: (jnp.minimum(i + 1, nsteps - 1),),
                               memory_space=pltpu.SMEM),
                  pl.BlockSpec((tile, LANES), lambda i: (i, 0)),
                  pl.BlockSpec((tile, d), lambda i: (i, 0)),
                  pl.BlockSpec(memory_space=pl.ANY),
                  _const_spec((1, d)), _const_spec((1, d))],
        out_specs=[pl.BlockSpec((tile, d), lambda i: (i, 0)), pl.BlockSpec((tile, d), lambda i: (i, 0))],
        out_shape=[jax.ShapeDtypeStruct((n, d), F32), jax.ShapeDtypeStruct((n, d), BF16)],
        scratch_shapes=[pltpu.VMEM((2, TOP_K, tile, d), F32), pltpu.SemaphoreType.DMA((2,))],
        compiler_params=_params("arbitrary"),
        name="moe_combine",
    )(dest_flat, dest_flat, route, x1, y_disp, lng.reshape(1, d), lnb.reshape(1, d))


def _tiles(bsz, seq):
    n = bsz * seq
    return dict(
        ln=min(1024, n),
        mlstm=min(512, seq), mlstm_chunk=min(128, seq),
        rwkv_prep=min(512, seq),
        scan_steps=min(32, seq),
        xattn=min(512, seq),
        merge=min(512, seq),
        moe_rows=min(256, n),
        moe_block=512,
    )


def _pad_cols(w, width):
    return jnp.pad(w, ((0, 0), (0, width - w.shape[1])))


def kernel(x, mem, ln_in_g, ln_in_b, mem_ln_g, mem_ln_b, w_in, ml_conv_w, ml_conv_b, ml_ig_b, ml_fg_b, ml_norm_g, rw_mu, rw_w0, rw_w_up, rw_a0, rw_a_up, rw_g_up, rw_kk, rw_ka, rw_rk, rw_ln_g, rw_ln_b, ca_w_kv, gate_b, w_br_ml, w_br_rw, w_br_ca, w_o, ln1_g, ln1_b, router_w, router_b, w_gu, b_gu, w_dn, b_dn, ln2_g, ln2_b):
    bsz, seq, d = x.shape
    mem_len = mem.shape[1]
    depth = w_in.shape[0]
    n = bsz * seq
    t = _tiles(bsz, seq)
    alpha = (2 * depth) ** 0.25
    d_ff = w_dn.shape[2]

    xf, xb = _ln_rows(x.reshape(n, d), ln_in_g, ln_in_b, t["ln"])
    _, memb = _ln_rows(mem.reshape(bsz * mem_len, d), mem_ln_g, mem_ln_b, min(t["ln"], bsz * mem_len))

    o_qk, o_v, o_og = 0, ML_QK_W, ML_QK_W + ML_W
    o_ig = o_og + ML_W
    o_fg = o_ig + ML_HEADS
    o_rw = o_fg + ML_HEADS
    o_ca = o_rw + 3 * RW_W + RW_DECAY_LORA + RW_AAA_LORA + RW_GATE_LORA
    o_gate = o_ca + CA_W

    head_of_lane = jnp.arange(RW_W) // RW_DH
    red = (head_of_lane[:, None] == jnp.arange(LANES)[None, :]).astype(BF16)
    bc = red.T
    n_asg = n * TOP_K
    block = t["moe_block"]
    n_blocks = -(-n_asg // block) + N_EXPERTS
    n_rows = n_blocks * block

    for l in range(depth):
        w = w_in[l]
        o_wd = o_rw + 3 * RW_W
        o_ad = o_wd + RW_DECAY_LORA
        o_gd = o_ad + RW_AAA_LORA
        wrw = jnp.concatenate([w[:, o_rw:o_wd], _pad_cols(w[:, o_wd:o_ad], LANES),
                               _pad_cols(w[:, o_ad:o_gd], LANES), w[:, o_gd:o_ca]], axis=1).astype(BF16)
        mu = rw_mu[l]
        mu_p = jnp.concatenate([mu[:3 * RW_W], jnp.pad(mu[3 * RW_W:3 * RW_W + RW_DECAY_LORA], (0, LANES - RW_DECAY_LORA)),
                                jnp.pad(mu[3 * RW_W + RW_DECAY_LORA:3 * RW_W + RW_DECAY_LORA + RW_AAA_LORA],
                                        (0, LANES - RW_AAA_LORA)),
                                mu[3 * RW_W + RW_DECAY_LORA + RW_AAA_LORA:]])[None, :]
        wup = jnp.pad(rw_w_up[l], ((0, LANES - RW_DECAY_LORA), (0, 0))).astype(BF16)
        aup = jnp.pad(rw_a_up[l], ((0, LANES - RW_AAA_LORA), (0, 0))).astype(BF16)
        scan_in, g, bonus = _rwkv_prep(
            xb.reshape(bsz, seq, d), wrw, mu_p, rw_w0[l][None, :], wup, rw_a0[l][None, :], aup,
            rw_g_up[l].astype(BF16), rw_kk[l][None, :], rw_ka[l][None, :], rw_rk[l].reshape(1, RW_W), red, bc,
            t["rwkv_prep"])

        nl = bsz * RW_HEADS
        nl_pad = -(-nl // LANES) * LANES
        ops = scan_in.reshape(SCAN_OPERANDS, seq, nl, RW_DH).transpose(0, 1, 3, 2)
        if nl_pad != nl:
            ops = jnp.pad(ops, ((0, 0), (0, 0), (0, 0), (0, nl_pad - nl)))

        wml = w[:, o_qk:o_ig].astype(BF16)
        wg = _pad_cols(w[:, o_ig:o_rw], LANES).astype(BF16)
        gb = _pad_cols(jnp.concatenate([ml_ig_b[l], ml_fg_b[l]])[None, :], LANES)
        h_ml = _mlstm(xb.reshape(bsz, seq, d), wml, wg, ml_conv_w[l], ml_conv_b[l][None, :], gb,
                      ml_norm_g[l][None, :], t["mlstm"], t["mlstm_chunk"], after=(g,))

        kv = _matmul(memb, ca_w_kv[l].astype(BF16), min(512, bsz * mem_len)).reshape(bsz, mem_len, 2 * CA_W)
        kt = kv[:, :, :CA_W].transpose(0, 2, 1).astype(BF16)
        vm = kv[:, :, CA_W:].astype(BF16)
        h_ca = _xattn(xb.reshape(bsz, seq, d), w[:, o_ca:o_gate].astype(BF16), kt, vm, t["xattn"], after=(g,))

        o_scan = _rwkv_scan(ops, t["scan_steps"], after=(h_ml, h_ca))[:, :, :nl]
        o_rwkv = o_scan.transpose(0, 2, 1).reshape(seq, bsz * RW_W)

        rw_pad = _pad_cols(router_w[l], LANES)
        rw_hi = rw_pad.astype(BF16)
        rw_lo = (rw_pad - rw_hi.astype(F32)).astype(BF16)
        rb = jnp.concatenate([router_b[l], jnp.full((LANES - N_EXPERTS,), NEG_BIG, F32)])[None, :]
        consts = (w[:, o_gate:].astype(BF16), gate_b[l][None, :], red, bc, rw_ln_g[l][None, :], rw_ln_b[l][None, :],
                  w_br_ml[l].astype(BF16), w_br_rw[l].astype(BF16), w_br_ca[l].astype(BF16), w_o[l].astype(BF16),
                  ln1_g[l][None, :], ln1_b[l][None, :], rw_hi, rw_lo, rb)
        x1, x1b, route, counts = _merge(xf, xb, h_ml.reshape(n, ML_W), o_rwkv, bonus.reshape(n, RW_W),
                                        g.reshape(n, RW_W), h_ca.reshape(n, CA_W), consts, t["merge"], alpha)
        del x1b

        cnt = counts[0, :N_EXPERTS].astype(jnp.int32)
        blocks_per = (cnt + block - 1) // block
        blk_end = jnp.cumsum(blocks_per)
        slot_start = (blk_end - blocks_per) * block
        e_idx = route[:, ROUTE_IDX:ROUTE_IDX + TOP_K].astype(jnp.int32)
        rank = route[:, ROUTE_RANK:ROUTE_RANK + TOP_K].astype(jnp.int32)
        dest = (slot_start[e_idx] + rank).reshape(n_asg)
        blk_ids = jnp.arange(n_blocks, dtype=jnp.int32)
        blk_e = jnp.minimum(jnp.sum((blk_ids[:, None] >= blk_end[None, :]).astype(jnp.int32), axis=1),
                            N_EXPERTS - 1)
        nused = blk_end[-1:].astype(jnp.int32)

        last_blk = jnp.maximum(blk_end - 1, 0).astype(jnp.int32)
        trailing = jnp.minimum(nused[0] + jnp.arange(N_EXPERTS, dtype=jnp.int32), n_blocks - 1)
        zero_blk = jnp.concatenate([last_blk, trailing])
        xd = _dispatch(dest, zero_blk, x1, n_rows, t["moe_rows"], block)
        w_glu, w_lin = _deinterleave(w_gu, l)
        y_disp = _experts(blk_e, nused, xd, w_glu, w_lin,
                          b_gu[l][:, None, 0::2], b_gu[l][:, None, 1::2], w_dn[l].astype(BF16),
                          b_dn[l][:, None, :], block)
        xf, xb = _combine(dest, route, x1, y_disp, ln2_g[l], ln2_b[l], t["moe_rows"], alpha)

    del d_ff
    return xf.reshape(bsz, seq, d)
```

```python
import functools

import jax
import jax.numpy as jnp
from jax import lax
from jax.experimental import pallas as pl
from jax.experimental.pallas import tpu as pltpu

ML_HEADS, ML_DQK, ML_DV, ML_CONV = 4, 64, 128, 4
ML_W = ML_HEADS * ML_DV
ML_QK_W = 2 * ML_HEADS * ML_DQK
RW_HEADS, RW_DH = 8, 64
RW_W = RW_HEADS * RW_DH
RW_DECAY_LORA, RW_AAA_LORA, RW_GATE_LORA = 64, 64, 128
RW_GN_EPS = 64e-5
CA_HEADS, CA_DH = 4, 128
CA_W = CA_HEADS * CA_DH
N_BRANCH = 3
N_EXPERTS, TOP_K = 32, 4
SWIGLU_LIMIT, SWIGLU_ALPHA = 7.0, 1.702
LN_EPS = 1e-5

LANES = 128
SUBLANES = 8
VMEM_LIMIT = 56 * 1024 * 1024

BF16 = jnp.bfloat16
F32 = jnp.float32
NEG_BIG = -1e30


def _dot(a, b):
    return jnp.dot(a.astype(BF16), b.astype(BF16), preferred_element_type=F32)


def _split3(a):
    hi = a.astype(BF16)
    r1 = a - hi.astype(F32)
    mid = r1.astype(BF16)
    lo = (r1 - mid.astype(F32)).astype(BF16)
    return hi, mid, lo


def _dot2_rhs(a, b_bf16):
    hi = a.astype(BF16)
    lo = (a - hi.astype(F32)).astype(BF16)
    d = functools.partial(jnp.dot, preferred_element_type=F32)
    return d(hi, b_bf16) + d(lo, b_bf16)


def _head_sum(x, red, bc):
    return _dot2_rhs(_dot2_rhs(x, red), bc)


def _dot_exact_lhs(a_bf16, b):
    hi, mid, lo = _split3(b)
    d = functools.partial(jnp.dot, preferred_element_type=F32)
    return d(a_bf16, hi) + d(a_bf16, mid) + d(a_bf16, lo)


def _sigmoid(x):
    return 1.0 / (1.0 + jnp.exp(-x))


def _softplus(x):
    return jnp.maximum(x, 0.0) + jnp.log1p(jnp.exp(-jnp.abs(x)))


def _layer_norm(v, g, b, eps=LN_EPS):
    mu = jnp.mean(v, axis=-1, keepdims=True)
    c = v - mu
    var = jnp.mean(c * c, axis=-1, keepdims=True)
    return c * lax.rsqrt(var + eps) * g + b


def _params(*sem):
    return pltpu.CompilerParams(dimension_semantics=sem, vmem_limit_bytes=VMEM_LIMIT)


def _ordered_after(kernel, first, count):
    def wrapped(*refs):
        return kernel(*refs[:first], *refs[first + count:])
    return wrapped


def _after_specs(after):
    return [pl.BlockSpec(memory_space=pl.ANY)] * len(after)


def _const_spec(shape):
    nd = len(shape)
    return pl.BlockSpec(shape, lambda *_: (0,) * nd, pipeline_mode=pl.Buffered(1))


def _ln_kernel(x_ref, g_ref, b_ref, o_ref, ob_ref):
    y = _layer_norm(x_ref[...], g_ref[...], b_ref[...])
    o_ref[...] = y
    ob_ref[...] = y.astype(BF16)


def _ln_rows(x2d, g, b, tile):
    n, d = x2d.shape
    return pl.pallas_call(
        _ln_kernel,
        grid=(n // tile,),
        in_specs=[pl.BlockSpec((tile, d), lambda i: (i, 0)), _const_spec((1, d)), _const_spec((1, d))],
        out_specs=[pl.BlockSpec((tile, d), lambda i: (i, 0)), pl.BlockSpec((tile, d), lambda i: (i, 0))],
        out_shape=[jax.ShapeDtypeStruct((n, d), F32), jax.ShapeDtypeStruct((n, d), BF16)],
        compiler_params=_params("parallel"),
        name="ln_rows",
    )(x2d, g.reshape(1, d), b.reshape(1, d))


def _mm_kernel(a_ref, b_ref, o_ref):
    o_ref[...] = jnp.dot(a_ref[...], b_ref[...], preferred_element_type=F32)


def _matmul(a, b, tile):
    m, k = a.shape
    n = b.shape[1]
    return pl.pallas_call(
        _mm_kernel,
        grid=(m // tile,),
        in_specs=[pl.BlockSpec((tile, k), lambda i: (i, 0)), _const_spec((k, n))],
        out_specs=pl.BlockSpec((tile, n), lambda i: (i, 0)),
        out_shape=jax.ShapeDtypeStruct((m, n), F32),
        compiler_params=_params("parallel"),
        name="matmul",
    )(a, b)


def _mlstm_kernel(xb_ref, wml_ref, wg_ref, convw_ref, convb_ref, gb_ref, normg_ref, tri_ref,
                  h_ref, ubuf, c_ref, n_ref, m_ref, *, tile, chunk):
    @pl.when(pl.program_id(1) == 0)
    def _():
        ubuf[0:SUBLANES, :] = jnp.zeros((SUBLANES, ML_QK_W), F32)
        c_ref[...] = jnp.zeros_like(c_ref)
        n_ref[...] = jnp.zeros_like(n_ref)
        m_ref[...] = jnp.zeros_like(m_ref)

    xb = xb_ref[...]
    u = jnp.dot(xb, wml_ref[...], preferred_element_type=F32)
    gates = jnp.dot(xb, wg_ref[...], preferred_element_type=F32) + gb_ref[...]

    ubuf[SUBLANES:SUBLANES + tile, :] = u[:, :ML_QK_W]
    acc = jnp.broadcast_to(convb_ref[...], (tile, ML_QK_W))
    for j in range(ML_CONV):
        acc = acc + convw_ref[j:j + 1, :] * ubuf[pl.ds(SUBLANES - ML_CONV + 1 + j, tile), :]
    ubuf[0:SUBLANES, :] = ubuf[tile:tile + SUBLANES, :]
    qk = acc * _sigmoid(acc)
    q = qk[:, :ML_HEADS * ML_DQK] * (ML_DQK ** -0.5)
    k = qk[:, ML_HEADS * ML_DQK:]
    v = u[:, ML_QK_W:ML_QK_W + ML_W]
    og = u[:, ML_QK_W + ML_W:]

    lane = lax.broadcasted_iota(jnp.int32, (chunk, LANES), 1)
    row = lax.broadcasted_iota(jnp.int32, (chunk, chunk), 0)
    col = lax.broadcasted_iota(jnp.int32, (chunk, chunk), 1)
    causal = row >= col
    log_f = -_softplus(-gates)

    for c in range(tile // chunk):
        rs = slice(c * chunk, (c + 1) * chunk)
        bcum = _dot_exact_lhs(tri_ref[...], log_f[rs])
        mcol = jnp.where(lane < ML_HEADS, gates[rs], bcum)
        mrow = mcol.T
        for h in range(ML_HEADS):
            i_col = mcol[:, h:h + 1]
            b_col = mcol[:, ML_HEADS + h:ML_HEADS + h + 1]
            i_row = mrow[h:h + 1, :]
            b_row = mrow[ML_HEADS + h:ML_HEADS + h + 1, :]
            m_prev = m_ref[h][0:1, 0:1]
            n_prev = n_ref[h][0:1, :]
            c_prev = c_ref[h]
            qh = q[rs, h * ML_DQK:(h + 1) * ML_DQK]
            kh = k[rs, h * ML_DQK:(h + 1) * ML_DQK]
            vh = v[rs, h * ML_DV:(h + 1) * ML_DV]

            dmat = jnp.where(causal, b_col - b_row + i_row, -jnp.inf)
            m_inter = b_col + m_prev
            m_t = jnp.maximum(m_inter, jnp.max(dmat, axis=1, keepdims=True))
            s = lax.dot_general(qh.astype(BF16), kh.astype(BF16), (((1,), (1,)), ((), ())),
                                preferred_element_type=F32)
            w_intra = jnp.exp(dmat - m_t) * s
            s_inter = jnp.exp(m_inter - m_t)
            num = s_inter * _dot(qh, c_prev) + _dot(w_intra, vh)
            den = (s_inter * jnp.sum(qh * n_prev, axis=1, keepdims=True)
                   + jnp.sum(w_intra, axis=1, keepdims=True))
            hh = num / jnp.maximum(jnp.abs(den), jnp.exp(-m_t))

            b_last = b_col[chunk - 1:chunk, :]
            g_col = b_last - b_col + i_col
            m_new = jnp.maximum(b_last + m_prev, jnp.max(g_col, axis=0, keepdims=True))
            carry = jnp.exp(b_last + m_prev - m_new)
            kw = kh * jnp.exp(g_col - m_new)
            c_ref[h] = carry * c_prev + lax.dot_general(
                kw.astype(BF16), vh.astype(BF16), (((0,), (0,)), ((), ())), preferred_element_type=F32)
            n_new = carry * n_prev + jnp.sum(kw, axis=0, keepdims=True)
            n_ref[h] = jnp.broadcast_to(n_new, (SUBLANES, ML_DQK))
            m_ref[h] = jnp.broadcast_to(m_new, (SUBLANES, LANES))

            mu = jnp.mean(hh, axis=1, keepdims=True)
            cen = hh - mu
            var = jnp.mean(cen * cen, axis=1, keepdims=True)
            y = cen * lax.rsqrt(var + LN_EPS) * normg_ref[:, h * ML_DV:(h + 1) * ML_DV]
            h_ref[rs, h * ML_DV:(h + 1) * ML_DV] = (
                _sigmoid(og[rs, h * ML_DV:(h + 1) * ML_DV]) * y).astype(BF16)


def _mlstm(xb, wml, wg, convw, convb, gb, normg, tile, chunk, after=()):
    bsz, s, d = xb.shape
    tri = (jnp.arange(chunk)[:, None] >= jnp.arange(chunk)[None, :]).astype(BF16)
    kern = _ordered_after(functools.partial(_mlstm_kernel, tile=tile, chunk=chunk), 8, len(after))
    return pl.pallas_call(
        kern,
        grid=(bsz, s // tile),
        in_specs=[pl.BlockSpec((None, tile, d), lambda b, j: (b, j, 0)),
                  _const_spec(wml.shape), _const_spec(wg.shape), _const_spec(convw.shape),
                  _const_spec(convb.shape), _const_spec(gb.shape), _const_spec(normg.shape),
                  _const_spec(tri.shape)] + _after_specs(after),
        out_specs=pl.BlockSpec((None, tile, ML_W), lambda b, j: (b, j, 0)),
        out_shape=jax.ShapeDtypeStruct((bsz, s, ML_W), BF16),
        scratch_shapes=[pltpu.VMEM((tile + SUBLANES, ML_QK_W), F32),
                        pltpu.VMEM((ML_HEADS, ML_DQK, ML_DV), F32),
                        pltpu.VMEM((ML_HEADS, SUBLANES, ML_DQK), F32),
                        pltpu.VMEM((ML_HEADS, SUBLANES, LANES), F32)],
        compiler_params=_params("parallel", "arbitrary"),
        name="mlstm",
    )(xb, wml, wg, convw, convb, gb, normg, tri, *after)


RW_PAD_IN = 3 * RW_W + 3 * LANES
SCAN_KK, SCAN_R, SCAN_W, SCAN_K, SCAN_V, SCAN_KKA = range(6)
SCAN_OPERANDS = 6


def _rwkv_prep_kernel(xb_ref, wrw_ref, mu_ref, w0_ref, wup_ref, a0_ref, aup_ref, gup_ref, kkw_ref,
                      ka_ref, rk_ref, red_ref, bc_ref,
                      scan_ref, g_ref, bonus_ref, ubuf, *, tile):
    @pl.when(pl.program_id(1) == 0)
    def _():
        ubuf[0:SUBLANES, :] = jnp.zeros((SUBLANES, RW_PAD_IN), F32)

    u = jnp.dot(xb_ref[...], wrw_ref[...], preferred_element_type=F32)
    ubuf[SUBLANES:SUBLANES + tile, :] = u
    u_prev = ubuf[pl.ds(SUBLANES - 1, tile), :]
    ubuf[0:SUBLANES, :] = ubuf[tile:tile + SUBLANES, :]
    us = u + (u_prev - u) * mu_ref[...]
    r = us[:, 0:RW_W]
    kr = us[:, RW_W:2 * RW_W]
    vr = us[:, 2 * RW_W:3 * RW_W]
    wd = us[:, 3 * RW_W:3 * RW_W + LANES]
    ad = us[:, 3 * RW_W + LANES:3 * RW_W + 2 * LANES]
    gd = us[:, 3 * RW_W + 2 * LANES:]

    w_log = -_softplus(-(w0_ref[...] + _dot(jnp.tanh(wd), wup_ref[...]))) - 0.5
    decay = jnp.exp(-jnp.exp(w_log))
    a = _sigmoid(a0_ref[...] + _dot(ad, aup_ref[...]))
    g = _dot(_sigmoid(gd), gup_ref[...])

    red, bc = red_ref[...], bc_ref[...]
    kk = kr * kkw_ref[...]
    kk = kk * lax.rsqrt(jnp.maximum(_head_sum(kk * kk, red, bc), 1e-24))
    k2 = kr * (1.0 + (a - 1.0) * ka_ref[...])
    bonus = _head_sum(r * k2 * rk_ref[...], red, bc) * vr

    scan_ref[SCAN_R] = r
    scan_ref[SCAN_W] = decay
    scan_ref[SCAN_K] = k2
    scan_ref[SCAN_V] = vr
    scan_ref[SCAN_KK] = kk
    scan_ref[SCAN_KKA] = kk * a
    g_ref[...] = g
    bonus_ref[...] = bonus


def _rwkv_prep(xb, wrw, mu, w0, wup, a0, aup, gup, kkw, ka, rk, red, bc, tile):
    bsz, s, d = xb.shape
    consts = (wrw, mu, w0, wup, a0, aup, gup, kkw, ka, rk, red, bc)
    spec = pl.BlockSpec((None, tile, RW_W), lambda b, j: (b, j, 0))
    tspec = pl.BlockSpec((SCAN_OPERANDS, tile, RW_W), lambda b, j: (0, j, b))
    return pl.pallas_call(
        functools.partial(_rwkv_prep_kernel, tile=tile),
        grid=(bsz, s // tile),
        in_specs=[pl.BlockSpec((None, tile, d), lambda b, j: (b, j, 0))] + [_const_spec(c.shape) for c in consts],
        out_specs=[tspec, spec, spec],
        out_shape=[jax.ShapeDtypeStruct((SCAN_OPERANDS, s, bsz * RW_W), F32)]
        + [jax.ShapeDtypeStruct((bsz, s, RW_W), F32)] * 2,
        scratch_shapes=[pltpu.VMEM((tile + SUBLANES, RW_PAD_IN), F32)],
        compiler_params=_params("parallel", "arbitrary"),
        name="rwkv_prep",
    )(xb, *consts)


assert LANES == 2 * RW_DH
SCAN_K_UNROLL = 16
SCAN_INNER = RW_DH // SCAN_K_UNROLL


def _transposed_pair(pair):
    return pair.reshape(2 * RW_DH, LANES).T


def _store_in_lanes(t2, dst0, dst1):
    top, bot = t2[:RW_DH], t2[RW_DH:]
    lo, hi = slice(0, RW_DH), slice(RW_DH, LANES)
    dst0[:, lo] = top[:, lo]
    dst1[:, hi] = bot[:, hi]
    dst0[:, hi] = pltpu.roll(bot, RW_DH, axis=1)[:, hi]
    dst1[:, lo] = pltpu.roll(top, RW_DH, axis=1)[:, lo]


def _rwkv_scan_kernel(x_ref, o_ref, st_ref, buf_a, buf_b, kk0_ref, tbuf, *, steps):
    @pl.when(pl.program_id(1) == 0)
    def _():
        st_ref[...] = jnp.zeros_like(st_ref)
        tbuf[...] = jnp.zeros_like(tbuf)

    npairs = steps // 2

    def transpose_in(a, q):
        start = jnp.where(a == SCAN_KK, jnp.minimum(2 * q + 1, steps - 2), 2 * q)
        tbuf[a] = _transposed_pair(x_ref[a, pl.ds(start, 2)])

    def relayout(dst, slot, q):
        done = jnp.clip(slot - 1, 0, SCAN_OPERANDS - 1)
        _store_in_lanes(tbuf[done], dst.at[done, 0], dst.at[done, 1])
        transpose_in(jnp.minimum(slot, SCAN_OPERANDS - 1), q)

    _store_in_lanes(_transposed_pair(x_ref[SCAN_KK, pl.ds(0, 2)]), kk0_ref.at[0], kk0_ref.at[1])
    for slot in range(SCAN_OPERANDS + 1):
        relayout(buf_a, slot, 0)

    zeros = jnp.zeros((RW_DH, LANES), F32)

    def first(kc, sa):
        return sa + st_ref[kc] * kk0_ref[0, pl.ds(kc, 1), :]

    sa0 = lax.fori_loop(0, RW_DH, first, zeros, unroll=8)

    def run_pair(q, cur, nxt, sa):
        qn = jnp.minimum(q + 1, npairs - 1)
        off = jnp.where(q == npairs - 1, 1, 0)
        for s in range(2):
            vt = cur[SCAN_V, s]
            ahead_i = jnp.minimum(s + off, 1)

            def body(j, carry, s=s, vt=vt, sa=sa, ahead_i=ahead_i):
                out, sa_next = carry
                relayout(nxt, s * SCAN_INNER + j, qn)
                for u in range(SCAN_K_UNROLL):
                    kc = j * SCAN_K_UNROLL + u
                    row = pl.ds(kc, 1)
                    new = (st_ref[kc] * cur[SCAN_W, s, row, :] - sa * cur[SCAN_KKA, s, row, :]
                           + vt * cur[SCAN_K, s, row, :])
                    st_ref[kc] = new
                    out = out + new * cur[SCAN_R, s, row, :]
                    sa_next = sa_next + new * cur[SCAN_KK, ahead_i, row, :]
                return out, sa_next

            out, sa = lax.fori_loop(0, SCAN_INNER, body, (zeros, zeros))
            o_ref[2 * q + s] = out
        return sa

    def two_pairs(h, sa):
        sa = run_pair(2 * h, buf_a, buf_b, sa)
        return run_pair(2 * h + 1, buf_b, buf_a, sa)

    lax.fori_loop(0, npairs // 2, two_pairs, sa0)


def _rwkv_scan(x, steps, after=()):
    _, s, rows, _ = x.shape
    assert steps % 4 == 0 and 2 * SCAN_INNER >= SCAN_OPERANDS + 1
    return pl.pallas_call(
        _ordered_after(functools.partial(_rwkv_scan_kernel, steps=steps), 1, len(after)),
        grid=(rows // RW_DH, s // steps),
        in_specs=[pl.BlockSpec((SCAN_OPERANDS, steps, RW_DH, LANES), lambda g, j: (0, j, g, 0))]
        + _after_specs(after),
        out_specs=pl.BlockSpec((steps, RW_DH, LANES), lambda g, j: (j, g, 0)),
        out_shape=jax.ShapeDtypeStruct((s, rows, LANES), F32),
        scratch_shapes=[pltpu.VMEM((RW_DH, RW_DH, LANES), F32),
                        pltpu.VMEM((SCAN_OPERANDS, 2, RW_DH, LANES), F32),
                        pltpu.VMEM((SCAN_OPERANDS, 2, RW_DH, LANES), F32),
                        pltpu.VMEM((2, RW_DH, LANES), F32),
                        pltpu.VMEM((SCAN_OPERANDS, 2 * RW_DH, LANES), F32)],
        compiler_params=_params("parallel", "arbitrary"),
        name="rwkv_scan",
    )(x, *after)


def _xattn_kernel(xb_ref, wq_ref, kt_ref, v_ref, o_ref):
    q = jnp.dot(xb_ref[...], wq_ref[...], preferred_element_type=F32)
    for h in range(CA_HEADS):
        hs = slice(h * CA_DH, (h + 1) * CA_DH)
        s = jnp.dot(q[:, hs].astype(BF16), kt_ref[hs, :], preferred_element_type=F32) * (CA_DH ** -0.5)
        p = jnp.exp(s - jnp.max(s, axis=1, keepdims=True))
        den = jnp.sum(p, axis=1, keepdims=True)
        o = jnp.dot(p.astype(BF16), v_ref[:, hs], preferred_element_type=F32) / den
        o_ref[:, hs] = o.astype(BF16)


def _xattn(xb, wq, kt, v, tile, after=()):
    bsz, s, d = xb.shape
    m = v.shape[1]
    return pl.pallas_call(
        _ordered_after(_xattn_kernel, 4, len(after)),
        grid=(bsz, s // tile),
        in_specs=[pl.BlockSpec((None, tile, d), lambda b, j: (b, j, 0)), _const_spec(wq.shape),
                  pl.BlockSpec((None, CA_W, m), lambda b, j: (b, 0, 0)),
                  pl.BlockSpec((None, m, CA_W), lambda b, j: (b, 0, 0))] + _after_specs(after),
        out_specs=pl.BlockSpec((None, tile, CA_W), lambda b, j: (b, j, 0)),
        out_shape=jax.ShapeDtypeStruct((bsz, s, CA_W), BF16),
        compiler_params=_params("parallel", "parallel"),
        name="xattn",
    )(xb, wq, kt, v, *after)


ROUTE_IDX, ROUTE_RANK, ROUTE_GATE = 0, TOP_K, 2 * TOP_K


def _merge_kernel(x_ref, xb_ref, hml_ref, o_ref, bonus_ref, g_ref, hca_ref,
                  wgate_ref, gateb_ref, red_ref, bc_ref, rwg_ref, rwb_ref, wml_ref, wrw_ref, wca_ref, wo_ref,
                  lng_ref, lnb_ref, rwhi_ref, rwlo_ref, rb_ref, tri_ref,
                  x1_ref, x1b_ref, route_ref, cnt_ref, carry_ref, *, tile, alpha, d_model):
    @pl.when(pl.program_id(0) == 0)
    def _():
        carry_ref[...] = jnp.zeros_like(carry_ref)

    red, bc = red_ref[...], bc_ref[...]
    o = o_ref[...]
    mu = _head_sum(o, red, bc) * (1.0 / RW_DH)
    cen = o - mu
    var = _head_sum(cen * cen, red, bc) * (1.0 / RW_DH)
    h_rw = (cen * lax.rsqrt(var + RW_GN_EPS) * rwg_ref[...] + rwb_ref[...] + bonus_ref[...]) * g_ref[...]

    xb = xb_ref[...]
    y = None
    for br, (h_br, w_ref) in enumerate(((hml_ref[...], wml_ref), (h_rw.astype(BF16), wrw_ref),
                                        (hca_ref[...], wca_ref))):
        cols = slice(br * d_model, (br + 1) * d_model)
        gate = _sigmoid(jnp.dot(xb, wgate_ref[:, cols], preferred_element_type=F32) + gateb_ref[:, cols])
        term = gate * jnp.dot(h_br, w_ref[...], preferred_element_type=F32)
        y = term if y is None else y + term
    mixed = _dot(y, wo_ref[...])
    x1 = _layer_norm(alpha * x_ref[...] + mixed, lng_ref[...], lnb_ref[...])
    x1_ref[...] = x1
    x1b_ref[...] = x1.astype(BF16)

    hi = x1.astype(BF16)
    lo = (x1 - hi.astype(F32)).astype(BF16)
    d = functools.partial(jnp.dot, preferred_element_type=F32)
    logits = d(hi, rwhi_ref[...]) + d(lo, rwhi_ref[...]) + d(hi, rwlo_ref[...]) + rb_ref[...]

    lane = lax.broadcasted_iota(jnp.int32, (tile, LANES), 1)
    vals = logits
    tops, onehots = [], []
    for _ in range(TOP_K):
        m = jnp.max(vals, axis=1, keepdims=True)
        idx = jnp.min(jnp.where(vals == m, lane, LANES), axis=1, keepdims=True)
        sel = lane == idx
        tops.append((m, idx))
        onehots.append(sel.astype(F32))
        vals = jnp.where(sel, -jnp.inf, vals)
    exps = [jnp.exp(m - tops[0][0]) for m, _ in tops]
    den = exps[0] + exps[1] + exps[2] + exps[3]

    cnt = onehots[0] + onehots[1] + onehots[2] + onehots[3]
    before = jnp.dot(tri_ref[...], cnt.astype(BF16), preferred_element_type=F32) + carry_ref[0:1, :]
    route = jnp.zeros((tile, LANES), F32)
    for kq in range(TOP_K):
        rank = jnp.sum(onehots[kq] * before, axis=1, keepdims=True)
        route = jnp.where(lane == ROUTE_IDX + kq, tops[kq][1].astype(F32), route)
        route = jnp.where(lane == ROUTE_RANK + kq, rank, route)
        route = jnp.where(lane == ROUTE_GATE + kq, exps[kq] / den, route)
    route_ref[...] = route
    total = carry_ref[0:1, :] + jnp.sum(cnt, axis=0, keepdims=True)
    carry_ref[...] = jnp.broadcast_to(total, carry_ref.shape)
    cnt_ref[...] = jnp.broadcast_to(total, cnt_ref.shape)


def _merge(x, xb, hml, o_tm, bonus, g, hca, consts, tile, alpha):
    n, d = x.shape
    tiles_per_seq = o_tm.shape[0] // tile
    tri = (jnp.arange(tile)[:, None] > jnp.arange(tile)[None, :]).astype(BF16)
    consts = tuple(consts) + (tri,)

    def rows(w):
        return pl.BlockSpec((tile, w), lambda i: (i, 0))

    o_spec = pl.BlockSpec((tile, RW_W), lambda i: (i % tiles_per_seq, i // tiles_per_seq))
    return pl.pallas_call(
        functools.partial(_merge_kernel, tile=tile, alpha=alpha, d_model=d),
        grid=(n // tile,),
        in_specs=[rows(d), rows(d), rows(ML_W), o_spec, rows(RW_W), rows(RW_W), rows(CA_W)]
        + [_const_spec(c.shape) for c in consts],
        out_specs=[rows(d), rows(d), rows(LANES), _const_spec((SUBLANES, LANES))],
        out_shape=[jax.ShapeDtypeStruct((n, d), F32), jax.ShapeDtypeStruct((n, d), BF16),
                   jax.ShapeDtypeStruct((n, LANES), F32), jax.ShapeDtypeStruct((SUBLANES, LANES), F32)],
        scratch_shapes=[pltpu.VMEM((SUBLANES, LANES), F32)],
        compiler_params=_params("arbitrary"),
        name="merge_route",
    )(x, xb, hml, o_tm, bonus, g, hca, *consts)


def _row_copy(src_ref, src_row, dst_ref, dst_row, sem):
    return pltpu.make_async_copy(src_ref.at[pl.ds(src_row, 1)], dst_ref.at[pl.ds(dst_row, 1)], sem)


def _dispatch_kernel(dest_ref, zero_blk_ref, x_ref, xd_ref, xbuf, zbuf, sems, zsem, *, tile, nsteps, block):
    i = pl.program_id(0)
    slot = i % 2

    @pl.when(i == 0)
    def _():
        zbuf[...] = jnp.zeros_like(zbuf)
        n_zero = zero_blk_ref.shape[0]

        def fresh(j):
            return jnp.logical_or(j == 0, zero_blk_ref[j] != zero_blk_ref[jnp.maximum(j - 1, 0)])

        def zero_copy(j):
            return pltpu.make_async_copy(zbuf, xd_ref.at[pl.ds(zero_blk_ref[j] * block, block)], zsem)

        def start(j, carry):
            @pl.when(fresh(j))
            def _():
                zero_copy(j).start()
            return carry

        def wait(j, carry):
            @pl.when(fresh(j))
            def _():
                zero_copy(j).wait()
            return carry

        lax.fori_loop(0, n_zero, start, 0)
        lax.fori_loop(0, n_zero, wait, 0)

    def retire(s):
        for _ in range(TOP_K):
            pltpu.make_async_copy(xbuf.at[s], xd_ref.at[pl.ds(0, tile)], sems.at[s]).wait()

    @pl.when(i >= 2)
    def _():
        retire(slot)

    xbuf[slot] = x_ref[...]

    def issue(r, carry):
        for kq in range(TOP_K):
            _row_copy(xbuf.at[slot], r, xd_ref, dest_ref[r * TOP_K + kq], sems.at[slot]).start()
        return carry

    lax.fori_loop(0, tile, issue, 0)

    @pl.when(i == nsteps - 1)
    def _():
        retire(slot)
        if nsteps > 1:
            retire(1 - slot)


def _dispatch(dest_flat, zero_blk, x1, n_rows, tile, block):
    n, d = x1.shape
    nsteps = n // tile
    return pl.pallas_call(
        functools.partial(_dispatch_kernel, tile=tile, nsteps=nsteps, block=block),
        grid=(nsteps,),
        in_specs=[pl.BlockSpec((tile * TOP_K,), lambda i: (i,), memory_space=pltpu.SMEM),
                  pl.BlockSpec(memory_space=pltpu.SMEM),
                  pl.BlockSpec((tile, d), lambda i: (i, 0))],
        out_specs=pl.BlockSpec(memory_space=pl.ANY),
        out_shape=jax.ShapeDtypeStruct((n_rows, d), F32),
        scratch_shapes=[pltpu.VMEM((2, tile, d), F32), pltpu.VMEM((block, d), F32),
                        pltpu.SemaphoreType.DMA((2,)), pltpu.SemaphoreType.DMA],
        compiler_params=_params("arbitrary"),
        name="moe_dispatch",
    )(dest_flat, zero_blk, x1)


DEINT_COLS = 512


def _deinterleave_kernel(w_ref, pe_ref, po_ref, glu_ref, lin_ref):
    wb = w_ref[...].astype(BF16)
    glu_ref[...] = jnp.dot(wb, pe_ref[...], preferred_element_type=F32).astype(BF16)
    lin_ref[...] = jnp.dot(wb, po_ref[...], preferred_element_type=F32).astype(BF16)


def _deinterleave(w_gu_all, layer):
    _, e, d, two_ff = w_gu_all.shape
    half = DEINT_COLS // 2
    src = jnp.arange(DEINT_COLS)[:, None]
    dst = jnp.arange(half)[None, :]
    pe = (src == 2 * dst).astype(BF16)
    po = (src == 2 * dst + 1).astype(BF16)
    out = jax.ShapeDtypeStruct((e, d, two_ff // 2), BF16)
    return pl.pallas_call(
        _deinterleave_kernel,
        grid=(e, two_ff // DEINT_COLS),
        in_specs=[pl.BlockSpec((None, None, d, DEINT_COLS), lambda i, c: (layer, i, 0, c)),
                  _const_spec(pe.shape), _const_spec(po.shape)],
        out_specs=[pl.BlockSpec((None, d, half), lambda i, c: (i, 0, c))] * 2,
        out_shape=[out, out],
        compiler_params=_params("parallel", "parallel"),
        name="deinterleave_w",
    )(w_gu_all, pe, po)


def _expert_kernel(blk_e_ref, nused_ref, xd_ref, wglu_ref, wlin_ref, bglu_ref, blin_ref, wdn_ref, bdn_ref, y_ref):
    del blk_e_ref
    live = pl.program_id(0) < nused_ref[0]

    @pl.when(live)
    def _():
        xb = xd_ref[...].astype(BF16)
        h_glu = jnp.dot(xb, wglu_ref[...], preferred_element_type=F32) + bglu_ref[...]
        h_lin = jnp.dot(xb, wlin_ref[...], preferred_element_type=F32) + blin_ref[...]
        x_glu = jnp.minimum(h_glu, SWIGLU_LIMIT)
        x_lin = jnp.clip(h_lin, -SWIGLU_LIMIT, SWIGLU_LIMIT)
        act = x_glu * _sigmoid(SWIGLU_ALPHA * x_glu) * (x_lin + 1.0)
        y_ref[...] = _dot(act, wdn_ref[...]) + bdn_ref[...]

    @pl.when(jnp.logical_not(live))
    def _():
        y_ref[...] = jnp.zeros_like(y_ref)


def _experts(blk_e, nused, xd, wglu, wlin, bglu, blin, wdn, bdn, block):
    n_rows, d = xd.shape
    dff = wglu.shape[2]
    grid_spec = pltpu.PrefetchScalarGridSpec(
        num_scalar_prefetch=2,
        grid=(n_rows // block,),
        in_specs=[pl.BlockSpec((block, d), lambda i, be, nu: (jnp.minimum(i, nu[0] - 1), 0)),
                  pl.BlockSpec((None, d, dff), lambda i, be, nu: (be[i], 0, 0)),
                  pl.BlockSpec((None, d, dff), lambda i, be, nu: (be[i], 0, 0)),
                  pl.BlockSpec((None, 1, dff), lambda i, be, nu: (be[i], 0, 0)),
                  pl.BlockSpec((None, 1, dff), lambda i, be, nu: (be[i], 0, 0)),
                  pl.BlockSpec((None, dff, d), lambda i, be, nu: (be[i], 0, 0)),
                  pl.BlockSpec((None, 1, d), lambda i, be, nu: (be[i], 0, 0))],
        out_specs=pl.BlockSpec((block, d), lambda i, be, nu: (i, 0)),
    )
    return pl.pallas_call(
        _expert_kernel,
        grid_spec=grid_spec,
        out_shape=jax.ShapeDtypeStruct((n_rows, d), F32),
        compiler_params=_params("arbitrary"),
        name="moe_experts",
    )(blk_e, nused, xd, wglu, wlin, bglu, blin, wdn, bdn)


def _combine_kernel(dest_ref, dest_next_ref, route_ref, x1_ref, y_ref, lng_ref, lnb_ref, x2_ref, x2b_ref,
                    ybuf, sems, *, tile, alpha, nsteps):
    i = pl.program_id(0)
    slot = i % 2

    def fetch(d_ref, s):
        def issue(r, carry):
            for kq in range(TOP_K):
                _row_copy(y_ref, d_ref[r * TOP_K + kq], ybuf.at[s, kq], r, sems.at[s]).start()
            return carry

        lax.fori_loop(0, tile, issue, 0, unroll=4)

    @pl.when(i == 0)
    def _():
        fetch(dest_ref, slot)

    @pl.when(i + 1 < nsteps)
    def _():
        fetch(dest_next_ref, 1 - slot)

    for kq in range(TOP_K):
        pltpu.make_async_copy(y_ref.at[pl.ds(0, tile)], ybuf.at[slot, kq], sems.at[slot]).wait()
    route = route_ref[...]
    moe = route[:, ROUTE_GATE:ROUTE_GATE + 1] * ybuf[slot, 0]
    for kq in range(1, TOP_K):
        moe = moe + route[:, ROUTE_GATE + kq:ROUTE_GATE + kq + 1] * ybuf[slot, kq]
    x2 = _layer_norm(alpha * x1_ref[...] + moe, lng_ref[...], lnb_ref[...])
    x2_ref[...] = x2
    x2b_ref[...] = x2.astype(BF16)


def _combine(dest_flat, route, x1, y_disp, lng, lnb, tile, alpha):
    n, d = x1.shape
    nsteps = n // tile
    return pl.pallas_call(
        functools.partial(_combine_kernel, tile=tile, alpha=alpha, nsteps=nsteps),
        grid=(nsteps,),
        in_specs=[pl.BlockSpec((tile * TOP_K,), lambda i: (i,), memory_space=pltpu.SMEM),
                  pl.BlockSpec((tile * TOP_K,), lambda i: (jnp.minimum(i + 1, nsteps - 1),),
                               memory_space=pltpu.SMEM),
                  pl.BlockSpec((tile, LANES), lambda i: (i, 0)),
                  pl.BlockSpec((tile, d), lambda i: (i, 0)),
                  pl.BlockSpec(memory_space=pl.ANY),
                  _const_spec((1, d)), _const_spec((1, d))],
        out_specs=[pl.BlockSpec((tile, d), lambda i: (i, 0)), pl.BlockSpec((tile, d), lambda i: (i, 0))],
        out_shape=[jax.ShapeDtypeStruct((n, d), F32), jax.ShapeDtypeStruct((n, d), BF16)],
        scratch_shapes=[pltpu.VMEM((2, TOP_K, tile, d), F32), pltpu.SemaphoreType.DMA((2,))],
        compiler_params=_params("arbitrary"),
        name="moe_combine",
    )(dest_flat, dest_flat, route, x1, y_disp, lng.reshape(1, d), lnb.reshape(1, d))


def _tiles(bsz, seq):
    n = bsz * seq
    return dict(
        ln=min(1024, n),
        mlstm=min(512, seq), mlstm_chunk=min(128, seq),
        rwkv_prep=min(512, seq),
        scan_steps=min(32, seq),
        xattn=min(512, seq),
        merge=min(512, seq),
        moe_rows=min(512, n),
        moe_block=512,
    )


def _pad_cols(w, width):
    return jnp.pad(w, ((0, 0), (0, width - w.shape[1])))


def kernel(x, mem, ln_in_g, ln_in_b, mem_ln_g, mem_ln_b, w_in, ml_conv_w, ml_conv_b, ml_ig_b, ml_fg_b, ml_norm_g, rw_mu, rw_w0, rw_w_up, rw_a0, rw_a_up, rw_g_up, rw_kk, rw_ka, rw_rk, rw_ln_g, rw_ln_b, ca_w_kv, gate_b, w_br_ml, w_br_rw, w_br_ca, w_o, ln1_g, ln1_b, router_w, router_b, w_gu, b_gu, w_dn, b_dn, ln2_g, ln2_b):
    bsz, seq, d = x.shape
    mem_len = mem.shape[1]
    depth = w_in.shape[0]
    n = bsz * seq
    t = _tiles(bsz, seq)
    alpha = (2 * depth) ** 0.25
    d_ff = w_dn.shape[2]

    xf, xb = _ln_rows(x.reshape(n, d), ln_in_g, ln_in_b, t["ln"])
    _, memb = _ln_rows(mem.reshape(bsz * mem_len, d), mem_ln_g, mem_ln_b, min(t["ln"], bsz * mem_len))

    o_qk, o_v, o_og = 0, ML_QK_W, ML_QK_W + ML_W
    o_ig = o_og + ML_W
    o_fg = o_ig + ML_HEADS
    o_rw = o_fg + ML_HEADS
    o_ca = o_rw + 3 * RW_W + RW_DECAY_LORA + RW_AAA_LORA + RW_GATE_LORA
    o_gate = o_ca + CA_W

    head_of_lane = jnp.arange(RW_W) // RW_DH
    red = (head_of_lane[:, None] == jnp.arange(LANES)[None, :]).astype(BF16)
    bc = red.T
    n_asg = n * TOP_K
    block = t["moe_block"]
    n_blocks = -(-n_asg // block) + N_EXPERTS
    n_rows = n_blocks * block

    for l in range(depth):
        w = w_in[l]
        o_wd = o_rw + 3 * RW_W
        o_ad = o_wd + RW_DECAY_LORA
        o_gd = o_ad + RW_AAA_LORA
        wrw = jnp.concatenate([w[:, o_rw:o_wd], _pad_cols(w[:, o_wd:o_ad], LANES),
                               _pad_cols(w[:, o_ad:o_gd], LANES), w[:, o_gd:o_ca]], axis=1).astype(BF16)
        mu = rw_mu[l]
        mu_p = jnp.concatenate([mu[:3 * RW_W], jnp.pad(mu[3 * RW_W:3 * RW_W + RW_DECAY_LORA], (0, LANES - RW_DECAY_LORA)),
                                jnp.pad(mu[3 * RW_W + RW_DECAY_LORA:3 * RW_W + RW_DECAY_LORA + RW_AAA_LORA],
                                        (0, LANES - RW_AAA_LORA)),
                                mu[3 * RW_W + RW_DECAY_LORA + RW_AAA_LORA:]])[None, :]
        wup = jnp.pad(rw_w_up[l], ((0, LANES - RW_DECAY_LORA), (0, 0))).astype(BF16)
        aup = jnp.pad(rw_a_up[l], ((0, LANES - RW_AAA_LORA), (0, 0))).astype(BF16)
        scan_in, g, bonus = _rwkv_prep(
            xb.reshape(bsz, seq, d), wrw, mu_p, rw_w0[l][None, :], wup, rw_a0[l][None, :], aup,
            rw_g_up[l].astype(BF16), rw_kk[l][None, :], rw_ka[l][None, :], rw_rk[l].reshape(1, RW_W), red, bc,
            t["rwkv_prep"])

        width = bsz * RW_W
        group = RW_DH * LANES
        width_pad = -(-width // group) * group
        ops = scan_in
        if width_pad != width:
            ops = jnp.pad(ops, ((0, 0), (0, 0), (0, width_pad - width)))
        ops = ops.reshape(SCAN_OPERANDS, seq, width_pad // LANES, LANES)

        wml = w[:, o_qk:o_ig].astype(BF16)
        wg = _pad_cols(w[:, o_ig:o_rw], LANES).astype(BF16)
        gb = _pad_cols(jnp.concatenate([ml_ig_b[l], ml_fg_b[l]])[None, :], LANES)
        h_ml = _mlstm(xb.reshape(bsz, seq, d), wml, wg, ml_conv_w[l], ml_conv_b[l][None, :], gb,
                      ml_norm_g[l][None, :], t["mlstm"], t["mlstm_chunk"], after=(g,))

        kv = _matmul(memb, ca_w_kv[l].astype(BF16), min(512, bsz * mem_len)).reshape(bsz, mem_len, 2 * CA_W)
        kt = kv[:, :, :CA_W].transpose(0, 2, 1).astype(BF16)
        vm = kv[:, :, CA_W:].astype(BF16)
        h_ca = _xattn(xb.reshape(bsz, seq, d), w[:, o_ca:o_gate].astype(BF16), kt, vm, t["xattn"], after=(g,))

        o_scan = _rwkv_scan(ops, t["scan_steps"])
        o_rwkv = (o_scan.reshape(seq, width_pad // group, RW_DH, 2, RW_DH).transpose(0, 1, 4, 3, 2)
                  .reshape(seq, width_pad)[:, :width])

        rw_pad = _pad_cols(router_w[l], LANES)
        rw_hi = rw_pad.astype(BF16)
        rw_lo = (rw_pad - rw_hi.astype(F32)).astype(BF16)
        rb = jnp.concatenate([router_b[l], jnp.full((LANES - N_EXPERTS,), NEG_BIG, F32)])[None, :]
        consts = (w[:, o_gate:].astype(BF16), gate_b[l][None, :], red, bc, rw_ln_g[l][None, :], rw_ln_b[l][None, :],
                  w_br_ml[l].astype(BF16), w_br_rw[l].astype(BF16), w_br_ca[l].astype(BF16), w_o[l].astype(BF16),
                  ln1_g[l][None, :], ln1_b[l][None, :], rw_hi, rw_lo, rb)
        x1, x1b, route, counts = _merge(xf, xb, h_ml.reshape(n, ML_W), o_rwkv, bonus.reshape(n, RW_W),
                                        g.reshape(n, RW_W), h_ca.reshape(n, CA_W), consts, t["merge"], alpha)
        del x1b

        cnt = counts[0, :N_EXPERTS].astype(jnp.int32)
        blocks_per = (cnt + block - 1) // block
        blk_end = jnp.cumsum(blocks_per)
        slot_start = (blk_end - blocks_per) * block
        e_idx = route[:, ROUTE_IDX:ROUTE_IDX + TOP_K].astype(jnp.int32)
        rank = route[:, ROUTE_RANK:ROUTE_RANK + TOP_K].astype(jnp.int32)
        dest = (slot_start[e_idx] + rank).reshape(n_asg)
        blk_ids = jnp.arange(n_blocks, dtype=jnp.int32)
        blk_e = jnp.minimum(jnp.sum((blk_ids[:, None] >= blk_end[None, :]).astype(jnp.int32), axis=1),
                            N_EXPERTS - 1)
        nused = blk_end[-1:].astype(jnp.int32)

        last_blk = jnp.maximum(blk_end - 1, 0).astype(jnp.int32)
        trailing = jnp.minimum(nused[0] + jnp.arange(N_EXPERTS, dtype=jnp.int32), n_blocks - 1)
        zero_blk = jnp.concatenate([last_blk, trailing])
        xd = _dispatch(dest, zero_blk, x1, n_rows, t["moe_rows"], block)
        w_glu, w_lin = _deinterleave(w_gu, l)
        y_disp = _experts(blk_e, nused, xd, w_glu, w_lin,
                          b_gu[l][:, None, 0::2], b_gu[l][:, None, 1::2], w_dn[l].astype(BF16),
                          b_dn[l][:, None, :], block)
        xf, xb = _combine(dest, route, x1, y_disp, ln2_g[l], ln2_b[l], t["moe_rows"], alpha)

    del d_ff
    return xf.reshape(bsz, seq, d)
```

```python
import functools

import jax
import jax.numpy as jnp
from jax import lax
from jax.experimental import pallas as pl
from jax.experimental.pallas import tpu as pltpu

ML_HEADS, ML_DQK, ML_DV, ML_CONV = 4, 64, 128, 4
ML_W = ML_HEADS * ML_DV
ML_QK_W = 2 * ML_HEADS * ML_DQK
RW_HEADS, RW_DH = 8, 64
RW_W = RW_HEADS * RW_DH
RW_DECAY_LORA, RW_AAA_LORA, RW_GATE_LORA = 64, 64, 128
RW_GN_EPS = 64e-5
CA_HEADS, CA_DH = 4, 128
CA_W = CA_HEADS * CA_DH
N_BRANCH = 3
N_EXPERTS, TOP_K = 32, 4
SWIGLU_LIMIT, SWIGLU_ALPHA = 7.0, 1.702
LN_EPS = 1e-5

LANES = 128
SUBLANES = 8
VMEM_LIMIT = 56 * 1024 * 1024

BF16 = jnp.bfloat16
F32 = jnp.float32
NEG_BIG = -1e30


def _dot(a, b):
    return jnp.dot(a.astype(BF16), b.astype(BF16), preferred_element_type=F32)


def _split3(a):
    hi = a.astype(BF16)
    r1 = a - hi.astype(F32)
    mid = r1.astype(BF16)
    lo = (r1 - mid.astype(F32)).astype(BF16)
    return hi, mid, lo


def _dot2_rhs(a, b_bf16):
    hi = a.astype(BF16)
    lo = (a - hi.astype(F32)).astype(BF16)
    d = functools.partial(jnp.dot, preferred_element_type=F32)
    return d(hi, b_bf16) + d(lo, b_bf16)


def _head_sum(x, red, bc):
    return _dot2_rhs(_dot2_rhs(x, red), bc)


def _dot_exact_lhs(a_bf16, b):
    hi, mid, lo = _split3(b)
    d = functools.partial(jnp.dot, preferred_element_type=F32)
    return d(a_bf16, hi) + d(a_bf16, mid) + d(a_bf16, lo)


def _sigmoid(x):
    return 1.0 / (1.0 + jnp.exp(-x))


def _softplus(x):
    return jnp.maximum(x, 0.0) + jnp.log1p(jnp.exp(-jnp.abs(x)))


def _layer_norm(v, g, b, eps=LN_EPS):
    mu = jnp.mean(v, axis=-1, keepdims=True)
    c = v - mu
    var = jnp.mean(c * c, axis=-1, keepdims=True)
    return c * lax.rsqrt(var + eps) * g + b


def _params(*sem):
    return pltpu.CompilerParams(dimension_semantics=sem, vmem_limit_bytes=VMEM_LIMIT)


def _ordered_after(kernel, first, count):
    def wrapped(*refs):
        return kernel(*refs[:first], *refs[first + count:])
    return wrapped


def _after_specs(after):
    return [pl.BlockSpec(memory_space=pl.ANY)] * len(after)


def _const_spec(shape):
    nd = len(shape)
    return pl.BlockSpec(shape, lambda *_: (0,) * nd, pipeline_mode=pl.Buffered(1))


def _ln_kernel(x_ref, g_ref, b_ref, o_ref, ob_ref):
    y = _layer_norm(x_ref[...], g_ref[...], b_ref[...])
    o_ref[...] = y
    ob_ref[...] = y.astype(BF16)


def _ln_rows(x2d, g, b, tile):
    n, d = x2d.shape
    return pl.pallas_call(
        _ln_kernel,
        grid=(n // tile,),
        in_specs=[pl.BlockSpec((tile, d), lambda i: (i, 0)), _const_spec((1, d)), _const_spec((1, d))],
        out_specs=[pl.BlockSpec((tile, d), lambda i: (i, 0)), pl.BlockSpec((tile, d), lambda i: (i, 0))],
        out_shape=[jax.ShapeDtypeStruct((n, d), F32), jax.ShapeDtypeStruct((n, d), BF16)],
        compiler_params=_params("parallel"),
        name="ln_rows",
    )(x2d, g.reshape(1, d), b.reshape(1, d))


def _mm_kernel(a_ref, b_ref, o_ref):
    o_ref[...] = jnp.dot(a_ref[...], b_ref[...], preferred_element_type=F32)


def _matmul(a, b, tile):
    m, k = a.shape
    n = b.shape[1]
    return pl.pallas_call(
        _mm_kernel,
        grid=(m // tile,),
        in_specs=[pl.BlockSpec((tile, k), lambda i: (i, 0)), _const_spec((k, n))],
        out_specs=pl.BlockSpec((tile, n), lambda i: (i, 0)),
        out_shape=jax.ShapeDtypeStruct((m, n), F32),
        compiler_params=_params("parallel"),
        name="matmul",
    )(a, b)


def _mlstm_kernel(xb_ref, wml_ref, wg_ref, convw_ref, convb_ref, gb_ref, normg_ref, tri_ref,
                  h_ref, ubuf, c_ref, n_ref, m_ref, *, tile, chunk):
    @pl.when(pl.program_id(1) == 0)
    def _():
        ubuf[0:SUBLANES, :] = jnp.zeros((SUBLANES, ML_QK_W), F32)
        c_ref[...] = jnp.zeros_like(c_ref)
        n_ref[...] = jnp.zeros_like(n_ref)
        m_ref[...] = jnp.zeros_like(m_ref)

    xb = xb_ref[...]
    u = jnp.dot(xb, wml_ref[...], preferred_element_type=F32)
    gates = jnp.dot(xb, wg_ref[...], preferred_element_type=F32) + gb_ref[...]

    ubuf[SUBLANES:SUBLANES + tile, :] = u[:, :ML_QK_W]
    acc = jnp.broadcast_to(convb_ref[...], (tile, ML_QK_W))
    for j in range(ML_CONV):
        acc = acc + convw_ref[j:j + 1, :] * ubuf[pl.ds(SUBLANES - ML_CONV + 1 + j, tile), :]
    ubuf[0:SUBLANES, :] = ubuf[tile:tile + SUBLANES, :]
    qk = acc * _sigmoid(acc)
    q = qk[:, :ML_HEADS * ML_DQK] * (ML_DQK ** -0.5)
    k = qk[:, ML_HEADS * ML_DQK:]
    v = u[:, ML_QK_W:ML_QK_W + ML_W]
    og = u[:, ML_QK_W + ML_W:]

    lane = lax.broadcasted_iota(jnp.int32, (chunk, LANES), 1)
    row = lax.broadcasted_iota(jnp.int32, (chunk, chunk), 0)
    col = lax.broadcasted_iota(jnp.int32, (chunk, chunk), 1)
    causal = row >= col
    log_f = -_softplus(-gates)

    for c in range(tile // chunk):
        rs = slice(c * chunk, (c + 1) * chunk)
        bcum = _dot_exact_lhs(tri_ref[...], log_f[rs])
        mcol = jnp.where(lane < ML_HEADS, gates[rs], bcum)
        mrow = mcol.T
        for h in range(ML_HEADS):
            i_col = mcol[:, h:h + 1]
            b_col = mcol[:, ML_HEADS + h:ML_HEADS + h + 1]
            i_row = mrow[h:h + 1, :]
            b_row = mrow[ML_HEADS + h:ML_HEADS + h + 1, :]
            m_prev = m_ref[h][0:1, 0:1]
            n_prev = n_ref[h][0:1, :]
            c_prev = c_ref[h]
            qh = q[rs, h * ML_DQK:(h + 1) * ML_DQK]
            kh = k[rs, h * ML_DQK:(h + 1) * ML_DQK]
            vh = v[rs, h * ML_DV:(h + 1) * ML_DV]

            dmat = jnp.where(causal, b_col - b_row + i_row, -jnp.inf)
            m_inter = b_col + m_prev
            m_t = jnp.maximum(m_inter, jnp.max(dmat, axis=1, keepdims=True))
            s = lax.dot_general(qh.astype(BF16), kh.astype(BF16), (((1,), (1,)), ((), ())),
                                preferred_element_type=F32)
            w_intra = jnp.exp(dmat - m_t) * s
            s_inter = jnp.exp(m_inter - m_t)
            num = s_inter * _dot(qh, c_prev) + _dot(w_intra, vh)
            den = (s_inter * jnp.sum(qh * n_prev, axis=1, keepdims=True)
                   + jnp.sum(w_intra, axis=1, keepdims=True))
            hh = num / jnp.maximum(jnp.abs(den), jnp.exp(-m_t))

            b_last = b_col[chunk - 1:chunk, :]
            g_col = b_last - b_col + i_col
            m_new = jnp.maximum(b_last + m_prev, jnp.max(g_col, axis=0, keepdims=True))
            carry = jnp.exp(b_last + m_prev - m_new)
            kw = kh * jnp.exp(g_col - m_new)
            c_ref[h] = carry * c_prev + lax.dot_general(
                kw.astype(BF16), vh.astype(BF16), (((0,), (0,)), ((), ())), preferred_element_type=F32)
            n_new = carry * n_prev + jnp.sum(kw, axis=0, keepdims=True)
            n_ref[h] = jnp.broadcast_to(n_new, (SUBLANES, ML_DQK))
            m_ref[h] = jnp.broadcast_to(m_new, (SUBLANES, LANES))

            mu = jnp.mean(hh, axis=1, keepdims=True)
            cen = hh - mu
            var = jnp.mean(cen * cen, axis=1, keepdims=True)
            y = cen * lax.rsqrt(var + LN_EPS) * normg_ref[:, h * ML_DV:(h + 1) * ML_DV]
            h_ref[rs, h * ML_DV:(h + 1) * ML_DV] = (
                _sigmoid(og[rs, h * ML_DV:(h + 1) * ML_DV]) * y).astype(BF16)


def _mlstm(xb, wml, wg, convw, convb, gb, normg, tile, chunk, after=()):
    bsz, s, d = xb.shape
    tri = (jnp.arange(chunk)[:, None] >= jnp.arange(chunk)[None, :]).astype(BF16)
    kern = _ordered_after(functools.partial(_mlstm_kernel, tile=tile, chunk=chunk), 8, len(after))
    return pl.pallas_call(
        kern,
        grid=(bsz, s // tile),
        in_specs=[pl.BlockSpec((None, tile, d), lambda b, j: (b, j, 0)),
                  _const_spec(wml.shape), _const_spec(wg.shape), _const_spec(convw.shape),
                  _const_spec(convb.shape), _const_spec(gb.shape), _const_spec(normg.shape),
                  _const_spec(tri.shape)] + _after_specs(after),
        out_specs=pl.BlockSpec((None, tile, ML_W), lambda b, j: (b, j, 0)),
        out_shape=jax.ShapeDtypeStruct((bsz, s, ML_W), BF16),
        scratch_shapes=[pltpu.VMEM((tile + SUBLANES, ML_QK_W), F32),
                        pltpu.VMEM((ML_HEADS, ML_DQK, ML_DV), F32),
                        pltpu.VMEM((ML_HEADS, SUBLANES, ML_DQK), F32),
                        pltpu.VMEM((ML_HEADS, SUBLANES, LANES), F32)],
        compiler_params=_params("parallel", "arbitrary"),
        name="mlstm",
    )(xb, wml, wg, convw, convb, gb, normg, tri, *after)


RW_PAD_IN = 3 * RW_W + 3 * LANES
SCAN_R, SCAN_W, SCAN_K, SCAN_V, SCAN_KK, SCAN_KKA = range(6)
SCAN_OPERANDS = 6


def _rwkv_prep_kernel(xb_ref, wrw_ref, mu_ref, w0_ref, wup_ref, a0_ref, aup_ref, gup_ref, kkw_ref,
                      ka_ref, rk_ref, red_ref, bc_ref,
                      scan_ref, g_ref, bonus_ref, ubuf, *, tile):
    @pl.when(pl.program_id(1) == 0)
    def _():
        ubuf[0:SUBLANES, :] = jnp.zeros((SUBLANES, RW_PAD_IN), F32)

    u = jnp.dot(xb_ref[...], wrw_ref[...], preferred_element_type=F32)
    ubuf[SUBLANES:SUBLANES + tile, :] = u
    u_prev = ubuf[pl.ds(SUBLANES - 1, tile), :]
    ubuf[0:SUBLANES, :] = ubuf[tile:tile + SUBLANES, :]
    us = u + (u_prev - u) * mu_ref[...]
    r = us[:, 0:RW_W]
    kr = us[:, RW_W:2 * RW_W]
    vr = us[:, 2 * RW_W:3 * RW_W]
    wd = us[:, 3 * RW_W:3 * RW_W + LANES]
    ad = us[:, 3 * RW_W + LANES:3 * RW_W + 2 * LANES]
    gd = us[:, 3 * RW_W + 2 * LANES:]

    w_log = -_softplus(-(w0_ref[...] + _dot(jnp.tanh(wd), wup_ref[...]))) - 0.5
    decay = jnp.exp(-jnp.exp(w_log))
    a = _sigmoid(a0_ref[...] + _dot(ad, aup_ref[...]))
    g = _dot(_sigmoid(gd), gup_ref[...])

    red, bc = red_ref[...], bc_ref[...]
    kk = kr * kkw_ref[...]
    kk = kk * lax.rsqrt(jnp.maximum(_head_sum(kk * kk, red, bc), 1e-24))
    k2 = kr * (1.0 + (a - 1.0) * ka_ref[...])
    bonus = _head_sum(r * k2 * rk_ref[...], red, bc) * vr

    scan_ref[SCAN_R] = r
    scan_ref[SCAN_W] = decay
    scan_ref[SCAN_K] = k2
    scan_ref[SCAN_V] = vr
    scan_ref[SCAN_KK] = kk
    scan_ref[SCAN_KKA] = kk * a
    g_ref[...] = g
    bonus_ref[...] = bonus


def _rwkv_prep(xb, wrw, mu, w0, wup, a0, aup, gup, kkw, ka, rk, red, bc, tile):
    bsz, s, d = xb.shape
    consts = (wrw, mu, w0, wup, a0, aup, gup, kkw, ka, rk, red, bc)
    spec = pl.BlockSpec((None, tile, RW_W), lambda b, j: (b, j, 0))
    tspec = pl.BlockSpec((SCAN_OPERANDS, tile, RW_W), lambda b, j: (0, j, b))
    return pl.pallas_call(
        functools.partial(_rwkv_prep_kernel, tile=tile),
        grid=(bsz, s // tile),
        in_specs=[pl.BlockSpec((None, tile, d), lambda b, j: (b, j, 0))] + [_const_spec(c.shape) for c in consts],
        out_specs=[tspec, spec, spec],
        out_shape=[jax.ShapeDtypeStruct((SCAN_OPERANDS, s, bsz * RW_W), F32)]
        + [jax.ShapeDtypeStruct((bsz, s, RW_W), F32)] * 2,
        scratch_shapes=[pltpu.VMEM((tile + SUBLANES, RW_PAD_IN), F32)],
        compiler_params=_params("parallel", "arbitrary"),
        name="rwkv_prep",
    )(xb, *consts)


def _rwkv_scan_kernel(x_ref, o_ref, st_ref, *, steps):
    @pl.when(pl.program_id(1) == 0)
    def _():
        st_ref[...] = jnp.zeros_like(st_ref)

    r_ref, w_ref, k_ref, v_ref, kk_ref, kka_ref = (
        x_ref.at[i] for i in (SCAN_R, SCAN_W, SCAN_K, SCAN_V, SCAN_KK, SCAN_KKA))
    zeros = jnp.zeros((RW_DH, LANES), F32)

    def first(kc, sa):
        return sa + st_ref[kc] * kk_ref[0, pl.ds(kc, 1), :]

    sa0 = lax.fori_loop(0, RW_DH, first, zeros, unroll=8)

    def step(t, sa):
        vt = v_ref[t]
        tn = jnp.minimum(t + 1, steps - 1)

        def body(kc, carry):
            out, sa_next = carry
            row = pl.ds(kc, 1)
            new = st_ref[kc] * w_ref[t, row, :] - sa * kka_ref[t, row, :] + vt * k_ref[t, row, :]
            st_ref[kc] = new
            return out + new * r_ref[t, row, :], sa_next + new * kk_ref[tn, row, :]

        out, sa_next = lax.fori_loop(0, RW_DH, body, (zeros, zeros), unroll=16)
        o_ref[t] = out
        return sa_next

    lax.fori_loop(0, steps, step, sa0)


def _rwkv_scan(x, steps, after=()):
    _, s, dh, nl = x.shape
    spec = pl.BlockSpec((steps, dh, LANES), lambda g, j: (j, 0, g))
    return pl.pallas_call(
        _ordered_after(functools.partial(_rwkv_scan_kernel, steps=steps), 1, len(after)),
        grid=(nl // LANES, s // steps),
        in_specs=[pl.BlockSpec((SCAN_OPERANDS, steps, dh, LANES), lambda g, j: (0, j, 0, g))]
        + _after_specs(after),
        out_specs=spec,
        out_shape=jax.ShapeDtypeStruct((s, dh, nl), F32),
        scratch_shapes=[pltpu.VMEM((dh, dh, LANES), F32)],
        compiler_params=_params("parallel", "arbitrary"),
        name="rwkv_scan",
    )(x, *after)


def _xattn_kernel(xb_ref, wq_ref, kt_ref, v_ref, o_ref):
    q = jnp.dot(xb_ref[...], wq_ref[...], preferred_element_type=F32)
    for h in range(CA_HEADS):
        hs = slice(h * CA_DH, (h + 1) * CA_DH)
        s = jnp.dot(q[:, hs].astype(BF16), kt_ref[hs, :], preferred_element_type=F32) * (CA_DH ** -0.5)
        p = jnp.exp(s - jnp.max(s, axis=1, keepdims=True))
        den = jnp.sum(p, axis=1, keepdims=True)
        o = jnp.dot(p.astype(BF16), v_ref[:, hs], preferred_element_type=F32) / den
        o_ref[:, hs] = o.astype(BF16)


def _xattn(xb, wq, kt, v, tile, after=()):
    bsz, s, d = xb.shape
    m = v.shape[1]
    return pl.pallas_call(
        _ordered_after(_xattn_kernel, 4, len(after)),
        grid=(bsz, s // tile),
        in_specs=[pl.BlockSpec((None, tile, d), lambda b, j: (b, j, 0)), _const_spec(wq.shape),
                  pl.BlockSpec((None, CA_W, m), lambda b, j: (b, 0, 0)),
                  pl.BlockSpec((None, m, CA_W), lambda b, j: (b, 0, 0))] + _after_specs(after),
        out_specs=pl.BlockSpec((None, tile, CA_W), lambda b, j: (b, j, 0)),
        out_shape=jax.ShapeDtypeStruct((bsz, s, CA_W), BF16),
        compiler_params=_params("parallel", "parallel"),
        name="xattn",
    )(xb, wq, kt, v, *after)


ROUTE_IDX, ROUTE_RANK, ROUTE_GATE = 0, TOP_K, 2 * TOP_K


def _merge_kernel(x_ref, xb_ref, hml_ref, o_ref, bonus_ref, g_ref, hca_ref,
                  wgate_ref, gateb_ref, red_ref, bc_ref, rwg_ref, rwb_ref, wml_ref, wrw_ref, wca_ref, wo_ref,
                  lng_ref, lnb_ref, rwhi_ref, rwlo_ref, rb_ref, tri_ref,
                  x1_ref, x1b_ref, route_ref, cnt_ref, *, tile, group, alpha, d_model):
    red, bc = red_ref[...], bc_ref[...]
    o = o_ref[...]
    mu = _head_sum(o, red, bc) * (1.0 / RW_DH)
    cen = o - mu
    var = _head_sum(cen * cen, red, bc) * (1.0 / RW_DH)
    h_rw = (cen * lax.rsqrt(var + RW_GN_EPS) * rwg_ref[...] + rwb_ref[...] + bonus_ref[...]) * g_ref[...]

    xb = xb_ref[...]
    y = None
    for br, (h_br, w_ref) in enumerate(((hml_ref[...], wml_ref), (h_rw.astype(BF16), wrw_ref),
                                        (hca_ref[...], wca_ref))):
        cols = slice(br * d_model, (br + 1) * d_model)
        gate = _sigmoid(jnp.dot(xb, wgate_ref[:, cols], preferred_element_type=F32) + gateb_ref[:, cols])
        term = gate * jnp.dot(h_br, w_ref[...], preferred_element_type=F32)
        y = term if y is None else y + term
    mixed = _dot(y, wo_ref[...])
    x1 = _layer_norm(alpha * x_ref[...] + mixed, lng_ref[...], lnb_ref[...])
    x1_ref[...] = x1
    x1b_ref[...] = x1.astype(BF16)

    hi = x1.astype(BF16)
    lo = (x1 - hi.astype(F32)).astype(BF16)
    d = functools.partial(jnp.dot, preferred_element_type=F32)
    logits = d(hi, rwhi_ref[...]) + d(lo, rwhi_ref[...]) + d(hi, rwlo_ref[...]) + rb_ref[...]

    lane = lax.broadcasted_iota(jnp.int32, (tile, LANES), 1)
    vals = logits
    tops, onehots = [], []
    for _ in range(TOP_K):
        m = jnp.max(vals, axis=1, keepdims=True)
        idx = jnp.min(jnp.where(vals == m, lane, LANES), axis=1, keepdims=True)
        sel = lane == idx
        tops.append((m, idx))
        onehots.append(sel.astype(F32))
        vals = jnp.where(sel, -jnp.inf, vals)
    exps = [jnp.exp(m - tops[0][0]) for m, _ in tops]
    den = exps[0] + exps[1] + exps[2] + exps[3]

    cnt = onehots[0] + onehots[1] + onehots[2] + onehots[3]
    befores = []
    for gi in range(tile // group):
        cg = cnt[gi * group:(gi + 1) * group]
        befores.append(jnp.dot(tri_ref[...], cg.astype(BF16), preferred_element_type=F32))
        cnt_ref[gi] = jnp.broadcast_to(jnp.sum(cg, axis=0, keepdims=True), (SUBLANES, LANES))
    before = befores[0] if len(befores) == 1 else jnp.concatenate(befores, axis=0)
    route = jnp.zeros((tile, LANES), F32)
    for kq in range(TOP_K):
        rank = jnp.sum(onehots[kq] * before, axis=1, keepdims=True)
        route = jnp.where(lane == ROUTE_IDX + kq, tops[kq][1].astype(F32), route)
        route = jnp.where(lane == ROUTE_RANK + kq, rank, route)
        route = jnp.where(lane == ROUTE_GATE + kq, exps[kq] / den, route)
    route_ref[...] = route


def _merge(x, xb, hml, o_tm, bonus, g, hca, consts, tile, group, alpha):
    n, d = x.shape
    tiles_per_seq = o_tm.shape[0] // tile
    gpt = tile // group
    tri = (jnp.arange(group)[:, None] > jnp.arange(group)[None, :]).astype(BF16)
    consts = tuple(consts) + (tri,)

    def rows(w):
        return pl.BlockSpec((tile, w), lambda i: (i, 0))

    o_spec = pl.BlockSpec((tile, RW_W), lambda i: (i % tiles_per_seq, i // tiles_per_seq))
    return pl.pallas_call(
        functools.partial(_merge_kernel, tile=tile, group=group, alpha=alpha, d_model=d),
        grid=(n // tile,),
        in_specs=[rows(d), rows(d), rows(ML_W), o_spec, rows(RW_W), rows(RW_W), rows(CA_W)]
        + [_const_spec(c.shape) for c in consts],
        out_specs=[rows(d), rows(d), rows(LANES),
                   pl.BlockSpec((gpt, SUBLANES, LANES), lambda i: (i, 0, 0))],
        out_shape=[jax.ShapeDtypeStruct((n, d), F32), jax.ShapeDtypeStruct((n, d), BF16),
                   jax.ShapeDtypeStruct((n, LANES), F32),
                   jax.ShapeDtypeStruct((n // group, SUBLANES, LANES), F32)],
        compiler_params=_params("parallel"),
        name="merge_route",
    )(x, xb, hml, o_tm, bonus, g, hca, *consts)


MOE_CHUNK = SUBLANES
CHUNK_TABLE = 1024


def _group_rows(group):
    return -(-(group * TOP_K + N_EXPERTS * (MOE_CHUNK - 1)) // LANES) * LANES


def _dispatch_kernel(tab_ref, zero_blk_ref, post_ref, xb_ref, xd_ref, xs_buf, zbuf, live_ref, sems, zsem,
                     *, group, nsteps, block, prows):
    i = pl.program_id(0)
    slot = i % 2

    @pl.when(i == 0)
    def _():
        zbuf[...] = jnp.zeros_like(zbuf)
        n_zero = zero_blk_ref.shape[0]

        def fresh(j):
            return jnp.logical_or(j == 0, zero_blk_ref[j] != zero_blk_ref[jnp.maximum(j - 1, 0)])

        def zero_copy(j):
            return pltpu.make_async_copy(zbuf, xd_ref.at[pl.ds(zero_blk_ref[j] * block, block)], zsem)

        def start(j, carry):
            @pl.when(fresh(j))
            def _():
                zero_copy(j).start()
            return carry

        def wait(j, carry):
            @pl.when(fresh(j))
            def _():
                zero_copy(j).wait()
            return carry

        lax.fori_loop(0, n_zero, start, 0)
        lax.fori_loop(0, n_zero, wait, 0)

    def chunk_copy(c, s):
        return pltpu.make_async_copy(xs_buf.at[s, pl.ds(c * MOE_CHUNK, MOE_CHUNK)],
                                     xd_ref.at[pl.ds(pl.multiple_of(tab_ref[c], MOE_CHUNK), MOE_CHUNK)],
                                     sems.at[s])

    def retire(s):
        def wait(c, carry):
            chunk_copy(0, s).wait()
            return carry

        lax.fori_loop(0, live_ref[s], wait, 0)

    @pl.when(i >= 2)
    def _():
        retire(slot)

    pos_t = post_ref[...]
    rows = lax.broadcasted_iota(jnp.int32, (prows, group), 0)
    pick = jnp.zeros((prows, group), F32)
    for kq in range(TOP_K):
        pick = pick + jnp.where(rows == pos_t[kq:kq + 1, :], 1.0, 0.0)
    xs_buf[slot] = jnp.dot(pick.astype(BF16), xb_ref[...], preferred_element_type=F32)

    live = tab_ref[CHUNK_TABLE - 1]
    live_ref[slot] = live

    def issue(c, carry):
        chunk_copy(c, slot).start()
        return carry

    lax.fori_loop(0, live, issue, 0)

    @pl.when(i == nsteps - 1)
    def _():
        retire(slot)
        if nsteps > 1:
            retire(1 - slot)


def _dispatch(table, zero_blk, pos_t, x1b, n_rows, group, block):
    n, d = x1b.shape
    nsteps = n // group
    prows = _group_rows(group)
    return pl.pallas_call(
        functools.partial(_dispatch_kernel, group=group, nsteps=nsteps, block=block, prows=prows),
        grid=(nsteps,),
        in_specs=[pl.BlockSpec((CHUNK_TABLE,), lambda i: (i,), memory_space=pltpu.SMEM),
                  pl.BlockSpec(memory_space=pltpu.SMEM),
                  pl.BlockSpec((None, SUBLANES, group), lambda i: (i, 0, 0)),
                  pl.BlockSpec((group, d), lambda i: (i, 0))],
        out_specs=pl.BlockSpec(memory_space=pl.ANY),
        out_shape=jax.ShapeDtypeStruct((n_rows, d), F32),
        scratch_shapes=[pltpu.VMEM((2, prows, d), F32), pltpu.VMEM((block, d), F32),
                        pltpu.SMEM((2,), jnp.int32),
                        pltpu.SemaphoreType.DMA((2,)), pltpu.SemaphoreType.DMA],
        compiler_params=_params("arbitrary"),
        name="moe_dispatch",
    )(table, zero_blk, pos_t, x1b)


DEINT_COLS = 512


def _deinterleave_kernel(w_ref, pe_ref, po_ref, glu_ref, lin_ref):
    wb = w_ref[...].astype(BF16)
    glu_ref[...] = jnp.dot(wb, pe_ref[...], preferred_element_type=F32).astype(BF16)
    lin_ref[...] = jnp.dot(wb, po_ref[...], preferred_element_type=F32).astype(BF16)


def _deinterleave(w_gu_all, layer):
    _, e, d, two_ff = w_gu_all.shape
    half = DEINT_COLS // 2
    src = jnp.arange(DEINT_COLS)[:, None]
    dst = jnp.arange(half)[None, :]
    pe = (src == 2 * dst).astype(BF16)
    po = (src == 2 * dst + 1).astype(BF16)
    out = jax.ShapeDtypeStruct((e, d, two_ff // 2), BF16)
    return pl.pallas_call(
        _deinterleave_kernel,
        grid=(e, two_ff // DEINT_COLS),
        in_specs=[pl.BlockSpec((None, None, d, DEINT_COLS), lambda i, c: (layer, i, 0, c)),
                  _const_spec(pe.shape), _const_spec(po.shape)],
        out_specs=[pl.BlockSpec((None, d, half), lambda i, c: (i, 0, c))] * 2,
        out_shape=[out, out],
        compiler_params=_params("parallel", "parallel"),
        name="deinterleave_w",
    )(w_gu_all, pe, po)


def _expert_kernel(blk_e_ref, nused_ref, xd_ref, wglu_ref, wlin_ref, bglu_ref, blin_ref, wdn_ref, bdn_ref, y_ref):
    del blk_e_ref
    live = pl.program_id(0) < nused_ref[0]

    @pl.when(live)
    def _():
        xb = xd_ref[...].astype(BF16)
        h_glu = jnp.dot(xb, wglu_ref[...], preferred_element_type=F32) + bglu_ref[...]
        h_lin = jnp.dot(xb, wlin_ref[...], preferred_element_type=F32) + blin_ref[...]
        x_glu = jnp.minimum(h_glu, SWIGLU_LIMIT)
        x_lin = jnp.clip(h_lin, -SWIGLU_LIMIT, SWIGLU_LIMIT)
        act = x_glu * _sigmoid(SWIGLU_ALPHA * x_glu) * (x_lin + 1.0)
        y_ref[...] = _dot(act, wdn_ref[...]) + bdn_ref[...]

    @pl.when(jnp.logical_not(live))
    def _():
        y_ref[...] = jnp.zeros_like(y_ref)


def _experts(blk_e, nused, xd, wglu, wlin, bglu, blin, wdn, bdn, block):
    n_rows, d = xd.shape
    dff = wglu.shape[2]
    grid_spec = pltpu.PrefetchScalarGridSpec(
        num_scalar_prefetch=2,
        grid=(n_rows // block,),
        in_specs=[pl.BlockSpec((block, d), lambda i, be, nu: (jnp.minimum(i, nu[0] - 1), 0)),
                  pl.BlockSpec((None, d, dff), lambda i, be, nu: (be[i], 0, 0)),
                  pl.BlockSpec((None, d, dff), lambda i, be, nu: (be[i], 0, 0)),
                  pl.BlockSpec((None, 1, dff), lambda i, be, nu: (be[i], 0, 0)),
                  pl.BlockSpec((None, 1, dff), lambda i, be, nu: (be[i], 0, 0)),
                  pl.BlockSpec((None, dff, d), lambda i, be, nu: (be[i], 0, 0)),
                  pl.BlockSpec((None, 1, d), lambda i, be, nu: (be[i], 0, 0))],
        out_specs=pl.BlockSpec((block, d), lambda i, be, nu: (i, 0)),
    )
    return pl.pallas_call(
        _expert_kernel,
        grid_spec=grid_spec,
        out_shape=jax.ShapeDtypeStruct((n_rows, d), F32),
        compiler_params=_params("arbitrary"),
        name="moe_experts",
    )(blk_e, nused, xd, wglu, wlin, bglu, blin, wdn, bdn)


def _combine_kernel(tab_ref, tab_next_ref, pos_ref, route_ref, x1_ref, y_ref, lng_ref, lnb_ref, x2_ref, x2b_ref,
                    ybuf, sems, *, group, alpha, nsteps, prows):
    i = pl.program_id(0)
    slot = i % 2
    nchunk = prows // MOE_CHUNK

    def fetch(t_ref, s):
        def issue(c, carry):
            pltpu.make_async_copy(y_ref.at[pl.ds(pl.multiple_of(t_ref[c], MOE_CHUNK), MOE_CHUNK)],
                                  ybuf.at[s, pl.ds(c * MOE_CHUNK, MOE_CHUNK)], sems.at[s]).start()
            return carry

        lax.fori_loop(0, nchunk, issue, 0, unroll=4)

    @pl.when(i == 0)
    def _():
        fetch(tab_ref, slot)

    @pl.when(i + 1 < nsteps)
    def _():
        fetch(tab_next_ref, 1 - slot)

    pltpu.make_async_copy(y_ref.at[pl.ds(0, prows)], ybuf.at[slot], sems.at[slot]).wait()
    route = route_ref[...]
    pos = pos_ref[...]
    cols = lax.broadcasted_iota(jnp.int32, (group, prows), 1)
    weights = jnp.zeros((group, prows), F32)
    for kq in range(TOP_K):
        weights = weights + jnp.where(cols == pos[:, kq:kq + 1], route[:, ROUTE_GATE + kq:ROUTE_GATE + kq + 1], 0.0)
    moe = jnp.dot(weights.astype(BF16), ybuf[slot].astype(BF16), preferred_element_type=F32)
    x2 = _layer_norm(alpha * x1_ref[...] + moe, lng_ref[...], lnb_ref[...])
    x2_ref[...] = x2
    x2b_ref[...] = x2.astype(BF16)


def _combine(table, pos, route, x1, y_disp, lng, lnb, group, alpha):
    n, d = x1.shape
    nsteps = n // group
    prows = _group_rows(group)
    return pl.pallas_call(
        functools.partial(_combine_kernel, group=group, alpha=alpha, nsteps=nsteps, prows=prows),
        grid=(nsteps,),
        in_specs=[pl.BlockSpec((CHUNK_TABLE,), lambda i: (i,), memory_space=pltpu.SMEM),
                  pl.BlockSpec((CHUNK_TABLE,), lambda i: (jnp.minimum(i + 1, nsteps - 1),),
                               memory_space=pltpu.SMEM),
                  pl.BlockSpec((group, SUBLANES), lambda i: (i, 0)),
                  pl.BlockSpec((group, LANES), lambda i: (i, 0)),
                  pl.BlockSpec((group, d), lambda i: (i, 0)),
                  pl.BlockSpec(memory_space=pl.ANY),
                  _const_spec((1, d)), _const_spec((1, d))],
        out_specs=[pl.BlockSpec((group, d), lambda i: (i, 0)), pl.BlockSpec((group, d), lambda i: (i, 0))],
        out_shape=[jax.ShapeDtypeStruct((n, d), F32), jax.ShapeDtypeStruct((n, d), BF16)],
        scratch_shapes=[pltpu.VMEM((2, prows, d), F32), pltpu.SemaphoreType.DMA((2,))],
        compiler_params=_params("arbitrary"),
        name="moe_combine",
    )(table, table, pos, route, x1, y_disp, lng.reshape(1, d), lnb.reshape(1, d))


def _tiles(bsz, seq):
    n = bsz * seq
    return dict(
        ln=min(1024, n),
        mlstm=min(512, seq), mlstm_chunk=min(128, seq),
        rwkv_prep=min(512, seq),
        scan_steps=min(32, seq),
        xattn=min(512, seq),
        merge=min(512, seq),
        moe_group=min(256, seq),
        moe_block=512,
    )


def _pad_cols(w, width):
    return jnp.pad(w, ((0, 0), (0, width - w.shape[1])))


def kernel(x, mem, ln_in_g, ln_in_b, mem_ln_g, mem_ln_b, w_in, ml_conv_w, ml_conv_b, ml_ig_b, ml_fg_b, ml_norm_g, rw_mu, rw_w0, rw_w_up, rw_a0, rw_a_up, rw_g_up, rw_kk, rw_ka, rw_rk, rw_ln_g, rw_ln_b, ca_w_kv, gate_b, w_br_ml, w_br_rw, w_br_ca, w_o, ln1_g, ln1_b, router_w, router_b, w_gu, b_gu, w_dn, b_dn, ln2_g, ln2_b):
    bsz, seq, d = x.shape
    mem_len = mem.shape[1]
    depth = w_in.shape[0]
    n = bsz * seq
    t = _tiles(bsz, seq)
    alpha = (2 * depth) ** 0.25
    d_ff = w_dn.shape[2]

    xf, xb = _ln_rows(x.reshape(n, d), ln_in_g, ln_in_b, t["ln"])
    _, memb = _ln_rows(mem.reshape(bsz * mem_len, d), mem_ln_g, mem_ln_b, min(t["ln"], bsz * mem_len))

    o_qk, o_v, o_og = 0, ML_QK_W, ML_QK_W + ML_W
    o_ig = o_og + ML_W
    o_fg = o_ig + ML_HEADS
    o_rw = o_fg + ML_HEADS
    o_ca = o_rw + 3 * RW_W + RW_DECAY_LORA + RW_AAA_LORA + RW_GATE_LORA
    o_gate = o_ca + CA_W

    head_of_lane = jnp.arange(RW_W) // RW_DH
    red = (head_of_lane[:, None] == jnp.arange(LANES)[None, :]).astype(BF16)
    bc = red.T
    n_asg = n * TOP_K
    block = t["moe_block"]
    group = t["moe_group"]
    n_groups = n // group
    n_chunk = _group_rows(group) // MOE_CHUNK
    assert n_chunk < CHUNK_TABLE and block % MOE_CHUNK == 0
    n_blocks = -(-(n_asg + n_groups * N_EXPERTS * (MOE_CHUNK - 1)) // block) + N_EXPERTS
    n_rows = n_blocks * block

    for l in range(depth):
        w = w_in[l]
        o_wd = o_rw + 3 * RW_W
        o_ad = o_wd + RW_DECAY_LORA
        o_gd = o_ad + RW_AAA_LORA
        wrw = jnp.concatenate([w[:, o_rw:o_wd], _pad_cols(w[:, o_wd:o_ad], LANES),
                               _pad_cols(w[:, o_ad:o_gd], LANES), w[:, o_gd:o_ca]], axis=1).astype(BF16)
        mu = rw_mu[l]
        mu_p = jnp.concatenate([mu[:3 * RW_W], jnp.pad(mu[3 * RW_W:3 * RW_W + RW_DECAY_LORA], (0, LANES - RW_DECAY_LORA)),
                                jnp.pad(mu[3 * RW_W + RW_DECAY_LORA:3 * RW_W + RW_DECAY_LORA + RW_AAA_LORA],
                                        (0, LANES - RW_AAA_LORA)),
                                mu[3 * RW_W + RW_DECAY_LORA + RW_AAA_LORA:]])[None, :]
        wup = jnp.pad(rw_w_up[l], ((0, LANES - RW_DECAY_LORA), (0, 0))).astype(BF16)
        aup = jnp.pad(rw_a_up[l], ((0, LANES - RW_AAA_LORA), (0, 0))).astype(BF16)
        scan_in, g, bonus = _rwkv_prep(
            xb.reshape(bsz, seq, d), wrw, mu_p, rw_w0[l][None, :], wup, rw_a0[l][None, :], aup,
            rw_g_up[l].astype(BF16), rw_kk[l][None, :], rw_ka[l][None, :], rw_rk[l].reshape(1, RW_W), red, bc,
            t["rwkv_prep"])

        nl = bsz * RW_HEADS
        nl_pad = -(-nl // LANES) * LANES
        ops = scan_in.reshape(SCAN_OPERANDS, seq, nl, RW_DH).transpose(0, 1, 3, 2)
        if nl_pad != nl:
            ops = jnp.pad(ops, ((0, 0), (0, 0), (0, 0), (0, nl_pad - nl)))

        wml = w[:, o_qk:o_ig].astype(BF16)
        wg = _pad_cols(w[:, o_ig:o_rw], LANES).astype(BF16)
        gb = _pad_cols(jnp.concatenate([ml_ig_b[l], ml_fg_b[l]])[None, :], LANES)
        h_ml = _mlstm(xb.reshape(bsz, seq, d), wml, wg, ml_conv_w[l], ml_conv_b[l][None, :], gb,
                      ml_norm_g[l][None, :], t["mlstm"], t["mlstm_chunk"], after=(g,))

        kv = _matmul(memb, ca_w_kv[l].astype(BF16), min(512, bsz * mem_len)).reshape(bsz, mem_len, 2 * CA_W)
        kt = kv[:, :, :CA_W].transpose(0, 2, 1).astype(BF16)
        vm = kv[:, :, CA_W:].astype(BF16)
        h_ca = _xattn(xb.reshape(bsz, seq, d), w[:, o_ca:o_gate].astype(BF16), kt, vm, t["xattn"], after=(g,))

        o_scan = _rwkv_scan(ops, t["scan_steps"], after=(h_ml, h_ca))[:, :, :nl]
        o_rwkv = o_scan.transpose(0, 2, 1).reshape(seq, bsz * RW_W)

        rw_pad = _pad_cols(router_w[l], LANES)
        rw_hi = rw_pad.astype(BF16)
        rw_lo = (rw_pad - rw_hi.astype(F32)).astype(BF16)
        rb = jnp.concatenate([router_b[l], jnp.full((LANES - N_EXPERTS,), NEG_BIG, F32)])[None, :]
        consts = (w[:, o_gate:].astype(BF16), gate_b[l][None, :], red, bc, rw_ln_g[l][None, :], rw_ln_b[l][None, :],
                  w_br_ml[l].astype(BF16), w_br_rw[l].astype(BF16), w_br_ca[l].astype(BF16), w_o[l].astype(BF16),
                  ln1_g[l][None, :], ln1_b[l][None, :], rw_hi, rw_lo, rb)
        x1, x1b, route, counts = _merge(xf, xb, h_ml.reshape(n, ML_W), o_rwkv, bonus.reshape(n, RW_W),
                                        g.reshape(n, RW_W), h_ca.reshape(n, CA_W), consts, t["merge"], group, alpha)

        cnt = counts[:, 0, :N_EXPERTS].astype(jnp.int32)
        run = (cnt + MOE_CHUNK - 1) // MOE_CHUNK * MOE_CHUNK
        run_off = jnp.cumsum(run, axis=1) - run
        group_rows = jnp.sum(run, axis=1)
        blocks_per = (jnp.sum(run, axis=0) + block - 1) // block
        blk_end = jnp.cumsum(blocks_per)
        slot_start = (blk_end - blocks_per) * block
        run_slot = slot_start[None, :] + jnp.cumsum(run, axis=0) - run
        e_idx = route[:, ROUTE_IDX:ROUTE_IDX + TOP_K].astype(jnp.int32)
        rank = route[:, ROUTE_RANK:ROUTE_RANK + TOP_K].astype(jnp.int32)
        grp = jnp.arange(n, dtype=jnp.int32) // group
        pos = run_off[grp[:, None], e_idx] + rank
        pos_cols = jnp.pad(pos, ((0, 0), (0, SUBLANES - TOP_K)))
        pos_rows = jnp.pad(pos.reshape(n_groups, group, TOP_K).transpose(0, 2, 1),
                           ((0, 0), (0, SUBLANES - TOP_K), (0, 0)))
        chunk_row = jnp.arange(n_chunk, dtype=jnp.int32) * MOE_CHUNK
        chunk_e = jnp.minimum(jnp.sum((chunk_row[None, :, None] >= (run_off + run)[:, None, :]).astype(jnp.int32),
                                      axis=2), N_EXPERTS - 1)
        chunk_slot = (jnp.take_along_axis(run_slot, chunk_e, axis=1) + chunk_row[None, :]
                      - jnp.take_along_axis(run_off, chunk_e, axis=1))
        chunk_slot = jnp.where(chunk_row[None, :] < group_rows[:, None], chunk_slot, 0)
        table = jnp.zeros((n_groups, CHUNK_TABLE), jnp.int32)
        table = table.at[:, :n_chunk].set(chunk_slot).at[:, CHUNK_TABLE - 1].set(group_rows // MOE_CHUNK)
        table = table.reshape(n_groups * CHUNK_TABLE)
        blk_ids = jnp.arange(n_blocks, dtype=jnp.int32)
        blk_e = jnp.minimum(jnp.sum((blk_ids[:, None] >= blk_end[None, :]).astype(jnp.int32), axis=1),
                            N_EXPERTS - 1)
        nused = blk_end[-1:].astype(jnp.int32)

        last_blk = jnp.maximum(blk_end - 1, 0).astype(jnp.int32)
        trailing = jnp.minimum(nused[0] + jnp.arange(n_blocks - n_asg // block, dtype=jnp.int32), n_blocks - 1)
        zero_blk = jnp.concatenate([last_blk, trailing])
        xd = _dispatch(table, zero_blk, pos_rows, x1b, n_rows, group, block)
        w_glu, w_lin = _deinterleave(w_gu, l)
        y_disp = _experts(blk_e, nused, xd, w_glu, w_lin,
                          b_gu[l][:, None, 0::2], b_gu[l][:, None, 1::2], w_dn[l].astype(BF16),
                          b_dn[l][:, None, :], block)
        xf, xb = _combine(table, pos_cols, route, x1, y_disp, ln2_g[l], ln2_b[l], group, alpha)

    del d_ff
    return xf.reshape(bsz, seq, d)
```

```python
import functools

import jax
import jax.numpy as jnp
from jax import lax
from jax.experimental import pallas as pl
from jax.experimental.pallas import tpu as pltpu

ML_HEADS, ML_DQK, ML_DV, ML_CONV = 4, 64, 128, 4
ML_W = ML_HEADS * ML_DV
ML_QK_W = 2 * ML_HEADS * ML_DQK
RW_HEADS, RW_DH = 8, 64
RW_W = RW_HEADS * RW_DH
RW_DECAY_LORA, RW_AAA_LORA, RW_GATE_LORA = 64, 64, 128
RW_GN_EPS = 64e-5
CA_HEADS, CA_DH = 4, 128
CA_W = CA_HEADS * CA_DH
N_BRANCH = 3
N_EXPERTS, TOP_K = 32, 4
SWIGLU_LIMIT, SWIGLU_ALPHA = 7.0, 1.702
LN_EPS = 1e-5

LANES = 128
SUBLANES = 8
VMEM_LIMIT = 56 * 1024 * 1024

BF16 = jnp.bfloat16
F32 = jnp.float32
NEG_BIG = -1e30


def _dot(a, b):
    return jnp.dot(a.astype(BF16), b.astype(BF16), preferred_element_type=F32)


def _split3(a):
    hi = a.astype(BF16)
    r1 = a - hi.astype(F32)
    mid = r1.astype(BF16)
    lo = (r1 - mid.astype(F32)).astype(BF16)
    return hi, mid, lo


def _dot2_rhs(a, b_bf16):
    hi = a.astype(BF16)
    lo = (a - hi.astype(F32)).astype(BF16)
    d = functools.partial(jnp.dot, preferred_element_type=F32)
    return d(hi, b_bf16) + d(lo, b_bf16)


def _head_sum(x, red, bc):
    return _dot2_rhs(_dot2_rhs(x, red), bc)


def _dot_exact_lhs(a_bf16, b):
    hi, mid, lo = _split3(b)
    d = functools.partial(jnp.dot, preferred_element_type=F32)
    return d(a_bf16, hi) + d(a_bf16, mid) + d(a_bf16, lo)


def _sigmoid(x):
    return 1.0 / (1.0 + jnp.exp(-x))


def _softplus(x):
    return jnp.maximum(x, 0.0) + jnp.log1p(jnp.exp(-jnp.abs(x)))


def _layer_norm(v, g, b, eps=LN_EPS):
    mu = jnp.mean(v, axis=-1, keepdims=True)
    c = v - mu
    var = jnp.mean(c * c, axis=-1, keepdims=True)
    return c * lax.rsqrt(var + eps) * g + b


def _params(*sem):
    return pltpu.CompilerParams(dimension_semantics=sem, vmem_limit_bytes=VMEM_LIMIT)


def _ordered_after(kernel, first, count):
    def wrapped(*refs):
        return kernel(*refs[:first], *refs[first + count:])
    return wrapped


def _after_specs(after):
    return [pl.BlockSpec(memory_space=pl.ANY)] * len(after)


def _const_spec(shape):
    nd = len(shape)
    return pl.BlockSpec(shape, lambda *_: (0,) * nd, pipeline_mode=pl.Buffered(1))


def _ln_kernel(x_ref, g_ref, b_ref, o_ref, ob_ref):
    y = _layer_norm(x_ref[...], g_ref[...], b_ref[...])
    o_ref[...] = y
    ob_ref[...] = y.astype(BF16)


def _ln_rows(x2d, g, b, tile):
    n, d = x2d.shape
    return pl.pallas_call(
        _ln_kernel,
        grid=(n // tile,),
        in_specs=[pl.BlockSpec((tile, d), lambda i: (i, 0)), _const_spec((1, d)), _const_spec((1, d))],
        out_specs=[pl.BlockSpec((tile, d), lambda i: (i, 0)), pl.BlockSpec((tile, d), lambda i: (i, 0))],
        out_shape=[jax.ShapeDtypeStruct((n, d), F32), jax.ShapeDtypeStruct((n, d), BF16)],
        compiler_params=_params("parallel"),
        name="ln_rows",
    )(x2d, g.reshape(1, d), b.reshape(1, d))


def _mm_kernel(a_ref, b_ref, o_ref):
    o_ref[...] = jnp.dot(a_ref[...], b_ref[...], preferred_element_type=F32)


def _matmul(a, b, tile):
    m, k = a.shape
    n = b.shape[1]
    return pl.pallas_call(
        _mm_kernel,
        grid=(m // tile,),
        in_specs=[pl.BlockSpec((tile, k), lambda i: (i, 0)), _const_spec((k, n))],
        out_specs=pl.BlockSpec((tile, n), lambda i: (i, 0)),
        out_shape=jax.ShapeDtypeStruct((m, n), F32),
        compiler_params=_params("parallel"),
        name="matmul",
    )(a, b)


def _mlstm_kernel(xb_ref, wml_ref, wg_ref, convw_ref, convb_ref, gb_ref, normg_ref, tri_ref,
                  h_ref, ubuf, c_ref, n_ref, m_ref, *, tile, chunk):
    @pl.when(pl.program_id(1) == 0)
    def _():
        ubuf[0:SUBLANES, :] = jnp.zeros((SUBLANES, ML_QK_W), F32)
        c_ref[...] = jnp.zeros_like(c_ref)
        n_ref[...] = jnp.zeros_like(n_ref)
        m_ref[...] = jnp.zeros_like(m_ref)

    xb = xb_ref[...]
    u = jnp.dot(xb, wml_ref[...], preferred_element_type=F32)
    gates = jnp.dot(xb, wg_ref[...], preferred_element_type=F32) + gb_ref[...]

    ubuf[SUBLANES:SUBLANES + tile, :] = u[:, :ML_QK_W]
    acc = jnp.broadcast_to(convb_ref[...], (tile, ML_QK_W))
    for j in range(ML_CONV):
        acc = acc + convw_ref[j:j + 1, :] * ubuf[pl.ds(SUBLANES - ML_CONV + 1 + j, tile), :]
    ubuf[0:SUBLANES, :] = ubuf[tile:tile + SUBLANES, :]
    qk = acc * _sigmoid(acc)
    q = qk[:, :ML_HEADS * ML_DQK] * (ML_DQK ** -0.5)
    k = qk[:, ML_HEADS * ML_DQK:]
    v = u[:, ML_QK_W:ML_QK_W + ML_W]
    og = u[:, ML_QK_W + ML_W:]

    lane = lax.broadcasted_iota(jnp.int32, (chunk, LANES), 1)
    row = lax.broadcasted_iota(jnp.int32, (chunk, chunk), 0)
    col = lax.broadcasted_iota(jnp.int32, (chunk, chunk), 1)
    causal = row >= col
    log_f = -_softplus(-gates)

    for c in range(tile // chunk):
        rs = slice(c * chunk, (c + 1) * chunk)
        bcum = _dot_exact_lhs(tri_ref[...], log_f[rs])
        mcol = jnp.where(lane < ML_HEADS, gates[rs], bcum)
        mrow = mcol.T
        for h in range(ML_HEADS):
            i_col = mcol[:, h:h + 1]
            b_col = mcol[:, ML_HEADS + h:ML_HEADS + h + 1]
            i_row = mrow[h:h + 1, :]
            b_row = mrow[ML_HEADS + h:ML_HEADS + h + 1, :]
            m_prev = m_ref[h][0:1, 0:1]
            n_prev = n_ref[h][0:1, :]
            c_prev = c_ref[h]
            qh = q[rs, h * ML_DQK:(h + 1) * ML_DQK]
            kh = k[rs, h * ML_DQK:(h + 1) * ML_DQK]
            vh = v[rs, h * ML_DV:(h + 1) * ML_DV]

            dmat = jnp.where(causal, b_col - b_row + i_row, -jnp.inf)
            m_inter = b_col + m_prev
            m_t = jnp.maximum(m_inter, jnp.max(dmat, axis=1, keepdims=True))
            s = lax.dot_general(qh.astype(BF16), kh.astype(BF16), (((1,), (1,)), ((), ())),
                                preferred_element_type=F32)
            w_intra = jnp.exp(dmat - m_t) * s
            s_inter = jnp.exp(m_inter - m_t)
            num = s_inter * _dot(qh, c_prev) + _dot(w_intra, vh)
            den = (s_inter * jnp.sum(qh * n_prev, axis=1, keepdims=True)
                   + jnp.sum(w_intra, axis=1, keepdims=True))
            hh = num / jnp.maximum(jnp.abs(den), jnp.exp(-m_t))

            b_last = b_col[chunk - 1:chunk, :]
            g_col = b_last - b_col + i_col
            m_new = jnp.maximum(b_last + m_prev, jnp.max(g_col, axis=0, keepdims=True))
            carry = jnp.exp(b_last + m_prev - m_new)
            kw = kh * jnp.exp(g_col - m_new)
            c_ref[h] = carry * c_prev + lax.dot_general(
                kw.astype(BF16), vh.astype(BF16), (((0,), (0,)), ((), ())), preferred_element_type=F32)
            n_new = carry * n_prev + jnp.sum(kw, axis=0, keepdims=True)
            n_ref[h] = jnp.broadcast_to(n_new, (SUBLANES, ML_DQK))
            m_ref[h] = jnp.broadcast_to(m_new, (SUBLANES, LANES))

            mu = jnp.mean(hh, axis=1, keepdims=True)
            cen = hh - mu
            var = jnp.mean(cen * cen, axis=1, keepdims=True)
            y = cen * lax.rsqrt(var + LN_EPS) * normg_ref[:, h * ML_DV:(h + 1) * ML_DV]
            h_ref[rs, h * ML_DV:(h + 1) * ML_DV] = (
                _sigmoid(og[rs, h * ML_DV:(h + 1) * ML_DV]) * y).astype(BF16)


def _mlstm(xb, wml, wg, convw, convb, gb, normg, tile, chunk, after=()):
    bsz, s, d = xb.shape
    tri = (jnp.arange(chunk)[:, None] >= jnp.arange(chunk)[None, :]).astype(BF16)
    kern = _ordered_after(functools.partial(_mlstm_kernel, tile=tile, chunk=chunk), 8, len(after))
    return pl.pallas_call(
        kern,
        grid=(bsz, s // tile),
        in_specs=[pl.BlockSpec((None, tile, d), lambda b, j: (b, j, 0)),
                  _const_spec(wml.shape), _const_spec(wg.shape), _const_spec(convw.shape),
                  _const_spec(convb.shape), _const_spec(gb.shape), _const_spec(normg.shape),
                  _const_spec(tri.shape)] + _after_specs(after),
        out_specs=pl.BlockSpec((None, tile, ML_W), lambda b, j: (b, j, 0)),
        out_shape=jax.ShapeDtypeStruct((bsz, s, ML_W), BF16),
        scratch_shapes=[pltpu.VMEM((tile + SUBLANES, ML_QK_W), F32),
                        pltpu.VMEM((ML_HEADS, ML_DQK, ML_DV), F32),
                        pltpu.VMEM((ML_HEADS, SUBLANES, ML_DQK), F32),
                        pltpu.VMEM((ML_HEADS, SUBLANES, LANES), F32)],
        compiler_params=_params("parallel", "arbitrary"),
        name="mlstm",
    )(xb, wml, wg, convw, convb, gb, normg, tri, *after)


RW_PAD_IN = 3 * RW_W + 3 * LANES
SCAN_R, SCAN_W, SCAN_K, SCAN_V, SCAN_KK, SCAN_KKA = range(6)
SCAN_OPERANDS = 6


def _rwkv_prep_kernel(xb_ref, wrw_ref, mu_ref, w0_ref, wup_ref, a0_ref, aup_ref, gup_ref, kkw_ref,
                      ka_ref, rk_ref, red_ref, bc_ref,
                      scan_ref, g_ref, bonus_ref, ubuf, *, tile):
    @pl.when(pl.program_id(1) == 0)
    def _():
        ubuf[0:SUBLANES, :] = jnp.zeros((SUBLANES, RW_PAD_IN), F32)

    u = jnp.dot(xb_ref[...], wrw_ref[...], preferred_element_type=F32)
    ubuf[SUBLANES:SUBLANES + tile, :] = u
    u_prev = ubuf[pl.ds(SUBLANES - 1, tile), :]
    ubuf[0:SUBLANES, :] = ubuf[tile:tile + SUBLANES, :]
    us = u + (u_prev - u) * mu_ref[...]
    r = us[:, 0:RW_W]
    kr = us[:, RW_W:2 * RW_W]
    vr = us[:, 2 * RW_W:3 * RW_W]
    wd = us[:, 3 * RW_W:3 * RW_W + LANES]
    ad = us[:, 3 * RW_W + LANES:3 * RW_W + 2 * LANES]
    gd = us[:, 3 * RW_W + 2 * LANES:]

    w_log = -_softplus(-(w0_ref[...] + _dot(jnp.tanh(wd), wup_ref[...]))) - 0.5
    decay = jnp.exp(-jnp.exp(w_log))
    a = _sigmoid(a0_ref[...] + _dot(ad, aup_ref[...]))
    g = _dot(_sigmoid(gd), gup_ref[...])

    red, bc = red_ref[...], bc_ref[...]
    kk = kr * kkw_ref[...]
    kk = kk * lax.rsqrt(jnp.maximum(_head_sum(kk * kk, red, bc), 1e-24))
    k2 = kr * (1.0 + (a - 1.0) * ka_ref[...])
    bonus = _head_sum(r * k2 * rk_ref[...], red, bc) * vr

    scan_ref[SCAN_R] = r
    scan_ref[SCAN_W] = decay
    scan_ref[SCAN_K] = k2
    scan_ref[SCAN_V] = vr
    scan_ref[SCAN_KK] = kk
    scan_ref[SCAN_KKA] = kk * a
    g_ref[...] = g
    bonus_ref[...] = bonus


def _rwkv_prep(xb, wrw, mu, w0, wup, a0, aup, gup, kkw, ka, rk, red, bc, tile):
    bsz, s, d = xb.shape
    consts = (wrw, mu, w0, wup, a0, aup, gup, kkw, ka, rk, red, bc)
    spec = pl.BlockSpec((None, tile, RW_W), lambda b, j: (b, j, 0))
    tspec = pl.BlockSpec((SCAN_OPERANDS, tile, RW_W), lambda b, j: (0, j, b))
    return pl.pallas_call(
        functools.partial(_rwkv_prep_kernel, tile=tile),
        grid=(bsz, s // tile),
        in_specs=[pl.BlockSpec((None, tile, d), lambda b, j: (b, j, 0))] + [_const_spec(c.shape) for c in consts],
        out_specs=[tspec, spec, spec],
        out_shape=[jax.ShapeDtypeStruct((SCAN_OPERANDS, s, bsz * RW_W), F32)]
        + [jax.ShapeDtypeStruct((bsz, s, RW_W), F32)] * 2,
        scratch_shapes=[pltpu.VMEM((tile + SUBLANES, RW_PAD_IN), F32)],
        compiler_params=_params("parallel", "arbitrary"),
        name="rwkv_prep",
    )(xb, *consts)


def _rwkv_scan_kernel(x_ref, o_ref, st_ref, *, steps):
    @pl.when(pl.program_id(1) == 0)
    def _():
        st_ref[...] = jnp.zeros_like(st_ref)

    r_ref, w_ref, k_ref, v_ref, kk_ref, kka_ref = (
        x_ref.at[i] for i in (SCAN_R, SCAN_W, SCAN_K, SCAN_V, SCAN_KK, SCAN_KKA))
    zeros = jnp.zeros((RW_DH, LANES), F32)

    def first(kc, sa):
        return sa + st_ref[kc] * kk_ref[0, pl.ds(kc, 1), :]

    sa0 = lax.fori_loop(0, RW_DH, first, zeros, unroll=8)

    def step(t, sa):
        vt = v_ref[t]
        tn = jnp.minimum(t + 1, steps - 1)

        def body(kc, carry):
            out, sa_next = carry
            row = pl.ds(kc, 1)
            new = st_ref[kc] * w_ref[t, row, :] - sa * kka_ref[t, row, :] + vt * k_ref[t, row, :]
            st_ref[kc] = new
            return out + new * r_ref[t, row, :], sa_next + new * kk_ref[tn, row, :]

        out, sa_next = lax.fori_loop(0, RW_DH, body, (zeros, zeros), unroll=16)
        o_ref[t] = out
        return sa_next

    lax.fori_loop(0, steps, step, sa0)


def _rwkv_scan(x, steps, after=()):
    _, s, dh, nl = x.shape
    spec = pl.BlockSpec((steps, dh, LANES), lambda g, j: (j, 0, g))
    return pl.pallas_call(
        _ordered_after(functools.partial(_rwkv_scan_kernel, steps=steps), 1, len(after)),
        grid=(nl // LANES, s // steps),
        in_specs=[pl.BlockSpec((SCAN_OPERANDS, steps, dh, LANES), lambda g, j: (0, j, 0, g))]
        + _after_specs(after),
        out_specs=spec,
        out_shape=jax.ShapeDtypeStruct((s, dh, nl), F32),
        scratch_shapes=[pltpu.VMEM((dh, dh, LANES), F32)],
        compiler_params=_params("parallel", "arbitrary"),
        name="rwkv_scan",
    )(x, *after)


def _xattn_kernel(xb_ref, wq_ref, kt_ref, v_ref, o_ref):
    q = jnp.dot(xb_ref[...], wq_ref[...], preferred_element_type=F32)
    for h in range(CA_HEADS):
        hs = slice(h * CA_DH, (h + 1) * CA_DH)
        s = jnp.dot(q[:, hs].astype(BF16), kt_ref[hs, :], preferred_element_type=F32) * (CA_DH ** -0.5)
        p = jnp.exp(s - jnp.max(s, axis=1, keepdims=True))
        den = jnp.sum(p, axis=1, keepdims=True)
        o = jnp.dot(p.astype(BF16), v_ref[:, hs], preferred_element_type=F32) / den
        o_ref[:, hs] = o.astype(BF16)


def _xattn(xb, wq, kt, v, tile, after=()):
    bsz, s, d = xb.shape
    m = v.shape[1]
    return pl.pallas_call(
        _ordered_after(_xattn_kernel, 4, len(after)),
        grid=(bsz, s // tile),
        in_specs=[pl.BlockSpec((None, tile, d), lambda b, j: (b, j, 0)), _const_spec(wq.shape),
                  pl.BlockSpec((None, CA_W, m), lambda b, j: (b, 0, 0)),
                  pl.BlockSpec((None, m, CA_W), lambda b, j: (b, 0, 0))] + _after_specs(after),
        out_specs=pl.BlockSpec((None, tile, CA_W), lambda b, j: (b, j, 0)),
        out_shape=jax.ShapeDtypeStruct((bsz, s, CA_W), BF16),
        compiler_params=_params("parallel", "parallel"),
        name="xattn",
    )(xb, wq, kt, v, *after)


ROUTE_IDX, ROUTE_RANK, ROUTE_GATE = 0, TOP_K, 2 * TOP_K


def _merge_kernel(x_ref, xb_ref, hml_ref, o_ref, bonus_ref, g_ref, hca_ref,
                  wgate_ref, gateb_ref, red_ref, bc_ref, rwg_ref, rwb_ref, wml_ref, wrw_ref, wca_ref, wo_ref,
                  lng_ref, lnb_ref, rwhi_ref, rwlo_ref, rb_ref, tri_ref,
                  x1_ref, x1b_ref, route_ref, cnt_ref, *, tile, group, alpha, d_model):
    red, bc = red_ref[...], bc_ref[...]
    o = o_ref[...]
    mu = _head_sum(o, red, bc) * (1.0 / RW_DH)
    cen = o - mu
    var = _head_sum(cen * cen, red, bc) * (1.0 / RW_DH)
    h_rw = (cen * lax.rsqrt(var + RW_GN_EPS) * rwg_ref[...] + rwb_ref[...] + bonus_ref[...]) * g_ref[...]

    xb = xb_ref[...]
    y = None
    for br, (h_br, w_ref) in enumerate(((hml_ref[...], wml_ref), (h_rw.astype(BF16), wrw_ref),
                                        (hca_ref[...], wca_ref))):
        cols = slice(br * d_model, (br + 1) * d_model)
        gate = _sigmoid(jnp.dot(xb, wgate_ref[:, cols], preferred_element_type=F32) + gateb_ref[:, cols])
        term = gate * jnp.dot(h_br, w_ref[...], preferred_element_type=F32)
        y = term if y is None else y + term
    mixed = _dot(y, wo_ref[...])
    x1 = _layer_norm(alpha * x_ref[...] + mixed, lng_ref[...], lnb_ref[...])
    x1_ref[...] = x1
    x1b_ref[...] = x1.astype(BF16)

    hi = x1.astype(BF16)
    lo = (x1 - hi.astype(F32)).astype(BF16)
    d = functools.partial(jnp.dot, preferred_element_type=F32)
    logits = d(hi, rwhi_ref[...]) + d(lo, rwhi_ref[...]) + d(hi, rwlo_ref[...]) + rb_ref[...]

    lane = lax.broadcasted_iota(jnp.int32, (tile, LANES), 1)
    vals = logits
    tops, onehots = [], []
    for _ in range(TOP_K):
        m = jnp.max(vals, axis=1, keepdims=True)
        idx = jnp.min(jnp.where(vals == m, lane, LANES), axis=1, keepdims=True)
        sel = lane == idx
        tops.append((m, idx))
        onehots.append(sel.astype(F32))
        vals = jnp.where(sel, -jnp.inf, vals)
    exps = [jnp.exp(m - tops[0][0]) for m, _ in tops]
    den = exps[0] + exps[1] + exps[2] + exps[3]

    cnt = onehots[0] + onehots[1] + onehots[2] + onehots[3]
    befores = []
    for gi in range(tile // group):
        cg = cnt[gi * group:(gi + 1) * group]
        befores.append(jnp.dot(tri_ref[...], cg.astype(BF16), preferred_element_type=F32))
        cnt_ref[gi] = jnp.broadcast_to(jnp.sum(cg, axis=0, keepdims=True), (SUBLANES, LANES))
    before = befores[0] if len(befores) == 1 else jnp.concatenate(befores, axis=0)
    route = jnp.zeros((tile, LANES), F32)
    for kq in range(TOP_K):
        rank = jnp.sum(onehots[kq] * before, axis=1, keepdims=True)
        route = jnp.where(lane == ROUTE_IDX + kq, tops[kq][1].astype(F32), route)
        route = jnp.where(lane == ROUTE_RANK + kq, rank, route)
        route = jnp.where(lane == ROUTE_GATE + kq, exps[kq] / den, route)
    route_ref[...] = route


def _merge(x, xb, hml, o_tm, bonus, g, hca, consts, tile, group, alpha):
    n, d = x.shape
    tiles_per_seq = o_tm.shape[0] // tile
    gpt = tile // group
    tri = (jnp.arange(group)[:, None] > jnp.arange(group)[None, :]).astype(BF16)
    consts = tuple(consts) + (tri,)

    def rows(w):
        return pl.BlockSpec((tile, w), lambda i: (i, 0))

    o_spec = pl.BlockSpec((tile, RW_W), lambda i: (i % tiles_per_seq, i // tiles_per_seq))
    return pl.pallas_call(
        functools.partial(_merge_kernel, tile=tile, group=group, alpha=alpha, d_model=d),
        grid=(n // tile,),
        in_specs=[rows(d), rows(d), rows(ML_W), o_spec, rows(RW_W), rows(RW_W), rows(CA_W)]
        + [_const_spec(c.shape) for c in consts],
        out_specs=[rows(d), rows(d), rows(LANES),
                   pl.BlockSpec((gpt, SUBLANES, LANES), lambda i: (i, 0, 0))],
        out_shape=[jax.ShapeDtypeStruct((n, d), F32), jax.ShapeDtypeStruct((n, d), BF16),
                   jax.ShapeDtypeStruct((n, LANES), F32),
                   jax.ShapeDtypeStruct((n // group, SUBLANES, LANES), F32)],
        compiler_params=_params("parallel"),
        name="merge_route",
    )(x, xb, hml, o_tm, bonus, g, hca, *consts)


MOE_CHUNK = SUBLANES
CHUNK_TABLE = 1024


def _group_rows(group):
    return -(-(group * TOP_K + N_EXPERTS * (MOE_CHUNK - 1)) // LANES) * LANES


def _ordered_positions(route, run_off_row):
    lane = lax.broadcasted_iota(jnp.int32, route.shape, 1).astype(F32)
    cols = []
    for kq in range(TOP_K):
        sel = lane == route[:, ROUTE_IDX + kq:ROUTE_IDX + kq + 1]
        cols.append(jnp.sum(jnp.where(sel, run_off_row, 0.0), axis=1, keepdims=True)
                    + route[:, ROUTE_RANK + kq:ROUTE_RANK + kq + 1])
    return cols


def _dispatch_kernel(tab_ref, zero_blk_ref, roff_ref, route_ref, xb_ref, xd_ref, xs_buf, zbuf, live_ref, sems, zsem,
                     *, group, nsteps, block, prows):
    i = pl.program_id(0)
    slot = i % 2

    @pl.when(i == 0)
    def _():
        zbuf[...] = jnp.zeros_like(zbuf)
        n_zero = zero_blk_ref.shape[0]

        def fresh(j):
            return jnp.logical_or(j == 0, zero_blk_ref[j] != zero_blk_ref[jnp.maximum(j - 1, 0)])

        def zero_copy(j):
            return pltpu.make_async_copy(zbuf, xd_ref.at[pl.ds(zero_blk_ref[j] * block, block)], zsem)

        def start(j, carry):
            @pl.when(fresh(j))
            def _():
                zero_copy(j).start()
            return carry

        def wait(j, carry):
            @pl.when(fresh(j))
            def _():
                zero_copy(j).wait()
            return carry

        lax.fori_loop(0, n_zero, start, 0)
        lax.fori_loop(0, n_zero, wait, 0)

    def chunk_copy(c, s):
        return pltpu.make_async_copy(xs_buf.at[s, pl.ds(c * MOE_CHUNK, MOE_CHUNK)],
                                     xd_ref.at[pl.ds(pl.multiple_of(tab_ref[c], MOE_CHUNK), MOE_CHUNK)],
                                     sems.at[s])

    def retire(s):
        def wait(c, carry):
            chunk_copy(0, s).wait()
            return carry

        lax.fori_loop(0, live_ref[s], wait, 0)

    @pl.when(i >= 2)
    def _():
        retire(slot)

    route = route_ref[...]
    lane = lax.broadcasted_iota(jnp.int32, route.shape, 1)
    pos_lanes = jnp.full(route.shape, -1.0, F32)
    for kq, col in enumerate(_ordered_positions(route, roff_ref[0:1, :])):
        pos_lanes = jnp.where(lane == kq, col, pos_lanes)
    pos_t = pos_lanes.T
    rows = lax.broadcasted_iota(jnp.int32, (prows, group), 0).astype(F32)
    pick = jnp.zeros((prows, group), F32)
    for kq in range(TOP_K):
        pick = pick + jnp.where(rows == pos_t[kq:kq + 1, :], 1.0, 0.0)
    xs_buf[slot] = jnp.dot(pick.astype(BF16), xb_ref[...], preferred_element_type=F32)

    live = tab_ref[CHUNK_TABLE - 1]
    live_ref[slot] = live

    def issue(c, carry):
        chunk_copy(c, slot).start()
        return carry

    lax.fori_loop(0, live, issue, 0)

    @pl.when(i == nsteps - 1)
    def _():
        retire(slot)
        if nsteps > 1:
            retire(1 - slot)


def _dispatch(table, zero_blk, run_off, route, x1b, n_rows, group, block):
    n, d = x1b.shape
    nsteps = n // group
    prows = _group_rows(group)
    return pl.pallas_call(
        functools.partial(_dispatch_kernel, group=group, nsteps=nsteps, block=block, prows=prows),
        grid=(nsteps,),
        in_specs=[pl.BlockSpec((CHUNK_TABLE,), lambda i: (i,), memory_space=pltpu.SMEM),
                  pl.BlockSpec(memory_space=pltpu.SMEM),
                  pl.BlockSpec((None, SUBLANES, LANES), lambda i: (i, 0, 0)),
                  pl.BlockSpec((group, LANES), lambda i: (i, 0)),
                  pl.BlockSpec((group, d), lambda i: (i, 0))],
        out_specs=pl.BlockSpec(memory_space=pl.ANY),
        out_shape=jax.ShapeDtypeStruct((n_rows, d), F32),
        scratch_shapes=[pltpu.VMEM((2, prows, d), F32), pltpu.VMEM((block, d), F32),
                        pltpu.SMEM((2,), jnp.int32),
                        pltpu.SemaphoreType.DMA((2,)), pltpu.SemaphoreType.DMA],
        compiler_params=_params("arbitrary"),
        name="moe_dispatch",
    )(table, zero_blk, run_off, route, x1b)


DEINT_COLS = 512


def _deinterleave_kernel(w_ref, pe_ref, po_ref, glu_ref, lin_ref):
    wb = w_ref[...].astype(BF16)
    glu_ref[...] = jnp.dot(wb, pe_ref[...], preferred_element_type=F32).astype(BF16)
    lin_ref[...] = jnp.dot(wb, po_ref[...], preferred_element_type=F32).astype(BF16)


def _deinterleave(w_gu_all, layer):
    _, e, d, two_ff = w_gu_all.shape
    half = DEINT_COLS // 2
    src = jnp.arange(DEINT_COLS)[:, None]
    dst = jnp.arange(half)[None, :]
    pe = (src == 2 * dst).astype(BF16)
    po = (src == 2 * dst + 1).astype(BF16)
    out = jax.ShapeDtypeStruct((e, d, two_ff // 2), BF16)
    return pl.pallas_call(
        _deinterleave_kernel,
        grid=(e, two_ff // DEINT_COLS),
        in_specs=[pl.BlockSpec((None, None, d, DEINT_COLS), lambda i, c: (layer, i, 0, c)),
                  _const_spec(pe.shape), _const_spec(po.shape)],
        out_specs=[pl.BlockSpec((None, d, half), lambda i, c: (i, 0, c))] * 2,
        out_shape=[out, out],
        compiler_params=_params("parallel", "parallel"),
        name="deinterleave_w",
    )(w_gu_all, pe, po)


def _expert_kernel(blk_e_ref, nused_ref, xd_ref, wglu_ref, wlin_ref, bglu_ref, blin_ref, wdn_ref, bdn_ref, y_ref):
    del blk_e_ref
    live = pl.program_id(0) < nused_ref[0]

    @pl.when(live)
    def _():
        xb = xd_ref[...].astype(BF16)
        h_glu = jnp.dot(xb, wglu_ref[...], preferred_element_type=F32) + bglu_ref[...]
        h_lin = jnp.dot(xb, wlin_ref[...], preferred_element_type=F32) + blin_ref[...]
        x_glu = jnp.minimum(h_glu, SWIGLU_LIMIT)
        x_lin = jnp.clip(h_lin, -SWIGLU_LIMIT, SWIGLU_LIMIT)
        act = x_glu * _sigmoid(SWIGLU_ALPHA * x_glu) * (x_lin + 1.0)
        y_ref[...] = _dot(act, wdn_ref[...]) + bdn_ref[...]

    @pl.when(jnp.logical_not(live))
    def _():
        y_ref[...] = jnp.zeros_like(y_ref)


def _experts(blk_e, nused, xd, wglu, wlin, bglu, blin, wdn, bdn, block):
    n_rows, d = xd.shape
    dff = wglu.shape[2]
    grid_spec = pltpu.PrefetchScalarGridSpec(
        num_scalar_prefetch=2,
        grid=(n_rows // block,),
        in_specs=[pl.BlockSpec((block, d), lambda i, be, nu: (jnp.minimum(i, nu[0] - 1), 0)),
                  pl.BlockSpec((None, d, dff), lambda i, be, nu: (be[i], 0, 0)),
                  pl.BlockSpec((None, d, dff), lambda i, be, nu: (be[i], 0, 0)),
                  pl.BlockSpec((None, 1, dff), lambda i, be, nu: (be[i], 0, 0)),
                  pl.BlockSpec((None, 1, dff), lambda i, be, nu: (be[i], 0, 0)),
                  pl.BlockSpec((None, dff, d), lambda i, be, nu: (be[i], 0, 0)),
                  pl.BlockSpec((None, 1, d), lambda i, be, nu: (be[i], 0, 0))],
        out_specs=pl.BlockSpec((block, d), lambda i, be, nu: (i, 0)),
    )
    return pl.pallas_call(
        _expert_kernel,
        grid_spec=grid_spec,
        out_shape=jax.ShapeDtypeStruct((n_rows, d), F32),
        compiler_params=_params("arbitrary"),
        name="moe_experts",
    )(blk_e, nused, xd, wglu, wlin, bglu, blin, wdn, bdn)


def _combine_kernel(tab_ref, tab_next_ref, roff_ref, route_ref, x1_ref, y_ref, lng_ref, lnb_ref, x2_ref, x2b_ref,
                    ybuf, sems, *, group, alpha, nsteps, prows):
    i = pl.program_id(0)
    slot = i % 2
    nchunk = prows // MOE_CHUNK

    def fetch(t_ref, s):
        def issue(c, carry):
            pltpu.make_async_copy(y_ref.at[pl.ds(pl.multiple_of(t_ref[c], MOE_CHUNK), MOE_CHUNK)],
                                  ybuf.at[s, pl.ds(c * MOE_CHUNK, MOE_CHUNK)], sems.at[s]).start()
            return carry

        lax.fori_loop(0, nchunk, issue, 0, unroll=4)

    @pl.when(i == 0)
    def _():
        fetch(tab_ref, slot)

    @pl.when(i + 1 < nsteps)
    def _():
        fetch(tab_next_ref, 1 - slot)

    pltpu.make_async_copy(y_ref.at[pl.ds(0, prows)], ybuf.at[slot], sems.at[slot]).wait()
    route = route_ref[...]
    cols = lax.broadcasted_iota(jnp.int32, (group, prows), 1).astype(F32)
    weights = jnp.zeros((group, prows), F32)
    for kq, pos in enumerate(_ordered_positions(route, roff_ref[0:1, :])):
        weights = weights + jnp.where(cols == pos, route[:, ROUTE_GATE + kq:ROUTE_GATE + kq + 1], 0.0)
    moe = jnp.dot(weights.astype(BF16), ybuf[slot].astype(BF16), preferred_element_type=F32)
    x2 = _layer_norm(alpha * x1_ref[...] + moe, lng_ref[...], lnb_ref[...])
    x2_ref[...] = x2
    x2b_ref[...] = x2.astype(BF16)


def _combine(table, run_off, route, x1, y_disp, lng, lnb, group, alpha):
    n, d = x1.shape
    nsteps = n // group
    prows = _group_rows(group)
    return pl.pallas_call(
        functools.partial(_combine_kernel, group=group, alpha=alpha, nsteps=nsteps, prows=prows),
        grid=(nsteps,),
        in_specs=[pl.BlockSpec((CHUNK_TABLE,), lambda i: (i,), memory_space=pltpu.SMEM),
                  pl.BlockSpec((CHUNK_TABLE,), lambda i: (jnp.minimum(i + 1, nsteps - 1),),
                               memory_space=pltpu.SMEM),
                  pl.BlockSpec((None, SUBLANES, LANES), lambda i: (i, 0, 0)),
                  pl.BlockSpec((group, LANES), lambda i: (i, 0)),
                  pl.BlockSpec((group, d), lambda i: (i, 0)),
                  pl.BlockSpec(memory_space=pl.ANY),
                  _const_spec((1, d)), _const_spec((1, d))],
        out_specs=[pl.BlockSpec((group, d), lambda i: (i, 0)), pl.BlockSpec((group, d), lambda i: (i, 0))],
        out_shape=[jax.ShapeDtypeStruct((n, d), F32), jax.ShapeDtypeStruct((n, d), BF16)],
        scratch_shapes=[pltpu.VMEM((2, prows, d), F32), pltpu.SemaphoreType.DMA((2,))],
        compiler_params=_params("arbitrary"),
        name="moe_combine",
    )(table, table, run_off, route, x1, y_disp, lng.reshape(1, d), lnb.reshape(1, d))


def _tiles(bsz, seq):
    n = bsz * seq
    return dict(
        ln=min(1024, n),
        mlstm=min(512, seq), mlstm_chunk=min(128, seq),
        rwkv_prep=min(512, seq),
        scan_steps=min(32, seq),
        xattn=min(512, seq),
        merge=min(512, seq),
        moe_group=min(256, seq),
        moe_block=512,
    )


def _pad_cols(w, width):
    return jnp.pad(w, ((0, 0), (0, width - w.shape[1])))


def kernel(x, mem, ln_in_g, ln_in_b, mem_ln_g, mem_ln_b, w_in, ml_conv_w, ml_conv_b, ml_ig_b, ml_fg_b, ml_norm_g, rw_mu, rw_w0, rw_w_up, rw_a0, rw_a_up, rw_g_up, rw_kk, rw_ka, rw_rk, rw_ln_g, rw_ln_b, ca_w_kv, gate_b, w_br_ml, w_br_rw, w_br_ca, w_o, ln1_g, ln1_b, router_w, router_b, w_gu, b_gu, w_dn, b_dn, ln2_g, ln2_b):
    bsz, seq, d = x.shape
    mem_len = mem.shape[1]
    depth = w_in.shape[0]
    n = bsz * seq
    t = _tiles(bsz, seq)
    alpha = (2 * depth) ** 0.25
    d_ff = w_dn.shape[2]

    xf, xb = _ln_rows(x.reshape(n, d), ln_in_g, ln_in_b, t["ln"])
    _, memb = _ln_rows(mem.reshape(bsz * mem_len, d), mem_ln_g, mem_ln_b, min(t["ln"], bsz * mem_len))

    o_qk, o_v, o_og = 0, ML_QK_W, ML_QK_W + ML_W
    o_ig = o_og + ML_W
    o_fg = o_ig + ML_HEADS
    o_rw = o_fg + ML_HEADS
    o_ca = o_rw + 3 * RW_W + RW_DECAY_LORA + RW_AAA_LORA + RW_GATE_LORA
    o_gate = o_ca + CA_W

    head_of_lane = jnp.arange(RW_W) // RW_DH
    red = (head_of_lane[:, None] == jnp.arange(LANES)[None, :]).astype(BF16)
    bc = red.T
    n_asg = n * TOP_K
    block = t["moe_block"]
    group = t["moe_group"]
    n_groups = n // group
    n_chunk = _group_rows(group) // MOE_CHUNK
    assert n_chunk < CHUNK_TABLE and block % MOE_CHUNK == 0
    n_blocks = -(-(n_asg + n_groups * N_EXPERTS * (MOE_CHUNK - 1)) // block) + N_EXPERTS
    n_rows = n_blocks * block

    for l in range(depth):
        w = w_in[l]
        o_wd = o_rw + 3 * RW_W
        o_ad = o_wd + RW_DECAY_LORA
        o_gd = o_ad + RW_AAA_LORA
        wrw = jnp.concatenate([w[:, o_rw:o_wd], _pad_cols(w[:, o_wd:o_ad], LANES),
                               _pad_cols(w[:, o_ad:o_gd], LANES), w[:, o_gd:o_ca]], axis=1).astype(BF16)
        mu = rw_mu[l]
        mu_p = jnp.concatenate([mu[:3 * RW_W], jnp.pad(mu[3 * RW_W:3 * RW_W + RW_DECAY_LORA], (0, LANES - RW_DECAY_LORA)),
                                jnp.pad(mu[3 * RW_W + RW_DECAY_LORA:3 * RW_W + RW_DECAY_LORA + RW_AAA_LORA],
                                        (0, LANES - RW_AAA_LORA)),
                                mu[3 * RW_W + RW_DECAY_LORA + RW_AAA_LORA:]])[None, :]
        wup = jnp.pad(rw_w_up[l], ((0, LANES - RW_DECAY_LORA), (0, 0))).astype(BF16)
        aup = jnp.pad(rw_a_up[l], ((0, LANES - RW_AAA_LORA), (0, 0))).astype(BF16)
        scan_in, g, bonus = _rwkv_prep(
            xb.reshape(bsz, seq, d), wrw, mu_p, rw_w0[l][None, :], wup, rw_a0[l][None, :], aup,
            rw_g_up[l].astype(BF16), rw_kk[l][None, :], rw_ka[l][None, :], rw_rk[l].reshape(1, RW_W), red, bc,
            t["rwkv_prep"])

        nl = bsz * RW_HEADS
        nl_pad = -(-nl // LANES) * LANES
        ops = scan_in.reshape(SCAN_OPERANDS, seq, nl, RW_DH).transpose(0, 1, 3, 2)
        if nl_pad != nl:
            ops = jnp.pad(ops, ((0, 0), (0, 0), (0, 0), (0, nl_pad - nl)))

        wml = w[:, o_qk:o_ig].astype(BF16)
        wg = _pad_cols(w[:, o_ig:o_rw], LANES).astype(BF16)
        gb = _pad_cols(jnp.concatenate([ml_ig_b[l], ml_fg_b[l]])[None, :], LANES)
        h_ml = _mlstm(xb.reshape(bsz, seq, d), wml, wg, ml_conv_w[l], ml_conv_b[l][None, :], gb,
                      ml_norm_g[l][None, :], t["mlstm"], t["mlstm_chunk"], after=(g,))

        kv = _matmul(memb, ca_w_kv[l].astype(BF16), min(512, bsz * mem_len)).reshape(bsz, mem_len, 2 * CA_W)
        kt = kv[:, :, :CA_W].transpose(0, 2, 1).astype(BF16)
        vm = kv[:, :, CA_W:].astype(BF16)
        h_ca = _xattn(xb.reshape(bsz, seq, d), w[:, o_ca:o_gate].astype(BF16), kt, vm, t["xattn"], after=(g,))

        o_scan = _rwkv_scan(ops, t["scan_steps"], after=(h_ml, h_ca))[:, :, :nl]
        o_rwkv = o_scan.transpose(0, 2, 1).reshape(seq, bsz * RW_W)

        rw_pad = _pad_cols(router_w[l], LANES)
        rw_hi = rw_pad.astype(BF16)
        rw_lo = (rw_pad - rw_hi.astype(F32)).astype(BF16)
        rb = jnp.concatenate([router_b[l], jnp.full((LANES - N_EXPERTS,), NEG_BIG, F32)])[None, :]
        consts = (w[:, o_gate:].astype(BF16), gate_b[l][None, :], red, bc, rw_ln_g[l][None, :], rw_ln_b[l][None, :],
                  w_br_ml[l].astype(BF16), w_br_rw[l].astype(BF16), w_br_ca[l].astype(BF16), w_o[l].astype(BF16),
                  ln1_g[l][None, :], ln1_b[l][None, :], rw_hi, rw_lo, rb)
        x1, x1b, route, counts = _merge(xf, xb, h_ml.reshape(n, ML_W), o_rwkv, bonus.reshape(n, RW_W),
                                        g.reshape(n, RW_W), h_ca.reshape(n, CA_W), consts, t["merge"], group, alpha)

        cnt = counts[:, 0, :N_EXPERTS].astype(jnp.int32)
        run = (cnt + MOE_CHUNK - 1) // MOE_CHUNK * MOE_CHUNK
        run_off = jnp.cumsum(run, axis=1) - run
        group_rows = jnp.sum(run, axis=1)
        blocks_per = (jnp.sum(run, axis=0) + block - 1) // block
        blk_end = jnp.cumsum(blocks_per)
        slot_start = (blk_end - blocks_per) * block
        run_slot = slot_start[None, :] + jnp.cumsum(run, axis=0) - run
        run_off_rows = jnp.broadcast_to(
            jnp.pad(run_off, ((0, 0), (0, LANES - N_EXPERTS))).astype(F32)[:, None, :],
            (n_groups, SUBLANES, LANES))
        chunk_row = (jnp.arange(n_chunk, dtype=jnp.int32) * MOE_CHUNK)[None, :, None]
        in_run = jnp.logical_and(chunk_row >= run_off[:, None, :], chunk_row < (run_off + run)[:, None, :])
        chunk_slot = chunk_row[:, :, 0] + jnp.sum(jnp.where(in_run, (run_slot - run_off)[:, None, :], 0), axis=2)
        chunk_slot = jnp.where(chunk_row[:, :, 0] < group_rows[:, None], chunk_slot, 0)
        table = jnp.zeros((n_groups, CHUNK_TABLE), jnp.int32)
        table = table.at[:, :n_chunk].set(chunk_slot).at[:, CHUNK_TABLE - 1].set(group_rows // MOE_CHUNK)
        table = table.reshape(n_groups * CHUNK_TABLE)
        blk_ids = jnp.arange(n_blocks, dtype=jnp.int32)
        blk_e = jnp.minimum(jnp.sum((blk_ids[:, None] >= blk_end[None, :]).astype(jnp.int32), axis=1),
                            N_EXPERTS - 1)
        nused = blk_end[-1:].astype(jnp.int32)

        last_blk = jnp.maximum(blk_end - 1, 0).astype(jnp.int32)
        trailing = jnp.minimum(nused[0] + jnp.arange(n_blocks - n_asg // block, dtype=jnp.int32), n_blocks - 1)
        zero_blk = jnp.concatenate([last_blk, trailing])
        xd = _dispatch(table, zero_blk, run_off_rows, route, x1b, n_rows, group, block)
        w_glu, w_lin = _deinterleave(w_gu, l)
        y_disp = _experts(blk_e, nused, xd, w_glu, w_lin,
                          b_gu[l][:, None, 0::2], b_gu[l][:, None, 1::2], w_dn[l].astype(BF16),
                          b_dn[l][:, None, :], block)
        xf, xb = _combine(table, run_off_rows, route, x1, y_disp, ln2_g[l], ln2_b[l], group, alpha)

    del d_ff
    return xf.reshape(bsz, seq, d)
```

```python
import functools

import jax
import jax.numpy as jnp
from jax import lax
from jax.experimental import pallas as pl
from jax.experimental.pallas import tpu as pltpu

ML_HEADS, ML_DQK, ML_DV, ML_CONV = 4, 64, 128, 4
ML_W = ML_HEADS * ML_DV
ML_QK_W = 2 * ML_HEADS * ML_DQK
RW_HEADS, RW_DH = 8, 64
RW_W = RW_HEADS * RW_DH
RW_DECAY_LORA, RW_AAA_LORA, RW_GATE_LORA = 64, 64, 128
RW_GN_EPS = 64e-5
CA_HEADS, CA_DH = 4, 128
CA_W = CA_HEADS * CA_DH
N_BRANCH = 3
N_EXPERTS, TOP_K = 32, 4
SWIGLU_LIMIT, SWIGLU_ALPHA = 7.0, 1.702
LN_EPS = 1e-5

LANES = 128
SUBLANES = 8
VMEM_LIMIT = 56 * 1024 * 1024

BF16 = jnp.bfloat16
F32 = jnp.float32
NEG_BIG = -1e30


def _dot(a, b):
    return jnp.dot(a.astype(BF16), b.astype(BF16), preferred_element_type=F32)


def _split3(a):
    hi = a.astype(BF16)
    r1 = a - hi.astype(F32)
    mid = r1.astype(BF16)
    lo = (r1 - mid.astype(F32)).astype(BF16)
    return hi, mid, lo


def _dot2_rhs(a, b_bf16):
    hi = a.astype(BF16)
    lo = (a - hi.astype(F32)).astype(BF16)
    d = functools.partial(jnp.dot, preferred_element_type=F32)
    return d(hi, b_bf16) + d(lo, b_bf16)


def _head_sum(x, red, bc):
    return _dot2_rhs(_dot2_rhs(x, red), bc)


def _dot_exact_lhs(a_bf16, b):
    hi, mid, lo = _split3(b)
    d = functools.partial(jnp.dot, preferred_element_type=F32)
    return d(a_bf16, hi) + d(a_bf16, mid) + d(a_bf16, lo)


def _sigmoid(x):
    return 1.0 / (1.0 + jnp.exp(-x))


def _softplus(x):
    return jnp.maximum(x, 0.0) + jnp.log1p(jnp.exp(-jnp.abs(x)))


def _layer_norm(v, g, b, eps=LN_EPS):
    mu = jnp.mean(v, axis=-1, keepdims=True)
    c = v - mu
    var = jnp.mean(c * c, axis=-1, keepdims=True)
    return c * lax.rsqrt(var + eps) * g + b


def _params(*sem):
    return pltpu.CompilerParams(dimension_semantics=sem, vmem_limit_bytes=VMEM_LIMIT)


def _const_spec(shape):
    nd = len(shape)
    return pl.BlockSpec(shape, lambda *_: (0,) * nd, pipeline_mode=pl.Buffered(1))


def _ln_kernel(x_ref, g_ref, b_ref, o_ref, ob_ref):
    y = _layer_norm(x_ref[...], g_ref[...], b_ref[...])
    o_ref[...] = y
    ob_ref[...] = y.astype(BF16)


def _ln_rows(x2d, g, b, tile):
    n, d = x2d.shape
    return pl.pallas_call(
        _ln_kernel,
        grid=(n // tile,),
        in_specs=[pl.BlockSpec((tile, d), lambda i: (i, 0)), _const_spec((1, d)), _const_spec((1, d))],
        out_specs=[pl.BlockSpec((tile, d), lambda i: (i, 0)), pl.BlockSpec((tile, d), lambda i: (i, 0))],
        out_shape=[jax.ShapeDtypeStruct((n, d), F32), jax.ShapeDtypeStruct((n, d), BF16)],
        compiler_params=_params("parallel"),
        name="ln_rows",
    )(x2d, g.reshape(1, d), b.reshape(1, d))


def _mm_kernel(a_ref, b_ref, o_ref):
    o_ref[...] = jnp.dot(a_ref[...], b_ref[...], preferred_element_type=F32)


def _matmul(a, b, tile):
    m, k = a.shape
    n = b.shape[1]
    return pl.pallas_call(
        _mm_kernel,
        grid=(m // tile,),
        in_specs=[pl.BlockSpec((tile, k), lambda i: (i, 0)), _const_spec((k, n))],
        out_specs=pl.BlockSpec((tile, n), lambda i: (i, 0)),
        out_shape=jax.ShapeDtypeStruct((m, n), F32),
        compiler_params=_params("parallel"),
        name="matmul",
    )(a, b)


def _mlstm_kernel(xb_ref, wml_ref, wg_ref, convw_ref, convb_ref, gb_ref, normg_ref, tri_ref,
                  h_ref, ubuf, c_ref, n_ref, m_ref, *, tile, chunk):
    @pl.when(pl.program_id(1) == 0)
    def _():
        ubuf[0:SUBLANES, :] = jnp.zeros((SUBLANES, ML_QK_W), F32)
        c_ref[...] = jnp.zeros_like(c_ref)
        n_ref[...] = jnp.zeros_like(n_ref)
        m_ref[...] = jnp.zeros_like(m_ref)

    xb = xb_ref[...]
    u = jnp.dot(xb, wml_ref[...], preferred_element_type=F32)
    gates = jnp.dot(xb, wg_ref[...], preferred_element_type=F32) + gb_ref[...]

    ubuf[SUBLANES:SUBLANES + tile, :] = u[:, :ML_QK_W]
    acc = jnp.broadcast_to(convb_ref[...], (tile, ML_QK_W))
    for j in range(ML_CONV):
        acc = acc + convw_ref[j:j + 1, :] * ubuf[pl.ds(SUBLANES - ML_CONV + 1 + j, tile), :]
    ubuf[0:SUBLANES, :] = ubuf[tile:tile + SUBLANES, :]
    qk = acc * _sigmoid(acc)
    q = qk[:, :ML_HEADS * ML_DQK] * (ML_DQK ** -0.5)
    k = qk[:, ML_HEADS * ML_DQK:]
    v = u[:, ML_QK_W:ML_QK_W + ML_W]
    og = u[:, ML_QK_W + ML_W:]

    lane = lax.broadcasted_iota(jnp.int32, (chunk, LANES), 1)
    row = lax.broadcasted_iota(jnp.int32, (chunk, chunk), 0)
    col = lax.broadcasted_iota(jnp.int32, (chunk, chunk), 1)
    causal = row >= col
    log_f = -_softplus(-gates)

    for c in range(tile // chunk):
        rs = slice(c * chunk, (c + 1) * chunk)
        bcum = _dot_exact_lhs(tri_ref[...], log_f[rs])
        mcol = jnp.where(lane < ML_HEADS, gates[rs], bcum)
        mrow = mcol.T
        for h in range(ML_HEADS):
            i_col = mcol[:, h:h + 1]
            b_col = mcol[:, ML_HEADS + h:ML_HEADS + h + 1]
            i_row = mrow[h:h + 1, :]
            b_row = mrow[ML_HEADS + h:ML_HEADS + h + 1, :]
            m_prev = m_ref[h][0:1, 0:1]
            n_prev = n_ref[h][0:1, :]
            c_prev = c_ref[h]
            qh = q[rs, h * ML_DQK:(h + 1) * ML_DQK]
            kh = k[rs, h * ML_DQK:(h + 1) * ML_DQK]
            vh = v[rs, h * ML_DV:(h + 1) * ML_DV]

            dmat = jnp.where(causal, b_col - b_row + i_row, -jnp.inf)
            m_inter = b_col + m_prev
            m_t = jnp.maximum(m_inter, jnp.max(dmat, axis=1, keepdims=True))
            s = lax.dot_general(qh.astype(BF16), kh.astype(BF16), (((1,), (1,)), ((), ())),
                                preferred_element_type=F32)
            w_intra = jnp.exp(dmat - m_t) * s
            s_inter = jnp.exp(m_inter - m_t)
            num = s_inter * _dot(qh, c_prev) + _dot(w_intra, vh)
            den = (s_inter * jnp.sum(qh * n_prev, axis=1, keepdims=True)
                   + jnp.sum(w_intra, axis=1, keepdims=True))
            hh = num / jnp.maximum(jnp.abs(den), jnp.exp(-m_t))

            b_last = b_col[chunk - 1:chunk, :]
            g_col = b_last - b_col + i_col
            m_new = jnp.maximum(b_last + m_prev, jnp.max(g_col, axis=0, keepdims=True))
            carry = jnp.exp(b_last + m_prev - m_new)
            kw = kh * jnp.exp(g_col - m_new)
            c_ref[h] = carry * c_prev + lax.dot_general(
                kw.astype(BF16), vh.astype(BF16), (((0,), (0,)), ((), ())), preferred_element_type=F32)
            n_new = carry * n_prev + jnp.sum(kw, axis=0, keepdims=True)
            n_ref[h] = jnp.broadcast_to(n_new, (SUBLANES, ML_DQK))
            m_ref[h] = jnp.broadcast_to(m_new, (SUBLANES, LANES))

            mu = jnp.mean(hh, axis=1, keepdims=True)
            cen = hh - mu
            var = jnp.mean(cen * cen, axis=1, keepdims=True)
            y = cen * lax.rsqrt(var + LN_EPS) * normg_ref[:, h * ML_DV:(h + 1) * ML_DV]
            h_ref[rs, h * ML_DV:(h + 1) * ML_DV] = (
                _sigmoid(og[rs, h * ML_DV:(h + 1) * ML_DV]) * y).astype(BF16)


def _mlstm(xb, wml, wg, convw, convb, gb, normg, tile, chunk):
    bsz, s, d = xb.shape
    tri = (jnp.arange(chunk)[:, None] >= jnp.arange(chunk)[None, :]).astype(BF16)
    kern = functools.partial(_mlstm_kernel, tile=tile, chunk=chunk)
    return pl.pallas_call(
        kern,
        grid=(bsz, s // tile),
        in_specs=[pl.BlockSpec((None, tile, d), lambda b, j: (b, j, 0)),
                  _const_spec(wml.shape), _const_spec(wg.shape), _const_spec(convw.shape),
                  _const_spec(convb.shape), _const_spec(gb.shape), _const_spec(normg.shape),
                  _const_spec(tri.shape)],
        out_specs=pl.BlockSpec((None, tile, ML_W), lambda b, j: (b, j, 0)),
        out_shape=jax.ShapeDtypeStruct((bsz, s, ML_W), BF16),
        scratch_shapes=[pltpu.VMEM((tile + SUBLANES, ML_QK_W), F32),
                        pltpu.VMEM((ML_HEADS, ML_DQK, ML_DV), F32),
                        pltpu.VMEM((ML_HEADS, SUBLANES, ML_DQK), F32),
                        pltpu.VMEM((ML_HEADS, SUBLANES, LANES), F32)],
        compiler_params=_params("parallel", "arbitrary"),
        name="mlstm",
    )(xb, wml, wg, convw, convb, gb, normg, tri)


RW_PAD_IN = 3 * RW_W + 3 * LANES
SCAN_R, SCAN_W, SCAN_K, SCAN_V, SCAN_KK, SCAN_KKA = range(6)
SCAN_OPERANDS = 6


def _rwkv_prep_kernel(xb_ref, wrw_ref, mu_ref, w0_ref, wup_ref, a0_ref, aup_ref, gup_ref, kkw_ref,
                      ka_ref, rk_ref, red_ref, bc_ref,
                      scan_ref, g_ref, bonus_ref, ubuf, *, tile):
    @pl.when(pl.program_id(1) == 0)
    def _():
        ubuf[0:SUBLANES, :] = jnp.zeros((SUBLANES, RW_PAD_IN), F32)

    u = jnp.dot(xb_ref[...], wrw_ref[...], preferred_element_type=F32)
    ubuf[SUBLANES:SUBLANES + tile, :] = u
    u_prev = ubuf[pl.ds(SUBLANES - 1, tile), :]
    ubuf[0:SUBLANES, :] = ubuf[tile:tile + SUBLANES, :]
    us = u + (u_prev - u) * mu_ref[...]
    r = us[:, 0:RW_W]
    kr = us[:, RW_W:2 * RW_W]
    vr = us[:, 2 * RW_W:3 * RW_W]
    wd = us[:, 3 * RW_W:3 * RW_W + LANES]
    ad = us[:, 3 * RW_W + LANES:3 * RW_W + 2 * LANES]
    gd = us[:, 3 * RW_W + 2 * LANES:]

    w_log = -_softplus(-(w0_ref[...] + _dot(jnp.tanh(wd), wup_ref[...]))) - 0.5
    decay = jnp.exp(-jnp.exp(w_log))
    a = _sigmoid(a0_ref[...] + _dot(ad, aup_ref[...]))
    g = _dot(_sigmoid(gd), gup_ref[...])

    red, bc = red_ref[...], bc_ref[...]
    kk = kr * kkw_ref[...]
    kk = kk * lax.rsqrt(jnp.maximum(_head_sum(kk * kk, red, bc), 1e-24))
    k2 = kr * (1.0 + (a - 1.0) * ka_ref[...])
    bonus = _head_sum(r * k2 * rk_ref[...], red, bc) * vr

    scan_ref[SCAN_R] = r
    scan_ref[SCAN_W] = decay
    scan_ref[SCAN_K] = k2
    scan_ref[SCAN_V] = vr
    scan_ref[SCAN_KK] = kk
    scan_ref[SCAN_KKA] = kk * a
    g_ref[...] = g
    bonus_ref[...] = bonus


def _rwkv_prep(xb, wrw, mu, w0, wup, a0, aup, gup, kkw, ka, rk, red, bc, tile):
    bsz, s, d = xb.shape
    consts = (wrw, mu, w0, wup, a0, aup, gup, kkw, ka, rk, red, bc)
    spec = pl.BlockSpec((None, tile, RW_W), lambda b, j: (b, j, 0))
    tspec = pl.BlockSpec((SCAN_OPERANDS, tile, RW_W), lambda b, j: (0, j, b))
    return pl.pallas_call(
        functools.partial(_rwkv_prep_kernel, tile=tile),
        grid=(bsz, s // tile),
        in_specs=[pl.BlockSpec((None, tile, d), lambda b, j: (b, j, 0))] + [_const_spec(c.shape) for c in consts],
        out_specs=[tspec, spec, spec],
        out_shape=[jax.ShapeDtypeStruct((SCAN_OPERANDS, s, bsz * RW_W), F32)]
        + [jax.ShapeDtypeStruct((bsz, s, RW_W), F32)] * 2,
        scratch_shapes=[pltpu.VMEM((tile + SUBLANES, RW_PAD_IN), F32)],
        compiler_params=_params("parallel", "arbitrary"),
        name="rwkv_prep",
    )(xb, *consts)


def _rwkv_scan_kernel(x_ref, o_ref, st_ref, *, steps):
    @pl.when(pl.program_id(1) == 0)
    def _():
        st_ref[...] = jnp.zeros_like(st_ref)

    r_ref, w_ref, k_ref, v_ref, kk_ref, kka_ref = (
        x_ref.at[i] for i in (SCAN_R, SCAN_W, SCAN_K, SCAN_V, SCAN_KK, SCAN_KKA))
    zeros = jnp.zeros((RW_DH, LANES), F32)

    def first(kc, sa):
        return sa + st_ref[kc] * kk_ref[0, pl.ds(kc, 1), :]

    sa0 = lax.fori_loop(0, RW_DH, first, zeros, unroll=8)

    def step(t, sa):
        vt = v_ref[t]
        tn = jnp.minimum(t + 1, steps - 1)

        def body(kc, carry):
            out, sa_next = carry
            row = pl.ds(kc, 1)
            new = st_ref[kc] * w_ref[t, row, :] - sa * kka_ref[t, row, :] + vt * k_ref[t, row, :]
            st_ref[kc] = new
            return out + new * r_ref[t, row, :], sa_next + new * kk_ref[tn, row, :]

        out, sa_next = lax.fori_loop(0, RW_DH, body, (zeros, zeros), unroll=16)
        o_ref[t] = out
        return sa_next

    lax.fori_loop(0, steps, step, sa0)


def _rwkv_scan(x, steps):
    _, s, dh, nl = x.shape
    spec = pl.BlockSpec((steps, dh, LANES), lambda g, j: (j, 0, g))
    return pl.pallas_call(
        functools.partial(_rwkv_scan_kernel, steps=steps),
        grid=(nl // LANES, s // steps),
        in_specs=[pl.BlockSpec((SCAN_OPERANDS, steps, dh, LANES), lambda g, j: (0, j, 0, g))],
        out_specs=spec,
        out_shape=jax.ShapeDtypeStruct((s, dh, nl), F32),
        scratch_shapes=[pltpu.VMEM((dh, dh, LANES), F32)],
        compiler_params=_params("parallel", "arbitrary"),
        name="rwkv_scan",
    )(x)


def _xattn_kernel(xb_ref, wq_ref, kt_ref, v_ref, o_ref):
    q = jnp.dot(xb_ref[...], wq_ref[...], preferred_element_type=F32)
    for h in range(CA_HEADS):
        hs = slice(h * CA_DH, (h + 1) * CA_DH)
        s = jnp.dot(q[:, hs].astype(BF16), kt_ref[hs, :], preferred_element_type=F32) * (CA_DH ** -0.5)
        p = jnp.exp(s - jnp.max(s, axis=1, keepdims=True))
        den = jnp.sum(p, axis=1, keepdims=True)
        o = jnp.dot(p.astype(BF16), v_ref[:, hs], preferred_element_type=F32) / den
        o_ref[:, hs] = o.astype(BF16)


def _xattn(xb, wq, kt, v, tile):
    bsz, s, d = xb.shape
    m = v.shape[1]
    return pl.pallas_call(
        _xattn_kernel,
        grid=(bsz, s // tile),
        in_specs=[pl.BlockSpec((None, tile, d), lambda b, j: (b, j, 0)), _const_spec(wq.shape),
                  pl.BlockSpec((None, CA_W, m), lambda b, j: (b, 0, 0)),
                  pl.BlockSpec((None, m, CA_W), lambda b, j: (b, 0, 0))],
        out_specs=pl.BlockSpec((None, tile, CA_W), lambda b, j: (b, j, 0)),
        out_shape=jax.ShapeDtypeStruct((bsz, s, CA_W), BF16),
        compiler_params=_params("parallel", "parallel"),
        name="xattn",
    )(xb, wq, kt, v)


ROUTE_IDX, ROUTE_POS, ROUTE_GATE = 0, TOP_K, 2 * TOP_K


def _merge_kernel(x_ref, xb_ref, hml_ref, o_ref, bonus_ref, g_ref, hca_ref,
                  wgate_ref, gateb_ref, red_ref, bc_ref, rwg_ref, rwb_ref, wml_ref, wrw_ref, wca_ref, wo_ref,
                  lng_ref, lnb_ref, rwhi_ref, rwlo_ref, rb_ref, tri_ref, upper_ref,
                  x1_ref, x1b_ref, route_ref, cnt_ref, *, tile, group, alpha, d_model):
    red, bc = red_ref[...], bc_ref[...]
    o = o_ref[...]
    mu = _head_sum(o, red, bc) * (1.0 / RW_DH)
    cen = o - mu
    var = _head_sum(cen * cen, red, bc) * (1.0 / RW_DH)
    h_rw = (cen * lax.rsqrt(var + RW_GN_EPS) * rwg_ref[...] + rwb_ref[...] + bonus_ref[...]) * g_ref[...]

    xb = xb_ref[...]
    y = None
    for br, (h_br, w_ref) in enumerate(((hml_ref[...], wml_ref), (h_rw.astype(BF16), wrw_ref),
                                        (hca_ref[...], wca_ref))):
        cols = slice(br * d_model, (br + 1) * d_model)
        gate = _sigmoid(jnp.dot(xb, wgate_ref[:, cols], preferred_element_type=F32) + gateb_ref[:, cols])
        term = gate * jnp.dot(h_br, w_ref[...], preferred_element_type=F32)
        y = term if y is None else y + term
    mixed = _dot(y, wo_ref[...])
    x1 = _layer_norm(alpha * x_ref[...] + mixed, lng_ref[...], lnb_ref[...])
    x1_ref[...] = x1
    x1b_ref[...] = x1.astype(BF16)

    hi = x1.astype(BF16)
    lo = (x1 - hi.astype(F32)).astype(BF16)
    d = functools.partial(jnp.dot, preferred_element_type=F32)
    logits = d(hi, rwhi_ref[...]) + d(lo, rwhi_ref[...]) + d(hi, rwlo_ref[...]) + rb_ref[...]

    lane = lax.broadcasted_iota(jnp.int32, (tile, LANES), 1)
    vals = logits
    tops, onehots = [], []
    for _ in range(TOP_K):
        m = jnp.max(vals, axis=1, keepdims=True)
        idx = jnp.min(jnp.where(vals == m, lane, LANES), axis=1, keepdims=True)
        sel = lane == idx
        tops.append((m, idx))
        onehots.append(sel.astype(F32))
        vals = jnp.where(sel, -jnp.inf, vals)
    exps = [jnp.exp(m - tops[0][0]) for m, _ in tops]
    den = exps[0] + exps[1] + exps[2] + exps[3]

    cnt = onehots[0] + onehots[1] + onehots[2] + onehots[3]
    starts = []
    for gi in range(tile // group):
        cg = cnt[gi * group:(gi + 1) * group]
        total = jnp.sum(cg, axis=0, keepdims=True)
        run = jnp.floor((total + (MOE_CHUNK - 1)) * (1.0 / MOE_CHUNK)) * MOE_CHUNK
        run_off = jnp.dot(jnp.broadcast_to(run, (SUBLANES, LANES)).astype(BF16), upper_ref[...],
                          preferred_element_type=F32)[0:1, :]
        starts.append(jnp.dot(tri_ref[...], cg.astype(BF16), preferred_element_type=F32) + run_off)
        cnt_ref[gi] = jnp.broadcast_to(total, (SUBLANES, LANES))
    start = starts[0] if len(starts) == 1 else jnp.concatenate(starts, axis=0)
    route = jnp.zeros((tile, LANES), F32)
    for kq in range(TOP_K):
        pos = jnp.sum(onehots[kq] * start, axis=1, keepdims=True)
        route = jnp.where(lane == ROUTE_IDX + kq, tops[kq][1].astype(F32), route)
        route = jnp.where(lane == ROUTE_POS + kq, pos, route)
        route = jnp.where(lane == ROUTE_GATE + kq, exps[kq] / den, route)
    route_ref[...] = route


def _merge(x, xb, hml, o_tm, bonus, g, hca, consts, tile, group, alpha):
    n, d = x.shape
    tiles_per_seq = o_tm.shape[0] // tile
    gpt = tile // group
    tri = (jnp.arange(group)[:, None] > jnp.arange(group)[None, :]).astype(BF16)
    upper = (jnp.arange(LANES)[:, None] < jnp.arange(LANES)[None, :]).astype(BF16)
    consts = tuple(consts) + (tri, upper)

    def rows(w):
        return pl.BlockSpec((tile, w), lambda i: (i, 0))

    o_spec = pl.BlockSpec((tile, RW_W), lambda i: (i % tiles_per_seq, i // tiles_per_seq))
    return pl.pallas_call(
        functools.partial(_merge_kernel, tile=tile, group=group, alpha=alpha, d_model=d),
        grid=(n // tile,),
        in_specs=[rows(d), rows(d), rows(ML_W), o_spec, rows(RW_W), rows(RW_W), rows(CA_W)]
        + [_const_spec(c.shape) for c in consts],
        out_specs=[rows(d), rows(d), rows(LANES),
                   pl.BlockSpec((gpt, SUBLANES, LANES), lambda i: (i, 0, 0))],
        out_shape=[jax.ShapeDtypeStruct((n, d), F32), jax.ShapeDtypeStruct((n, d), BF16),
                   jax.ShapeDtypeStruct((n, LANES), F32),
                   jax.ShapeDtypeStruct((n // group, SUBLANES, LANES), F32)],
        compiler_params=_params("parallel"),
        name="merge_route",
    )(x, xb, hml, o_tm, bonus, g, hca, *consts)


MOE_CHUNK = SUBLANES
CHUNK_TABLE = 1024


def _group_rows(group):
    return -(-(group * TOP_K + N_EXPERTS * (MOE_CHUNK - 1)) // LANES) * LANES


def _dispatch_kernel(tab_ref, zero_blk_ref, route_ref, xb_ref, xd_ref, xs_buf, zbuf, live_ref, sems, zsem,
                     *, group, nsteps, block, prows):
    i = pl.program_id(0)
    slot = i % 2

    @pl.when(i == 0)
    def _():
        zbuf[...] = jnp.zeros_like(zbuf)
        n_zero = zero_blk_ref.shape[0]

        def fresh(j):
            return jnp.logical_or(j == 0, zero_blk_ref[j] != zero_blk_ref[jnp.maximum(j - 1, 0)])

        def zero_copy(j):
            return pltpu.make_async_copy(zbuf, xd_ref.at[pl.ds(zero_blk_ref[j] * block, block)], zsem)

        def start(j, carry):
            @pl.when(fresh(j))
            def _():
                zero_copy(j).start()
            return carry

        def wait(j, carry):
            @pl.when(fresh(j))
            def _():
                zero_copy(j).wait()
            return carry

        lax.fori_loop(0, n_zero, start, 0)
        lax.fori_loop(0, n_zero, wait, 0)

    def chunk_copy(c, s):
        return pltpu.make_async_copy(xs_buf.at[s, pl.ds(c * MOE_CHUNK, MOE_CHUNK)],
                                     xd_ref.at[pl.ds(pl.multiple_of(tab_ref[c], MOE_CHUNK), MOE_CHUNK)],
                                     sems.at[s])

    def retire(s):
        def wait(c, carry):
            chunk_copy(0, s).wait()
            return carry

        lax.fori_loop(0, live_ref[s], wait, 0)

    @pl.when(i >= 2)
    def _():
        retire(slot)

    route_t = route_ref[...].T
    rows = lax.broadcasted_iota(jnp.int32, (prows, group), 0).astype(F32)
    pick = jnp.zeros((prows, group), F32)
    for kq in range(TOP_K):
        pick = pick + jnp.where(rows == route_t[ROUTE_POS + kq:ROUTE_POS + kq + 1, :], 1.0, 0.0)
    xs_buf[slot] = jnp.dot(pick.astype(BF16), xb_ref[...], preferred_element_type=F32)

    live = tab_ref[CHUNK_TABLE - 1]
    live_ref[slot] = live

    def issue(c, carry):
        chunk_copy(c, slot).start()
        return carry

    lax.fori_loop(0, live, issue, 0)

    @pl.when(i == nsteps - 1)
    def _():
        retire(slot)
        if nsteps > 1:
            retire(1 - slot)


def _dispatch(table, zero_blk, route, x1b, n_rows, group, block):
    n, d = x1b.shape
    nsteps = n // group
    prows = _group_rows(group)
    return pl.pallas_call(
        functools.partial(_dispatch_kernel, group=group, nsteps=nsteps, block=block, prows=prows),
        grid=(nsteps,),
        in_specs=[pl.BlockSpec((CHUNK_TABLE,), lambda i: (i,), memory_space=pltpu.SMEM),
                  pl.BlockSpec(memory_space=pltpu.SMEM),
                  pl.BlockSpec((group, LANES), lambda i: (i, 0)),
                  pl.BlockSpec((group, d), lambda i: (i, 0))],
        out_specs=pl.BlockSpec(memory_space=pl.ANY),
        out_shape=jax.ShapeDtypeStruct((n_rows, d), F32),
        scratch_shapes=[pltpu.VMEM((2, prows, d), F32), pltpu.VMEM((block, d), F32),
                        pltpu.SMEM((2,), jnp.int32),
                        pltpu.SemaphoreType.DMA((2,)), pltpu.SemaphoreType.DMA],
        compiler_params=_params("arbitrary"),
        name="moe_dispatch",
    )(table, zero_blk, route, x1b)


DEINT_COLS = 512


def _deinterleave_kernel(w_ref, pe_ref, po_ref, glu_ref, lin_ref):
    wb = w_ref[...].astype(BF16)
    glu_ref[...] = jnp.dot(wb, pe_ref[...], preferred_element_type=F32).astype(BF16)
    lin_ref[...] = jnp.dot(wb, po_ref[...], preferred_element_type=F32).astype(BF16)


def _deinterleave(w_gu_all, layer):
    _, e, d, two_ff = w_gu_all.shape
    half = DEINT_COLS // 2
    src = jnp.arange(DEINT_COLS)[:, None]
    dst = jnp.arange(half)[None, :]
    pe = (src == 2 * dst).astype(BF16)
    po = (src == 2 * dst + 1).astype(BF16)
    out = jax.ShapeDtypeStruct((e, d, two_ff // 2), BF16)
    return pl.pallas_call(
        _deinterleave_kernel,
        grid=(e, two_ff // DEINT_COLS),
        in_specs=[pl.BlockSpec((None, None, d, DEINT_COLS), lambda i, c: (layer, i, 0, c)),
                  _const_spec(pe.shape), _const_spec(po.shape)],
        out_specs=[pl.BlockSpec((None, d, half), lambda i, c: (i, 0, c))] * 2,
        out_shape=[out, out],
        compiler_params=_params("parallel", "parallel"),
        name="deinterleave_w",
    )(w_gu_all, pe, po)


def _expert_kernel(blk_e_ref, nused_ref, xd_ref, wglu_ref, wlin_ref, bglu_ref, blin_ref, wdn_ref, bdn_ref, y_ref):
    del blk_e_ref
    live = pl.program_id(0) < nused_ref[0]

    @pl.when(live)
    def _():
        xb = xd_ref[...].astype(BF16)
        h_glu = jnp.dot(xb, wglu_ref[...], preferred_element_type=F32) + bglu_ref[...]
        h_lin = jnp.dot(xb, wlin_ref[...], preferred_element_type=F32) + blin_ref[...]
        x_glu = jnp.minimum(h_glu, SWIGLU_LIMIT)
        x_lin = jnp.clip(h_lin, -SWIGLU_LIMIT, SWIGLU_LIMIT)
        act = x_glu * _sigmoid(SWIGLU_ALPHA * x_glu) * (x_lin + 1.0)
        y_ref[...] = _dot(act, wdn_ref[...]) + bdn_ref[...]

    @pl.when(jnp.logical_not(live))
    def _():
        y_ref[...] = jnp.zeros_like(y_ref)


def _experts(blk_e, nused, xd, wglu, wlin, bglu, blin, wdn, bdn, block):
    n_rows, d = xd.shape
    dff = wglu.shape[2]
    grid_spec = pltpu.PrefetchScalarGridSpec(
        num_scalar_prefetch=2,
        grid=(n_rows // block,),
        in_specs=[pl.BlockSpec((block, d), lambda i, be, nu: (jnp.minimum(i, nu[0] - 1), 0)),
                  pl.BlockSpec((None, d, dff), lambda i, be, nu: (be[i], 0, 0)),
                  pl.BlockSpec((None, d, dff), lambda i, be, nu: (be[i], 0, 0)),
                  pl.BlockSpec((None, 1, dff), lambda i, be, nu: (be[i], 0, 0)),
                  pl.BlockSpec((None, 1, dff), lambda i, be, nu: (be[i], 0, 0)),
                  pl.BlockSpec((None, dff, d), lambda i, be, nu: (be[i], 0, 0)),
                  pl.BlockSpec((None, 1, d), lambda i, be, nu: (be[i], 0, 0))],
        out_specs=pl.BlockSpec((block, d), lambda i, be, nu: (i, 0)),
    )
    return pl.pallas_call(
        _expert_kernel,
        grid_spec=grid_spec,
        out_shape=jax.ShapeDtypeStruct((n_rows, d), F32),
        compiler_params=_params("arbitrary"),
        name="moe_experts",
    )(blk_e, nused, xd, wglu, wlin, bglu, blin, wdn, bdn)


def _combine_kernel(tab_ref, tab_next_ref, route_ref, x1_ref, y_ref, lng_ref, lnb_ref, x2_ref, x2b_ref,
                    ybuf, sems, *, group, alpha, nsteps, prows):
    i = pl.program_id(0)
    slot = i % 2
    nchunk = prows // MOE_CHUNK

    def fetch(t_ref, s):
        def issue(c, carry):
            pltpu.make_async_copy(y_ref.at[pl.ds(pl.multiple_of(t_ref[c], MOE_CHUNK), MOE_CHUNK)],
                                  ybuf.at[s, pl.ds(c * MOE_CHUNK, MOE_CHUNK)], sems.at[s]).start()
            return carry

        lax.fori_loop(0, nchunk, issue, 0, unroll=4)

    @pl.when(i == 0)
    def _():
        fetch(tab_ref, slot)

    @pl.when(i + 1 < nsteps)
    def _():
        fetch(tab_next_ref, 1 - slot)

    pltpu.make_async_copy(y_ref.at[pl.ds(0, prows)], ybuf.at[slot], sems.at[slot]).wait()
    route = route_ref[...]
    cols = lax.broadcasted_iota(jnp.int32, (group, prows), 1).astype(F32)
    weights = jnp.zeros((group, prows), F32)
    for kq in range(TOP_K):
        weights = weights + jnp.where(cols == route[:, ROUTE_POS + kq:ROUTE_POS + kq + 1],
                                      route[:, ROUTE_GATE + kq:ROUTE_GATE + kq + 1], 0.0)
    moe = jnp.dot(weights.astype(BF16), ybuf[slot].astype(BF16), preferred_element_type=F32)
    x2 = _layer_norm(alpha * x1_ref[...] + moe, lng_ref[...], lnb_ref[...])
    x2_ref[...] = x2
    x2b_ref[...] = x2.astype(BF16)


def _combine(table, route, x1, y_disp, lng, lnb, group, alpha):
    n, d = x1.shape
    nsteps = n // group
    prows = _group_rows(group)
    return pl.pallas_call(
        functools.partial(_combine_kernel, group=group, alpha=alpha, nsteps=nsteps, prows=prows),
        grid=(nsteps,),
        in_specs=[pl.BlockSpec((CHUNK_TABLE,), lambda i: (i,), memory_space=pltpu.SMEM),
                  pl.BlockSpec((CHUNK_TABLE,), lambda i: (jnp.minimum(i + 1, nsteps - 1),),
                               memory_space=pltpu.SMEM),
                  pl.BlockSpec((group, LANES), lambda i: (i, 0)),
                  pl.BlockSpec((group, d), lambda i: (i, 0)),
                  pl.BlockSpec(memory_space=pl.ANY),
                  _const_spec((1, d)), _const_spec((1, d))],
        out_specs=[pl.BlockSpec((group, d), lambda i: (i, 0)), pl.BlockSpec((group, d), lambda i: (i, 0))],
        out_shape=[jax.ShapeDtypeStruct((n, d), F32), jax.ShapeDtypeStruct((n, d), BF16)],
        scratch_shapes=[pltpu.VMEM((2, prows, d), F32), pltpu.SemaphoreType.DMA((2,))],
        compiler_params=_params("arbitrary"),
        name="moe_combine",
    )(table, table, route, x1, y_disp, lng.reshape(1, d), lnb.reshape(1, d))


def _tiles(bsz, seq):
    n = bsz * seq
    return dict(
        ln=min(1024, n),
        mlstm=min(512, seq), mlstm_chunk=min(128, seq),
        rwkv_prep=min(512, seq),
        scan_steps=min(32, seq),
        xattn=min(512, seq),
        merge=min(512, seq),
        moe_group=min(256, seq),
        moe_block=512,
    )


def _pad_cols(w, width):
    return jnp.pad(w, ((0, 0), (0, width - w.shape[1])))


def kernel(x, mem, ln_in_g, ln_in_b, mem_ln_g, mem_ln_b, w_in, ml_conv_w, ml_conv_b, ml_ig_b, ml_fg_b, ml_norm_g, rw_mu, rw_w0, rw_w_up, rw_a0, rw_a_up, rw_g_up, rw_kk, rw_ka, rw_rk, rw_ln_g, rw_ln_b, ca_w_kv, gate_b, w_br_ml, w_br_rw, w_br_ca, w_o, ln1_g, ln1_b, router_w, router_b, w_gu, b_gu, w_dn, b_dn, ln2_g, ln2_b):
    bsz, seq, d = x.shape
    mem_len = mem.shape[1]
    depth = w_in.shape[0]
    n = bsz * seq
    t = _tiles(bsz, seq)
    alpha = (2 * depth) ** 0.25
    d_ff = w_dn.shape[2]

    xf, xb = _ln_rows(x.reshape(n, d), ln_in_g, ln_in_b, t["ln"])
    _, memb = _ln_rows(mem.reshape(bsz * mem_len, d), mem_ln_g, mem_ln_b, min(t["ln"], bsz * mem_len))

    o_qk, o_v, o_og = 0, ML_QK_W, ML_QK_W + ML_W
    o_ig = o_og + ML_W
    o_fg = o_ig + ML_HEADS
    o_rw = o_fg + ML_HEADS
    o_ca = o_rw + 3 * RW_W + RW_DECAY_LORA + RW_AAA_LORA + RW_GATE_LORA
    o_gate = o_ca + CA_W

    head_of_lane = jnp.arange(RW_W) // RW_DH
    red = (head_of_lane[:, None] == jnp.arange(LANES)[None, :]).astype(BF16)
    bc = red.T
    n_asg = n * TOP_K
    block = t["moe_block"]
    group = t["moe_group"]
    n_groups = n // group
    n_chunk = _group_rows(group) // MOE_CHUNK
    assert n_chunk < CHUNK_TABLE and block % MOE_CHUNK == 0
    n_blocks = -(-(n_asg + n_groups * N_EXPERTS * (MOE_CHUNK - 1)) // block) + N_EXPERTS
    n_rows = n_blocks * block

    for l in range(depth):
        w = w_in[l]
        o_wd = o_rw + 3 * RW_W
        o_ad = o_wd + RW_DECAY_LORA
        o_gd = o_ad + RW_AAA_LORA
        wrw = jnp.concatenate([w[:, o_rw:o_wd], _pad_cols(w[:, o_wd:o_ad], LANES),
                               _pad_cols(w[:, o_ad:o_gd], LANES), w[:, o_gd:o_ca]], axis=1).astype(BF16)
        mu = rw_mu[l]
        mu_p = jnp.concatenate([mu[:3 * RW_W], jnp.pad(mu[3 * RW_W:3 * RW_W + RW_DECAY_LORA], (0, LANES - RW_DECAY_LORA)),
                                jnp.pad(mu[3 * RW_W + RW_DECAY_LORA:3 * RW_W + RW_DECAY_LORA + RW_AAA_LORA],
                                        (0, LANES - RW_AAA_LORA)),
                                mu[3 * RW_W + RW_DECAY_LORA + RW_AAA_LORA:]])[None, :]
        wup = jnp.pad(rw_w_up[l], ((0, LANES - RW_DECAY_LORA), (0, 0))).astype(BF16)
        aup = jnp.pad(rw_a_up[l], ((0, LANES - RW_AAA_LORA), (0, 0))).astype(BF16)
        scan_in, g, bonus = _rwkv_prep(
            xb.reshape(bsz, seq, d), wrw, mu_p, rw_w0[l][None, :], wup, rw_a0[l][None, :], aup,
            rw_g_up[l].astype(BF16), rw_kk[l][None, :], rw_ka[l][None, :], rw_rk[l].reshape(1, RW_W), red, bc,
            t["rwkv_prep"])

        nl = bsz * RW_HEADS
        nl_pad = -(-nl // LANES) * LANES
        ops = scan_in.reshape(SCAN_OPERANDS, seq, nl, RW_DH).transpose(0, 1, 3, 2)
        if nl_pad != nl:
            ops = jnp.pad(ops, ((0, 0), (0, 0), (0, 0), (0, nl_pad - nl)))

        wml = w[:, o_qk:o_ig].astype(BF16)
        wg = _pad_cols(w[:, o_ig:o_rw], LANES).astype(BF16)
        gb = _pad_cols(jnp.concatenate([ml_ig_b[l], ml_fg_b[l]])[None, :], LANES)
        h_ml = _mlstm(xb.reshape(bsz, seq, d), wml, wg, ml_conv_w[l], ml_conv_b[l][None, :], gb,
                      ml_norm_g[l][None, :], t["mlstm"], t["mlstm_chunk"])

        kv = _matmul(memb, ca_w_kv[l].astype(BF16), min(512, bsz * mem_len)).reshape(bsz, mem_len, 2 * CA_W)
        kt = kv[:, :, :CA_W].transpose(0, 2, 1).astype(BF16)
        vm = kv[:, :, CA_W:].astype(BF16)
        h_ca = _xattn(xb.reshape(bsz, seq, d), w[:, o_ca:o_gate].astype(BF16), kt, vm, t["xattn"])

        o_scan = _rwkv_scan(ops, t["scan_steps"])[:, :, :nl]
        o_rwkv = o_scan.transpose(0, 2, 1).reshape(seq, bsz * RW_W)

        rw_pad = _pad_cols(router_w[l], LANES)
        rw_hi = rw_pad.astype(BF16)
        rw_lo = (rw_pad - rw_hi.astype(F32)).astype(BF16)
        rb = jnp.concatenate([router_b[l], jnp.full((LANES - N_EXPERTS,), NEG_BIG, F32)])[None, :]
        consts = (w[:, o_gate:].astype(BF16), gate_b[l][None, :], red, bc, rw_ln_g[l][None, :], rw_ln_b[l][None, :],
                  w_br_ml[l].astype(BF16), w_br_rw[l].astype(BF16), w_br_ca[l].astype(BF16), w_o[l].astype(BF16),
                  ln1_g[l][None, :], ln1_b[l][None, :], rw_hi, rw_lo, rb)
        x1, x1b, route, counts = _merge(xf, xb, h_ml.reshape(n, ML_W), o_rwkv, bonus.reshape(n, RW_W),
                                        g.reshape(n, RW_W), h_ca.reshape(n, CA_W), consts, t["merge"], group, alpha)

        cnt = counts[:, 0, :N_EXPERTS].astype(jnp.int32)
        run = (cnt + MOE_CHUNK - 1) // MOE_CHUNK * MOE_CHUNK
        run_off = jnp.cumsum(run, axis=1) - run
        group_rows = jnp.sum(run, axis=1)
        blocks_per = (jnp.sum(run, axis=0) + block - 1) // block
        blk_end = jnp.cumsum(blocks_per)
        slot_start = (blk_end - blocks_per) * block
        run_slot = slot_start[None, :] + jnp.cumsum(run, axis=0) - run
        chunk_row = (jnp.arange(n_chunk, dtype=jnp.int32) * MOE_CHUNK)[None, :, None]
        in_run = jnp.logical_and(chunk_row >= run_off[:, None, :], chunk_row < (run_off + run)[:, None, :])
        chunk_slot = chunk_row[:, :, 0] + jnp.sum(jnp.where(in_run, (run_slot - run_off)[:, None, :], 0), axis=2)
        chunk_slot = jnp.where(chunk_row[:, :, 0] < group_rows[:, None], chunk_slot, 0)
        table = jnp.zeros((n_groups, CHUNK_TABLE), jnp.int32)
        table = table.at[:, :n_chunk].set(chunk_slot).at[:, CHUNK_TABLE - 1].set(group_rows // MOE_CHUNK)
        table = table.reshape(n_groups * CHUNK_TABLE)
        blk_ids = jnp.arange(n_blocks, dtype=jnp.int32)
        blk_e = jnp.minimum(jnp.sum((blk_ids[:, None] >= blk_end[None, :]).astype(jnp.int32), axis=1),
                            N_EXPERTS - 1)
        nused = blk_end[-1:].astype(jnp.int32)

        last_blk = jnp.maximum(blk_end - 1, 0).astype(jnp.int32)
        trailing = jnp.minimum(nused[0] + jnp.arange(n_blocks - n_asg // block, dtype=jnp.int32), n_blocks - 1)
        zero_blk = jnp.concatenate([last_blk, trailing])
        xd = _dispatch(table, zero_blk, route, x1b, n_rows, group, block)
        w_glu, w_lin = _deinterleave(w_gu, l)
        y_disp = _experts(blk_e, nused, xd, w_glu, w_lin,
                          b_gu[l][:, None, 0::2], b_gu[l][:, None, 1::2], w_dn[l].astype(BF16),
                          b_dn[l][:, None, :], block)
        xf, xb = _combine(table, route, x1, y_disp, ln2_g[l], ln2_b[l], group, alpha)

    del d_ff
    return xf.reshape(bsz, seq, d)
```

```python
import functools

import jax
import jax.numpy as jnp
from jax import lax
from jax.experimental import pallas as pl
from jax.experimental.pallas import tpu as pltpu

ML_HEADS, ML_DQK, ML_DV, ML_CONV = 4, 64, 128, 4
ML_W = ML_HEADS * ML_DV
ML_QK_W = 2 * ML_HEADS * ML_DQK
RW_HEADS, RW_DH = 8, 64
RW_W = RW_HEADS * RW_DH
RW_DECAY_LORA, RW_AAA_LORA, RW_GATE_LORA = 64, 64, 128
RW_GN_EPS = 64e-5
CA_HEADS, CA_DH = 4, 128
CA_W = CA_HEADS * CA_DH
N_BRANCH = 3
N_EXPERTS, TOP_K = 32, 4
SWIGLU_LIMIT, SWIGLU_ALPHA = 7.0, 1.702
LN_EPS = 1e-5

LANES = 128
SUBLANES = 8
VMEM_LIMIT = 56 * 1024 * 1024

BF16 = jnp.bfloat16
F32 = jnp.float32
NEG_BIG = -1e30


def _dot(a, b):
    return jnp.dot(a.astype(BF16), b.astype(BF16), preferred_element_type=F32)


def _split3(a):
    hi = a.astype(BF16)
    r1 = a - hi.astype(F32)
    mid = r1.astype(BF16)
    lo = (r1 - mid.astype(F32)).astype(BF16)
    return hi, mid, lo


def _dot2_rhs(a, b_bf16):
    hi = a.astype(BF16)
    lo = (a - hi.astype(F32)).astype(BF16)
    d = functools.partial(jnp.dot, preferred_element_type=F32)
    return d(hi, b_bf16) + d(lo, b_bf16)


def _head_sum(x, red, bc):
    return _dot2_rhs(_dot2_rhs(x, red), bc)


def _dot_exact_lhs(a_bf16, b):
    hi, mid, lo = _split3(b)
    d = functools.partial(jnp.dot, preferred_element_type=F32)
    return d(a_bf16, hi) + d(a_bf16, mid) + d(a_bf16, lo)


def _sigmoid(x):
    return 1.0 / (1.0 + jnp.exp(-x))


def _softplus(x):
    return jnp.maximum(x, 0.0) + jnp.log1p(jnp.exp(-jnp.abs(x)))


def _layer_norm(v, g, b, eps=LN_EPS):
    mu = jnp.mean(v, axis=-1, keepdims=True)
    c = v - mu
    var = jnp.mean(c * c, axis=-1, keepdims=True)
    return c * lax.rsqrt(var + eps) * g + b


def _params(*sem):
    return pltpu.CompilerParams(dimension_semantics=sem, vmem_limit_bytes=VMEM_LIMIT)


def _ordered_after(kernel, first, count):
    def wrapped(*refs):
        return kernel(*refs[:first], *refs[first + count:])
    return wrapped


def _after_specs(after):
    return [pl.BlockSpec(memory_space=pl.ANY)] * len(after)


def _const_spec(shape):
    nd = len(shape)
    return pl.BlockSpec(shape, lambda *_: (0,) * nd, pipeline_mode=pl.Buffered(1))


def _ln_kernel(x_ref, g_ref, b_ref, o_ref, ob_ref):
    y = _layer_norm(x_ref[...], g_ref[...], b_ref[...])
    o_ref[...] = y
    ob_ref[...] = y.astype(BF16)


def _ln_rows(x2d, g, b, tile):
    n, d = x2d.shape
    return pl.pallas_call(
        _ln_kernel,
        grid=(n // tile,),
        in_specs=[pl.BlockSpec((tile, d), lambda i: (i, 0)), _const_spec((1, d)), _const_spec((1, d))],
        out_specs=[pl.BlockSpec((tile, d), lambda i: (i, 0)), pl.BlockSpec((tile, d), lambda i: (i, 0))],
        out_shape=[jax.ShapeDtypeStruct((n, d), F32), jax.ShapeDtypeStruct((n, d), BF16)],
        compiler_params=_params("parallel"),
        name="ln_rows",
    )(x2d, g.reshape(1, d), b.reshape(1, d))


def _mm_kernel(a_ref, b_ref, o_ref):
    o_ref[...] = jnp.dot(a_ref[...], b_ref[...], preferred_element_type=F32)


def _matmul(a, b, tile):
    m, k = a.shape
    n = b.shape[1]
    return pl.pallas_call(
        _mm_kernel,
        grid=(m // tile,),
        in_specs=[pl.BlockSpec((tile, k), lambda i: (i, 0)), _const_spec((k, n))],
        out_specs=pl.BlockSpec((tile, n), lambda i: (i, 0)),
        out_shape=jax.ShapeDtypeStruct((m, n), F32),
        compiler_params=_params("parallel"),
        name="matmul",
    )(a, b)


def _mlstm_kernel(xb_ref, wml_ref, wg_ref, convw_ref, convb_ref, gb_ref, normg_ref, tri_ref,
                  h_ref, ubuf, c_ref, n_ref, m_ref, *, tile, chunk):
    @pl.when(pl.program_id(1) == 0)
    def _():
        ubuf[0:SUBLANES, :] = jnp.zeros((SUBLANES, ML_QK_W), F32)
        c_ref[...] = jnp.zeros_like(c_ref)
        n_ref[...] = jnp.zeros_like(n_ref)
        m_ref[...] = jnp.zeros_like(m_ref)

    xb = xb_ref[...]
    u = jnp.dot(xb, wml_ref[...], preferred_element_type=F32)
    gates = jnp.dot(xb, wg_ref[...], preferred_element_type=F32) + gb_ref[...]

    ubuf[SUBLANES:SUBLANES + tile, :] = u[:, :ML_QK_W]
    acc = jnp.broadcast_to(convb_ref[...], (tile, ML_QK_W))
    for j in range(ML_CONV):
        acc = acc + convw_ref[j:j + 1, :] * ubuf[pl.ds(SUBLANES - ML_CONV + 1 + j, tile), :]
    ubuf[0:SUBLANES, :] = ubuf[tile:tile + SUBLANES, :]
    qk = acc * _sigmoid(acc)
    q = qk[:, :ML_HEADS * ML_DQK] * (ML_DQK ** -0.5)
    k = qk[:, ML_HEADS * ML_DQK:]
    v = u[:, ML_QK_W:ML_QK_W + ML_W]
    og = u[:, ML_QK_W + ML_W:]

    lane = lax.broadcasted_iota(jnp.int32, (chunk, LANES), 1)
    row = lax.broadcasted_iota(jnp.int32, (chunk, chunk), 0)
    col = lax.broadcasted_iota(jnp.int32, (chunk, chunk), 1)
    causal = row >= col
    log_f = -_softplus(-gates)

    for c in range(tile // chunk):
        rs = slice(c * chunk, (c + 1) * chunk)
        bcum = _dot_exact_lhs(tri_ref[...], log_f[rs])
        mcol = jnp.where(lane < ML_HEADS, gates[rs], bcum)
        mrow = mcol.T
        for h in range(ML_HEADS):
            i_col = mcol[:, h:h + 1]
            b_col = mcol[:, ML_HEADS + h:ML_HEADS + h + 1]
            i_row = mrow[h:h + 1, :]
            b_row = mrow[ML_HEADS + h:ML_HEADS + h + 1, :]
            m_prev = m_ref[h][0:1, 0:1]
            n_prev = n_ref[h][0:1, :]
            c_prev = c_ref[h]
            qh = q[rs, h * ML_DQK:(h + 1) * ML_DQK]
            kh = k[rs, h * ML_DQK:(h + 1) * ML_DQK]
            vh = v[rs, h * ML_DV:(h + 1) * ML_DV]

            dmat = jnp.where(causal, b_col - b_row + i_row, -jnp.inf)
            m_inter = b_col + m_prev
            m_t = jnp.maximum(m_inter, jnp.max(dmat, axis=1, keepdims=True))
            s = lax.dot_general(qh.astype(BF16), kh.astype(BF16), (((1,), (1,)), ((), ())),
                                preferred_element_type=F32)
            w_intra = jnp.exp(dmat - m_t) * s
            s_inter = jnp.exp(m_inter - m_t)
            num = s_inter * _dot(qh, c_prev) + _dot(w_intra, vh)
            den = (s_inter * jnp.sum(qh * n_prev, axis=1, keepdims=True)
                   + jnp.sum(w_intra, axis=1, keepdims=True))
            hh = num / jnp.maximum(jnp.abs(den), jnp.exp(-m_t))

            b_last = b_col[chunk - 1:chunk, :]
            g_col = b_last - b_col + i_col
            m_new = jnp.maximum(b_last + m_prev, jnp.max(g_col, axis=0, keepdims=True))
            carry = jnp.exp(b_last + m_prev - m_new)
            kw = kh * jnp.exp(g_col - m_new)
            c_ref[h] = carry * c_prev + lax.dot_general(
                kw.astype(BF16), vh.astype(BF16), (((0,), (0,)), ((), ())), preferred_element_type=F32)
            n_new = carry * n_prev + jnp.sum(kw, axis=0, keepdims=True)
            n_ref[h] = jnp.broadcast_to(n_new, (SUBLANES, ML_DQK))
            m_ref[h] = jnp.broadcast_to(m_new, (SUBLANES, LANES))

            mu = jnp.mean(hh, axis=1, keepdims=True)
            cen = hh - mu
            var = jnp.mean(cen * cen, axis=1, keepdims=True)
            y = cen * lax.rsqrt(var + LN_EPS) * normg_ref[:, h * ML_DV:(h + 1) * ML_DV]
            h_ref[rs, h * ML_DV:(h + 1) * ML_DV] = (
                _sigmoid(og[rs, h * ML_DV:(h + 1) * ML_DV]) * y).astype(BF16)


def _mlstm(xb, wml, wg, convw, convb, gb, normg, tile, chunk, after=()):
    bsz, s, d = xb.shape
    tri = (jnp.arange(chunk)[:, None] >= jnp.arange(chunk)[None, :]).astype(BF16)
    kern = _ordered_after(functools.partial(_mlstm_kernel, tile=tile, chunk=chunk), 8, len(after))
    return pl.pallas_call(
        kern,
        grid=(bsz, s // tile),
        in_specs=[pl.BlockSpec((None, tile, d), lambda b, j: (b, j, 0)),
                  _const_spec(wml.shape), _const_spec(wg.shape), _const_spec(convw.shape),
                  _const_spec(convb.shape), _const_spec(gb.shape), _const_spec(normg.shape),
                  _const_spec(tri.shape)] + _after_specs(after),
        out_specs=pl.BlockSpec((None, tile, ML_W), lambda b, j: (b, j, 0)),
        out_shape=jax.ShapeDtypeStruct((bsz, s, ML_W), BF16),
        scratch_shapes=[pltpu.VMEM((tile + SUBLANES, ML_QK_W), F32),
                        pltpu.VMEM((ML_HEADS, ML_DQK, ML_DV), F32),
                        pltpu.VMEM((ML_HEADS, SUBLANES, ML_DQK), F32),
                        pltpu.VMEM((ML_HEADS, SUBLANES, LANES), F32)],
        compiler_params=_params("parallel", "arbitrary"),
        name="mlstm",
    )(xb, wml, wg, convw, convb, gb, normg, tri, *after)


RW_PAD_IN = 3 * RW_W + 3 * LANES
SCAN_R, SCAN_W, SCAN_K, SCAN_V, SCAN_KK, SCAN_KKA = range(6)
SCAN_OPERANDS = 6


def _rwkv_prep_kernel(xb_ref, wrw_ref, mu_ref, w0_ref, wup_ref, a0_ref, aup_ref, gup_ref, kkw_ref,
                      ka_ref, rk_ref, red_ref, bc_ref,
                      scan_ref, g_ref, bonus_ref, ubuf, *, tile):
    @pl.when(pl.program_id(1) == 0)
    def _():
        ubuf[0:SUBLANES, :] = jnp.zeros((SUBLANES, RW_PAD_IN), F32)

    u = jnp.dot(xb_ref[...], wrw_ref[...], preferred_element_type=F32)
    ubuf[SUBLANES:SUBLANES + tile, :] = u
    u_prev = ubuf[pl.ds(SUBLANES - 1, tile), :]
    ubuf[0:SUBLANES, :] = ubuf[tile:tile + SUBLANES, :]
    us = u + (u_prev - u) * mu_ref[...]
    r = us[:, 0:RW_W]
    kr = us[:, RW_W:2 * RW_W]
    vr = us[:, 2 * RW_W:3 * RW_W]
    wd = us[:, 3 * RW_W:3 * RW_W + LANES]
    ad = us[:, 3 * RW_W + LANES:3 * RW_W + 2 * LANES]
    gd = us[:, 3 * RW_W + 2 * LANES:]

    w_log = -_softplus(-(w0_ref[...] + _dot(jnp.tanh(wd), wup_ref[...]))) - 0.5
    decay = jnp.exp(-jnp.exp(w_log))
    a = _sigmoid(a0_ref[...] + _dot(ad, aup_ref[...]))
    g = _dot(_sigmoid(gd), gup_ref[...])

    red, bc = red_ref[...], bc_ref[...]
    kk = kr * kkw_ref[...]
    kk = kk * lax.rsqrt(jnp.maximum(_head_sum(kk * kk, red, bc), 1e-24))
    k2 = kr * (1.0 + (a - 1.0) * ka_ref[...])
    bonus = _head_sum(r * k2 * rk_ref[...], red, bc) * vr

    scan_ref[SCAN_R] = r
    scan_ref[SCAN_W] = decay
    scan_ref[SCAN_K] = k2
    scan_ref[SCAN_V] = vr
    scan_ref[SCAN_KK] = kk
    scan_ref[SCAN_KKA] = kk * a
    g_ref[...] = g
    bonus_ref[...] = bonus


def _rwkv_prep(xb, wrw, mu, w0, wup, a0, aup, gup, kkw, ka, rk, red, bc, tile):
    bsz, s, d = xb.shape
    consts = (wrw, mu, w0, wup, a0, aup, gup, kkw, ka, rk, red, bc)
    spec = pl.BlockSpec((None, tile, RW_W), lambda b, j: (b, j, 0))
    tspec = pl.BlockSpec((SCAN_OPERANDS, tile, RW_W), lambda b, j: (0, j, b))
    return pl.pallas_call(
        functools.partial(_rwkv_prep_kernel, tile=tile),
        grid=(bsz, s // tile),
        in_specs=[pl.BlockSpec((None, tile, d), lambda b, j: (b, j, 0))] + [_const_spec(c.shape) for c in consts],
        out_specs=[tspec, spec, spec],
        out_shape=[jax.ShapeDtypeStruct((SCAN_OPERANDS, s, bsz * RW_W), F32)]
        + [jax.ShapeDtypeStruct((bsz, s, RW_W), F32)] * 2,
        scratch_shapes=[pltpu.VMEM((tile + SUBLANES, RW_PAD_IN), F32)],
        compiler_params=_params("parallel", "arbitrary"),
        name="rwkv_prep",
    )(xb, *consts)


def _rwkv_scan_kernel(x_ref, o_ref, st_ref, *, steps):
    @pl.when(pl.program_id(1) == 0)
    def _():
        st_ref[...] = jnp.zeros_like(st_ref)

    r_ref, w_ref, k_ref, v_ref, kk_ref, kka_ref = (
        x_ref.at[i] for i in (SCAN_R, SCAN_W, SCAN_K, SCAN_V, SCAN_KK, SCAN_KKA))
    zeros = jnp.zeros((RW_DH, LANES), F32)

    def first(kc, sa):
        return sa + st_ref[kc] * kk_ref[0, pl.ds(kc, 1), :]

    sa0 = lax.fori_loop(0, RW_DH, first, zeros, unroll=8)

    def step(t, sa):
        vt = v_ref[t]
        tn = jnp.minimum(t + 1, steps - 1)

        def body(kc, carry):
            out, sa_next = carry
            row = pl.ds(kc, 1)
            new = st_ref[kc] * w_ref[t, row, :] - sa * kka_ref[t, row, :] + vt * k_ref[t, row, :]
            st_ref[kc] = new
            return out + new * r_ref[t, row, :], sa_next + new * kk_ref[tn, row, :]

        out, sa_next = lax.fori_loop(0, RW_DH, body, (zeros, zeros), unroll=16)
        o_ref[t] = out
        return sa_next

    lax.fori_loop(0, steps, step, sa0)


def _rwkv_scan(x, steps, after=()):
    _, s, dh, nl = x.shape
    spec = pl.BlockSpec((steps, dh, LANES), lambda g, j: (j, 0, g))
    return pl.pallas_call(
        _ordered_after(functools.partial(_rwkv_scan_kernel, steps=steps), 1, len(after)),
        grid=(nl // LANES, s // steps),
        in_specs=[pl.BlockSpec((SCAN_OPERANDS, steps, dh, LANES), lambda g, j: (0, j, 0, g))]
        + _after_specs(after),
        out_specs=spec,
        out_shape=jax.ShapeDtypeStruct((s, dh, nl), F32),
        scratch_shapes=[pltpu.VMEM((dh, dh, LANES), F32)],
        compiler_params=_params("parallel", "arbitrary"),
        name="rwkv_scan",
    )(x, *after)


def _xattn_kernel(xb_ref, wq_ref, kt_ref, v_ref, o_ref):
    q = jnp.dot(xb_ref[...], wq_ref[...], preferred_element_type=F32)
    for h in range(CA_HEADS):
        hs = slice(h * CA_DH, (h + 1) * CA_DH)
        s = jnp.dot(q[:, hs].astype(BF16), kt_ref[hs, :], preferred_element_type=F32) * (CA_DH ** -0.5)
        p = jnp.exp(s - jnp.max(s, axis=1, keepdims=True))
        den = jnp.sum(p, axis=1, keepdims=True)
        o = jnp.dot(p.astype(BF16), v_ref[:, hs], preferred_element_type=F32) / den
        o_ref[:, hs] = o.astype(BF16)


def _xattn(xb, wq, kt, v, tile, after=()):
    bsz, s, d = xb.shape
    m = v.shape[1]
    return pl.pallas_call(
        _ordered_after(_xattn_kernel, 4, len(after)),
        grid=(bsz, s // tile),
        in_specs=[pl.BlockSpec((None, tile, d), lambda b, j: (b, j, 0)), _const_spec(wq.shape),
                  pl.BlockSpec((None, CA_W, m), lambda b, j: (b, 0, 0)),
                  pl.BlockSpec((None, m, CA_W), lambda b, j: (b, 0, 0))] + _after_specs(after),
        out_specs=pl.BlockSpec((None, tile, CA_W), lambda b, j: (b, j, 0)),
        out_shape=jax.ShapeDtypeStruct((bsz, s, CA_W), BF16),
        compiler_params=_params("parallel", "parallel"),
        name="xattn",
    )(xb, wq, kt, v, *after)


ROUTE_IDX, ROUTE_POS, ROUTE_GATE = 0, TOP_K, 2 * TOP_K


def _merge_kernel(x_ref, xb_ref, hml_ref, o_ref, bonus_ref, g_ref, hca_ref,
                  wgate_ref, gateb_ref, red_ref, bc_ref, rwg_ref, rwb_ref, wml_ref, wrw_ref, wca_ref, wo_ref,
                  lng_ref, lnb_ref, rwhi_ref, rwlo_ref, rb_ref, tri_ref, upper_ref,
                  x1_ref, x1b_ref, route_ref, cnt_ref, *, tile, group, alpha, d_model):
    red, bc = red_ref[...], bc_ref[...]
    o = o_ref[...]
    mu = _head_sum(o, red, bc) * (1.0 / RW_DH)
    cen = o - mu
    var = _head_sum(cen * cen, red, bc) * (1.0 / RW_DH)
    h_rw = (cen * lax.rsqrt(var + RW_GN_EPS) * rwg_ref[...] + rwb_ref[...] + bonus_ref[...]) * g_ref[...]

    xb = xb_ref[...]
    y = None
    for br, (h_br, w_ref) in enumerate(((hml_ref[...], wml_ref), (h_rw.astype(BF16), wrw_ref),
                                        (hca_ref[...], wca_ref))):
        cols = slice(br * d_model, (br + 1) * d_model)
        gate = _sigmoid(jnp.dot(xb, wgate_ref[:, cols], preferred_element_type=F32) + gateb_ref[:, cols])
        term = gate * jnp.dot(h_br, w_ref[...], preferred_element_type=F32)
        y = term if y is None else y + term
    mixed = _dot(y, wo_ref[...])
    x1 = _layer_norm(alpha * x_ref[...] + mixed, lng_ref[...], lnb_ref[...])
    x1_ref[...] = x1
    x1b_ref[...] = x1.astype(BF16)

    hi = x1.astype(BF16)
    lo = (x1 - hi.astype(F32)).astype(BF16)
    d = functools.partial(jnp.dot, preferred_element_type=F32)
    logits = d(hi, rwhi_ref[...]) + d(lo, rwhi_ref[...]) + d(hi, rwlo_ref[...]) + rb_ref[...]

    lane = lax.broadcasted_iota(jnp.int32, (tile, LANES), 1)
    vals = logits
    tops, onehots = [], []
    for _ in range(TOP_K):
        m = jnp.max(vals, axis=1, keepdims=True)
        idx = jnp.min(jnp.where(vals == m, lane, LANES), axis=1, keepdims=True)
        sel = lane == idx
        tops.append((m, idx))
        onehots.append(sel.astype(F32))
        vals = jnp.where(sel, -jnp.inf, vals)
    exps = [jnp.exp(m - tops[0][0]) for m, _ in tops]
    den = exps[0] + exps[1] + exps[2] + exps[3]

    cnt = onehots[0] + onehots[1] + onehots[2] + onehots[3]
    starts = []
    for gi in range(tile // group):
        cg = cnt[gi * group:(gi + 1) * group]
        total = jnp.sum(cg, axis=0, keepdims=True)
        run = jnp.floor((total + (MOE_CHUNK - 1)) * (1.0 / MOE_CHUNK)) * MOE_CHUNK
        run_off = jnp.dot(jnp.broadcast_to(run, (SUBLANES, LANES)).astype(BF16), upper_ref[...],
                          preferred_element_type=F32)[0:1, :]
        starts.append(jnp.dot(tri_ref[...], cg.astype(BF16), preferred_element_type=F32) + run_off)
        cnt_ref[gi] = jnp.broadcast_to(total, (SUBLANES, LANES))
    start = starts[0] if len(starts) == 1 else jnp.concatenate(starts, axis=0)
    route = jnp.zeros((tile, LANES), F32)
    for kq in range(TOP_K):
        pos = jnp.sum(onehots[kq] * start, axis=1, keepdims=True)
        route = jnp.where(lane == ROUTE_IDX + kq, tops[kq][1].astype(F32), route)
        route = jnp.where(lane == ROUTE_POS + kq, pos, route)
        route = jnp.where(lane == ROUTE_GATE + kq, exps[kq] / den, route)
    route_ref[...] = route


def _merge(x, xb, hml, o_tm, bonus, g, hca, consts, tile, group, alpha):
    n, d = x.shape
    tiles_per_seq = o_tm.shape[0] // tile
    gpt = tile // group
    tri = (jnp.arange(group)[:, None] > jnp.arange(group)[None, :]).astype(BF16)
    upper = (jnp.arange(LANES)[:, None] < jnp.arange(LANES)[None, :]).astype(BF16)
    consts = tuple(consts) + (tri, upper)

    def rows(w):
        return pl.BlockSpec((tile, w), lambda i: (i, 0))

    o_spec = pl.BlockSpec((tile, RW_W), lambda i: (i % tiles_per_seq, i // tiles_per_seq))
    return pl.pallas_call(
        functools.partial(_merge_kernel, tile=tile, group=group, alpha=alpha, d_model=d),
        grid=(n // tile,),
        in_specs=[rows(d), rows(d), rows(ML_W), o_spec, rows(RW_W), rows(RW_W), rows(CA_W)]
        + [_const_spec(c.shape) for c in consts],
        out_specs=[rows(d), rows(d), rows(LANES),
                   pl.BlockSpec((gpt, SUBLANES, LANES), lambda i: (i, 0, 0))],
        out_shape=[jax.ShapeDtypeStruct((n, d), F32), jax.ShapeDtypeStruct((n, d), BF16),
                   jax.ShapeDtypeStruct((n, LANES), F32),
                   jax.ShapeDtypeStruct((n // group, SUBLANES, LANES), F32)],
        compiler_params=_params("parallel"),
        name="merge_route",
    )(x, xb, hml, o_tm, bonus, g, hca, *consts)


MOE_CHUNK = SUBLANES
CHUNK_TABLE = 1024


def _group_rows(group):
    return -(-(group * TOP_K + N_EXPERTS * (MOE_CHUNK - 1)) // LANES) * LANES


def _dispatch_kernel(tab_ref, zero_blk_ref, route_ref, xb_ref, xd_ref, xs_buf, zbuf, live_ref, sems, zsem,
                     *, group, nsteps, block, prows):
    i = pl.program_id(0)
    slot = i % 2

    @pl.when(i == 0)
    def _():
        zbuf[...] = jnp.zeros_like(zbuf)
        n_zero = zero_blk_ref.shape[0]

        def fresh(j):
            return jnp.logical_or(j == 0, zero_blk_ref[j] != zero_blk_ref[jnp.maximum(j - 1, 0)])

        def zero_copy(j):
            return pltpu.make_async_copy(zbuf, xd_ref.at[pl.ds(zero_blk_ref[j] * block, block)], zsem)

        def start(j, carry):
            @pl.when(fresh(j))
            def _():
                zero_copy(j).start()
            return carry

        def wait(j, carry):
            @pl.when(fresh(j))
            def _():
                zero_copy(j).wait()
            return carry

        lax.fori_loop(0, n_zero, start, 0)
        lax.fori_loop(0, n_zero, wait, 0)

    def chunk_copy(c, s):
        return pltpu.make_async_copy(xs_buf.at[s, pl.ds(c * MOE_CHUNK, MOE_CHUNK)],
                                     xd_ref.at[pl.ds(pl.multiple_of(tab_ref[c], MOE_CHUNK), MOE_CHUNK)],
                                     sems.at[s])

    def retire(s):
        def wait(c, carry):
            chunk_copy(0, s).wait()
            return carry

        lax.fori_loop(0, live_ref[s], wait, 0)

    @pl.when(i >= 2)
    def _():
        retire(slot)

    route_t = route_ref[...].T
    rows = lax.broadcasted_iota(jnp.int32, (prows, group), 0).astype(F32)
    pick = jnp.zeros((prows, group), F32)
    for kq in range(TOP_K):
        pick = pick + jnp.where(rows == route_t[ROUTE_POS + kq:ROUTE_POS + kq + 1, :], 1.0, 0.0)
    xs_buf[slot] = jnp.dot(pick.astype(BF16), xb_ref[...], preferred_element_type=F32)

    live = tab_ref[CHUNK_TABLE - 1]
    live_ref[slot] = live

    def issue(c, carry):
        chunk_copy(c, slot).start()
        return carry

    lax.fori_loop(0, live, issue, 0)

    @pl.when(i == nsteps - 1)
    def _():
        retire(slot)
        if nsteps > 1:
            retire(1 - slot)


def _dispatch(table, zero_blk, route, x1b, n_rows, group, block):
    n, d = x1b.shape
    nsteps = n // group
    prows = _group_rows(group)
    return pl.pallas_call(
        functools.partial(_dispatch_kernel, group=group, nsteps=nsteps, block=block, prows=prows),
        grid=(nsteps,),
        in_specs=[pl.BlockSpec((CHUNK_TABLE,), lambda i: (i,), memory_space=pltpu.SMEM),
                  pl.BlockSpec(memory_space=pltpu.SMEM),
                  pl.BlockSpec((group, LANES), lambda i: (i, 0)),
                  pl.BlockSpec((group, d), lambda i: (i, 0))],
        out_specs=pl.BlockSpec(memory_space=pl.ANY),
        out_shape=jax.ShapeDtypeStruct((n_rows, d), F32),
        scratch_shapes=[pltpu.VMEM((2, prows, d), F32), pltpu.VMEM((block, d), F32),
                        pltpu.SMEM((2,), jnp.int32),
                        pltpu.SemaphoreType.DMA((2,)), pltpu.SemaphoreType.DMA],
        compiler_params=_params("arbitrary"),
        name="moe_dispatch",
    )(table, zero_blk, route, x1b)


DEINT_COLS = 512


def _deinterleave_kernel(w_ref, pe_ref, po_ref, glu_ref, lin_ref):
    wb = w_ref[...].astype(BF16)
    glu_ref[...] = jnp.dot(wb, pe_ref[...], preferred_element_type=F32).astype(BF16)
    lin_ref[...] = jnp.dot(wb, po_ref[...], preferred_element_type=F32).astype(BF16)


def _deinterleave(w_gu_all, layer):
    _, e, d, two_ff = w_gu_all.shape
    half = DEINT_COLS // 2
    src = jnp.arange(DEINT_COLS)[:, None]
    dst = jnp.arange(half)[None, :]
    pe = (src == 2 * dst).astype(BF16)
    po = (src == 2 * dst + 1).astype(BF16)
    out = jax.ShapeDtypeStruct((e, d, two_ff // 2), BF16)
    return pl.pallas_call(
        _deinterleave_kernel,
        grid=(e, two_ff // DEINT_COLS),
        in_specs=[pl.BlockSpec((None, None, d, DEINT_COLS), lambda i, c: (layer, i, 0, c)),
                  _const_spec(pe.shape), _const_spec(po.shape)],
        out_specs=[pl.BlockSpec((None, d, half), lambda i, c: (i, 0, c))] * 2,
        out_shape=[out, out],
        compiler_params=_params("parallel", "parallel"),
        name="deinterleave_w",
    )(w_gu_all, pe, po)


def _expert_kernel(blk_e_ref, nused_ref, xd_ref, wglu_ref, wlin_ref, bglu_ref, blin_ref, wdn_ref, bdn_ref, y_ref):
    del blk_e_ref
    live = pl.program_id(0) < nused_ref[0]

    @pl.when(live)
    def _():
        xb = xd_ref[...].astype(BF16)
        h_glu = jnp.dot(xb, wglu_ref[...], preferred_element_type=F32) + bglu_ref[...]
        h_lin = jnp.dot(xb, wlin_ref[...], preferred_element_type=F32) + blin_ref[...]
        x_glu = jnp.minimum(h_glu, SWIGLU_LIMIT)
        x_lin = jnp.clip(h_lin, -SWIGLU_LIMIT, SWIGLU_LIMIT)
        act = x_glu * _sigmoid(SWIGLU_ALPHA * x_glu) * (x_lin + 1.0)
        y_ref[...] = _dot(act, wdn_ref[...]) + bdn_ref[...]

    @pl.when(jnp.logical_not(live))
    def _():
        y_ref[...] = jnp.zeros_like(y_ref)


def _experts(blk_e, nused, xd, wglu, wlin, bglu, blin, wdn, bdn, block):
    n_rows, d = xd.shape
    dff = wglu.shape[2]
    grid_spec = pltpu.PrefetchScalarGridSpec(
        num_scalar_prefetch=2,
        grid=(n_rows // block,),
        in_specs=[pl.BlockSpec((block, d), lambda i, be, nu: (jnp.minimum(i, nu[0] - 1), 0)),
                  pl.BlockSpec((None, d, dff), lambda i, be, nu: (be[i], 0, 0)),
                  pl.BlockSpec((None, d, dff), lambda i, be, nu: (be[i], 0, 0)),
                  pl.BlockSpec((None, 1, dff), lambda i, be, nu: (be[i], 0, 0)),
                  pl.BlockSpec((None, 1, dff), lambda i, be, nu: (be[i], 0, 0)),
                  pl.BlockSpec((None, dff, d), lambda i, be, nu: (be[i], 0, 0)),
                  pl.BlockSpec((None, 1, d), lambda i, be, nu: (be[i], 0, 0))],
        out_specs=pl.BlockSpec((block, d), lambda i, be, nu: (i, 0)),
    )
    return pl.pallas_call(
        _expert_kernel,
        grid_spec=grid_spec,
        out_shape=jax.ShapeDtypeStruct((n_rows, d), F32),
        compiler_params=_params("arbitrary"),
        name="moe_experts",
    )(blk_e, nused, xd, wglu, wlin, bglu, blin, wdn, bdn)


def _combine_kernel(tab_ref, tab_next_ref, route_ref, x1_ref, y_ref, lng_ref, lnb_ref, x2_ref, x2b_ref,
                    ybuf, sems, *, group, alpha, nsteps, prows):
    i = pl.program_id(0)
    slot = i % 2
    nchunk = prows // MOE_CHUNK

    def fetch(t_ref, s):
        def issue(c, carry):
            pltpu.make_async_copy(y_ref.at[pl.ds(pl.multiple_of(t_ref[c], MOE_CHUNK), MOE_CHUNK)],
                                  ybuf.at[s, pl.ds(c * MOE_CHUNK, MOE_CHUNK)], sems.at[s]).start()
            return carry

        lax.fori_loop(0, nchunk, issue, 0, unroll=4)

    @pl.when(i == 0)
    def _():
        fetch(tab_ref, slot)

    @pl.when(i + 1 < nsteps)
    def _():
        fetch(tab_next_ref, 1 - slot)

    pltpu.make_async_copy(y_ref.at[pl.ds(0, prows)], ybuf.at[slot], sems.at[slot]).wait()
    route = route_ref[...]
    cols = lax.broadcasted_iota(jnp.int32, (group, prows), 1).astype(F32)
    weights = jnp.zeros((group, prows), F32)
    for kq in range(TOP_K):
        weights = weights + jnp.where(cols == route[:, ROUTE_POS + kq:ROUTE_POS + kq + 1],
                                      route[:, ROUTE_GATE + kq:ROUTE_GATE + kq + 1], 0.0)
    moe = jnp.dot(weights.astype(BF16), ybuf[slot].astype(BF16), preferred_element_type=F32)
    x2 = _layer_norm(alpha * x1_ref[...] + moe, lng_ref[...], lnb_ref[...])
    x2_ref[...] = x2
    x2b_ref[...] = x2.astype(BF16)


def _combine(table, route, x1, y_disp, lng, lnb, group, alpha):
    n, d = x1.shape
    nsteps = n // group
    prows = _group_rows(group)
    return pl.pallas_call(
        functools.partial(_combine_kernel, group=group, alpha=alpha, nsteps=nsteps, prows=prows),
        grid=(nsteps,),
        in_specs=[pl.BlockSpec((CHUNK_TABLE,), lambda i: (i,), memory_space=pltpu.SMEM),
                  pl.BlockSpec((CHUNK_TABLE,), lambda i: (jnp.minimum(i + 1, nsteps - 1),),
                               memory_space=pltpu.SMEM),
                  pl.BlockSpec((group, LANES), lambda i: (i, 0)),
                  pl.BlockSpec((group, d), lambda i: (i, 0)),
                  pl.BlockSpec(memory_space=pl.ANY),
                  _const_spec((1, d)), _const_spec((1, d))],
        out_specs=[pl.BlockSpec((group, d), lambda i: (i, 0)), pl.BlockSpec((group, d), lambda i: (i, 0))],
        out_shape=[jax.ShapeDtypeStruct((n, d), F32), jax.ShapeDtypeStruct((n, d), BF16)],
        scratch_shapes=[pltpu.VMEM((2, prows, d), F32), pltpu.SemaphoreType.DMA((2,))],
        compiler_params=_params("arbitrary"),
        name="moe_combine",
    )(table, table, route, x1, y_disp, lng.reshape(1, d), lnb.reshape(1, d))


def _tiles(bsz, seq):
    n = bsz * seq
    return dict(
        ln=min(1024, n),
        mlstm=min(512, seq), mlstm_chunk=min(128, seq),
        rwkv_prep=min(512, seq),
        scan_steps=min(32, seq),
        xattn=min(512, seq),
        merge=min(512, seq),
        moe_group=min(256, seq),
        moe_block=512,
    )


def _pad_cols(w, width):
    return jnp.pad(w, ((0, 0), (0, width - w.shape[1])))


def kernel(x, mem, ln_in_g, ln_in_b, mem_ln_g, mem_ln_b, w_in, ml_conv_w, ml_conv_b, ml_ig_b, ml_fg_b, ml_norm_g, rw_mu, rw_w0, rw_w_up, rw_a0, rw_a_up, rw_g_up, rw_kk, rw_ka, rw_rk, rw_ln_g, rw_ln_b, ca_w_kv, gate_b, w_br_ml, w_br_rw, w_br_ca, w_o, ln1_g, ln1_b, router_w, router_b, w_gu, b_gu, w_dn, b_dn, ln2_g, ln2_b):
    bsz, seq, d = x.shape
    mem_len = mem.shape[1]
    depth = w_in.shape[0]
    n = bsz * seq
    t = _tiles(bsz, seq)
    alpha = (2 * depth) ** 0.25
    d_ff = w_dn.shape[2]

    xf, xb = _ln_rows(x.reshape(n, d), ln_in_g, ln_in_b, t["ln"])
    _, memb = _ln_rows(mem.reshape(bsz * mem_len, d), mem_ln_g, mem_ln_b, min(t["ln"], bsz * mem_len))

    o_qk, o_v, o_og = 0, ML_QK_W, ML_QK_W + ML_W
    o_ig = o_og + ML_W
    o_fg = o_ig + ML_HEADS
    o_rw = o_fg + ML_HEADS
    o_ca = o_rw + 3 * RW_W + RW_DECAY_LORA + RW_AAA_LORA + RW_GATE_LORA
    o_gate = o_ca + CA_W

    head_of_lane = jnp.arange(RW_W) // RW_DH
    red = (head_of_lane[:, None] == jnp.arange(LANES)[None, :]).astype(BF16)
    bc = red.T
    n_asg = n * TOP_K
    block = t["moe_block"]
    group = t["moe_group"]
    n_groups = n // group
    n_chunk = _group_rows(group) // MOE_CHUNK
    assert n_chunk < CHUNK_TABLE and block % MOE_CHUNK == 0
    n_blocks = -(-(n_asg + n_groups * N_EXPERTS * (MOE_CHUNK - 1)) // block) + N_EXPERTS
    n_rows = n_blocks * block

    for l in range(depth):
        w = w_in[l]
        o_wd = o_rw + 3 * RW_W
        o_ad = o_wd + RW_DECAY_LORA
        o_gd = o_ad + RW_AAA_LORA
        wrw = jnp.concatenate([w[:, o_rw:o_wd], _pad_cols(w[:, o_wd:o_ad], LANES),
                               _pad_cols(w[:, o_ad:o_gd], LANES), w[:, o_gd:o_ca]], axis=1).astype(BF16)
        mu = rw_mu[l]
        mu_p = jnp.concatenate([mu[:3 * RW_W], jnp.pad(mu[3 * RW_W:3 * RW_W + RW_DECAY_LORA], (0, LANES - RW_DECAY_LORA)),
                                jnp.pad(mu[3 * RW_W + RW_DECAY_LORA:3 * RW_W + RW_DECAY_LORA + RW_AAA_LORA],
                                        (0, LANES - RW_AAA_LORA)),
                                mu[3 * RW_W + RW_DECAY_LORA + RW_AAA_LORA:]])[None, :]
        wup = jnp.pad(rw_w_up[l], ((0, LANES - RW_DECAY_LORA), (0, 0))).astype(BF16)
        aup = jnp.pad(rw_a_up[l], ((0, LANES - RW_AAA_LORA), (0, 0))).astype(BF16)
        scan_in, g, bonus = _rwkv_prep(
            xb.reshape(bsz, seq, d), wrw, mu_p, rw_w0[l][None, :], wup, rw_a0[l][None, :], aup,
            rw_g_up[l].astype(BF16), rw_kk[l][None, :], rw_ka[l][None, :], rw_rk[l].reshape(1, RW_W), red, bc,
            t["rwkv_prep"])

        nl = bsz * RW_HEADS
        nl_pad = -(-nl // LANES) * LANES
        ops = scan_in.reshape(SCAN_OPERANDS, seq, nl, RW_DH).transpose(0, 1, 3, 2)
        if nl_pad != nl:
            ops = jnp.pad(ops, ((0, 0), (0, 0), (0, 0), (0, nl_pad - nl)))

        wml = w[:, o_qk:o_ig].astype(BF16)
        wg = _pad_cols(w[:, o_ig:o_rw], LANES).astype(BF16)
        gb = _pad_cols(jnp.concatenate([ml_ig_b[l], ml_fg_b[l]])[None, :], LANES)
        h_ml = _mlstm(xb.reshape(bsz, seq, d), wml, wg, ml_conv_w[l], ml_conv_b[l][None, :], gb,
                      ml_norm_g[l][None, :], t["mlstm"], t["mlstm_chunk"], after=(g,))

        kv = _matmul(memb, ca_w_kv[l].astype(BF16), min(512, bsz * mem_len)).reshape(bsz, mem_len, 2 * CA_W)
        kt = kv[:, :, :CA_W].transpose(0, 2, 1).astype(BF16)
        vm = kv[:, :, CA_W:].astype(BF16)
        h_ca = _xattn(xb.reshape(bsz, seq, d), w[:, o_ca:o_gate].astype(BF16), kt, vm, t["xattn"], after=(g,))

        o_scan = _rwkv_scan(ops, t["scan_steps"], after=(h_ml, h_ca))[:, :, :nl]
        o_rwkv = o_scan.transpose(0, 2, 1).reshape(seq, bsz * RW_W)

        rw_pad = _pad_cols(router_w[l], LANES)
        rw_hi = rw_pad.astype(BF16)
        rw_lo = (rw_pad - rw_hi.astype(F32)).astype(BF16)
        rb = jnp.concatenate([router_b[l], jnp.full((LANES - N_EXPERTS,), NEG_BIG, F32)])[None, :]
        consts = (w[:, o_gate:].astype(BF16), gate_b[l][None, :], red, bc, rw_ln_g[l][None, :], rw_ln_b[l][None, :],
                  w_br_ml[l].astype(BF16), w_br_rw[l].astype(BF16), w_br_ca[l].astype(BF16), w_o[l].astype(BF16),
                  ln1_g[l][None, :], ln1_b[l][None, :], rw_hi, rw_lo, rb)
        x1, x1b, route, counts = _merge(xf, xb, h_ml.reshape(n, ML_W), o_rwkv, bonus.reshape(n, RW_W),
                                        g.reshape(n, RW_W), h_ca.reshape(n, CA_W), consts, t["merge"], group, alpha)

        cnt = counts[:, 0, :N_EXPERTS].astype(jnp.int32)
        run = (cnt + MOE_CHUNK - 1) // MOE_CHUNK * MOE_CHUNK
        run_off = jnp.cumsum(run, axis=1) - run
        group_rows = jnp.sum(run, axis=1)
        blocks_per = (jnp.sum(run, axis=0) + block - 1) // block
        blk_end = jnp.cumsum(blocks_per)
        slot_start = (blk_end - blocks_per) * block
        run_slot = slot_start[None, :] + jnp.cumsum(run, axis=0) - run
        chunk_row = (jnp.arange(n_chunk, dtype=jnp.int32) * MOE_CHUNK)[None, :, None]
        in_run = jnp.logical_and(chunk_row >= run_off[:, None, :], chunk_row < (run_off + run)[:, None, :])
        chunk_slot = chunk_row[:, :, 0] + jnp.sum(jnp.where(in_run, (run_slot - run_off)[:, None, :], 0), axis=2)
        chunk_slot = jnp.where(chunk_row[:, :, 0] < group_rows[:, None], chunk_slot, 0)
        table = jnp.zeros((n_groups, CHUNK_TABLE), jnp.int32)
        table = table.at[:, :n_chunk].set(chunk_slot).at[:, CHUNK_TABLE - 1].set(group_rows // MOE_CHUNK)
        table = table.reshape(n_groups * CHUNK_TABLE)
        blk_ids = jnp.arange(n_blocks, dtype=jnp.int32)
        blk_e = jnp.minimum(jnp.sum((blk_ids[:, None] >= blk_end[None, :]).astype(jnp.int32), axis=1),
                            N_EXPERTS - 1)
        nused = blk_end[-1:].astype(jnp.int32)

        last_blk = jnp.maximum(blk_end - 1, 0).astype(jnp.int32)
        trailing = jnp.minimum(nused[0] + jnp.arange(n_blocks - n_asg // block, dtype=jnp.int32), n_blocks - 1)
        zero_blk = jnp.concatenate([last_blk, trailing])
        xd = _dispatch(table, zero_blk, route, x1b, n_rows, group, block)
        w_glu, w_lin = _deinterleave(w_gu, l)
        y_disp = _experts(blk_e, nused, xd, w_glu, w_lin,
                          b_gu[l][:, None, 0::2], b_gu[l][:, None, 1::2], w_dn[l].astype(BF16),
                          b_dn[l][:, None, :], block)
        xf, xb = _combine(table, route, x1, y_disp, ln2_g[l], ln2_b[l], group, alpha)

    del d_ff
    return xf.reshape(bsz, seq, d)
```

```python
import functools

import jax
import jax.numpy as jnp
from jax import lax
from jax.experimental import pallas as pl
from jax.experimental.pallas import tpu as pltpu

ML_HEADS, ML_DQK, ML_DV, ML_CONV = 4, 64, 128, 4
ML_W = ML_HEADS * ML_DV
ML_QK_W = 2 * ML_HEADS * ML_DQK
RW_HEADS, RW_DH = 8, 64
RW_W = RW_HEADS * RW_DH
RW_DECAY_LORA, RW_AAA_LORA, RW_GATE_LORA = 64, 64, 128
RW_GN_EPS = 64e-5
CA_HEADS, CA_DH = 4, 128
CA_W = CA_HEADS * CA_DH
N_BRANCH = 3
N_EXPERTS, TOP_K = 32, 4
SWIGLU_LIMIT, SWIGLU_ALPHA = 7.0, 1.702
LN_EPS = 1e-5

LANES = 128
SUBLANES = 8
VMEM_LIMIT = 56 * 1024 * 1024

BF16 = jnp.bfloat16
F32 = jnp.float32
NEG_BIG = -1e30


def _dot(a, b):
    return jnp.dot(a.astype(BF16), b.astype(BF16), preferred_element_type=F32)


def _split3(a):
    hi = a.astype(BF16)
    r1 = a - hi.astype(F32)
    mid = r1.astype(BF16)
    lo = (r1 - mid.astype(F32)).astype(BF16)
    return hi, mid, lo


def _dot2_rhs(a, b_bf16):
    hi = a.astype(BF16)
    lo = (a - hi.astype(F32)).astype(BF16)
    d = functools.partial(jnp.dot, preferred_element_type=F32)
    return d(hi, b_bf16) + d(lo, b_bf16)


def _head_sum(x, red, bc):
    return _dot2_rhs(_dot2_rhs(x, red), bc)


def _dot_exact_lhs(a_bf16, b):
    hi, mid, lo = _split3(b)
    d = functools.partial(jnp.dot, preferred_element_type=F32)
    return d(a_bf16, hi) + d(a_bf16, mid) + d(a_bf16, lo)


def _sigmoid(x):
    return 1.0 / (1.0 + jnp.exp(-x))


def _softplus(x):
    return jnp.maximum(x, 0.0) + jnp.log1p(jnp.exp(-jnp.abs(x)))


def _layer_norm(v, g, b, eps=LN_EPS):
    mu = jnp.mean(v, axis=-1, keepdims=True)
    c = v - mu
    var = jnp.mean(c * c, axis=-1, keepdims=True)
    return c * lax.rsqrt(var + eps) * g + b


def _params(*sem):
    return pltpu.CompilerParams(dimension_semantics=sem, vmem_limit_bytes=VMEM_LIMIT)


def _ordered_after(kernel, first, count):
    def wrapped(*refs):
        return kernel(*refs[:first], *refs[first + count:])
    return wrapped


def _after_specs(after):
    return [pl.BlockSpec(memory_space=pl.ANY)] * len(after)


def _const_spec(shape):
    nd = len(shape)
    return pl.BlockSpec(shape, lambda *_: (0,) * nd, pipeline_mode=pl.Buffered(1))


def _ln_kernel(x_ref, g_ref, b_ref, o_ref, ob_ref):
    y = _layer_norm(x_ref[...], g_ref[...], b_ref[...])
    o_ref[...] = y
    ob_ref[...] = y.astype(BF16)


def _ln_rows(x2d, g, b, tile):
    n, d = x2d.shape
    return pl.pallas_call(
        _ln_kernel,
        grid=(n // tile,),
        in_specs=[pl.BlockSpec((tile, d), lambda i: (i, 0)), _const_spec((1, d)), _const_spec((1, d))],
        out_specs=[pl.BlockSpec((tile, d), lambda i: (i, 0)), pl.BlockSpec((tile, d), lambda i: (i, 0))],
        out_shape=[jax.ShapeDtypeStruct((n, d), F32), jax.ShapeDtypeStruct((n, d), BF16)],
        compiler_params=_params("parallel"),
        name="ln_rows",
    )(x2d, g.reshape(1, d), b.reshape(1, d))


def _mm_kernel(a_ref, b_ref, o_ref):
    o_ref[...] = jnp.dot(a_ref[...], b_ref[...], preferred_element_type=F32)


def _matmul(a, b, tile):
    m, k = a.shape
    n = b.shape[1]
    return pl.pallas_call(
        _mm_kernel,
        grid=(m // tile,),
        in_specs=[pl.BlockSpec((tile, k), lambda i: (i, 0)), _const_spec((k, n))],
        out_specs=pl.BlockSpec((tile, n), lambda i: (i, 0)),
        out_shape=jax.ShapeDtypeStruct((m, n), F32),
        compiler_params=_params("parallel"),
        name="matmul",
    )(a, b)


def _mlstm_kernel(xb_ref, wml_ref, wg_ref, convw_ref, convb_ref, gb_ref, normg_ref, tri_ref,
                  h_ref, ubuf, c_ref, n_ref, m_ref, *, tile, chunk):
    @pl.when(pl.program_id(1) == 0)
    def _():
        ubuf[0:SUBLANES, :] = jnp.zeros((SUBLANES, ML_QK_W), F32)
        c_ref[...] = jnp.zeros_like(c_ref)
        n_ref[...] = jnp.zeros_like(n_ref)
        m_ref[...] = jnp.zeros_like(m_ref)

    xb = xb_ref[...]
    u = jnp.dot(xb, wml_ref[...], preferred_element_type=F32)
    gates = jnp.dot(xb, wg_ref[...], preferred_element_type=F32) + gb_ref[...]

    ubuf[SUBLANES:SUBLANES + tile, :] = u[:, :ML_QK_W]
    acc = jnp.broadcast_to(convb_ref[...], (tile, ML_QK_W))
    for j in range(ML_CONV):
        acc = acc + convw_ref[j:j + 1, :] * ubuf[pl.ds(SUBLANES - ML_CONV + 1 + j, tile), :]
    ubuf[0:SUBLANES, :] = ubuf[tile:tile + SUBLANES, :]
    qk = acc * _sigmoid(acc)
    q = qk[:, :ML_HEADS * ML_DQK] * (ML_DQK ** -0.5)
    k = qk[:, ML_HEADS * ML_DQK:]
    v = u[:, ML_QK_W:ML_QK_W + ML_W]
    og = u[:, ML_QK_W + ML_W:]

    lane = lax.broadcasted_iota(jnp.int32, (chunk, LANES), 1)
    row = lax.broadcasted_iota(jnp.int32, (chunk, chunk), 0)
    col = lax.broadcasted_iota(jnp.int32, (chunk, chunk), 1)
    causal = row >= col
    log_f = -_softplus(-gates)

    for c in range(tile // chunk):
        rs = slice(c * chunk, (c + 1) * chunk)
        bcum = _dot_exact_lhs(tri_ref[...], log_f[rs])
        mcol = jnp.where(lane < ML_HEADS, gates[rs], bcum)
        mrow = mcol.T
        for h in range(ML_HEADS):
            i_col = mcol[:, h:h + 1]
            b_col = mcol[:, ML_HEADS + h:ML_HEADS + h + 1]
            i_row = mrow[h:h + 1, :]
            b_row = mrow[ML_HEADS + h:ML_HEADS + h + 1, :]
            m_prev = m_ref[h][0:1, 0:1]
            n_prev = n_ref[h][0:1, :]
            c_prev = c_ref[h]
            qh = q[rs, h * ML_DQK:(h + 1) * ML_DQK]
            kh = k[rs, h * ML_DQK:(h + 1) * ML_DQK]
            vh = v[rs, h * ML_DV:(h + 1) * ML_DV]

            dmat = jnp.where(causal, b_col - b_row + i_row, -jnp.inf)
            m_inter = b_col + m_prev
            m_t = jnp.maximum(m_inter, jnp.max(dmat, axis=1, keepdims=True))
            s = lax.dot_general(qh.astype(BF16), kh.astype(BF16), (((1,), (1,)), ((), ())),
                                preferred_element_type=F32)
            w_intra = jnp.exp(dmat - m_t) * s
            s_inter = jnp.exp(m_inter - m_t)
            num = s_inter * _dot(qh, c_prev) + _dot(w_intra, vh)
            den = (s_inter * jnp.sum(qh * n_prev, axis=1, keepdims=True)
                   + jnp.sum(w_intra, axis=1, keepdims=True))
            hh = num / jnp.maximum(jnp.abs(den), jnp.exp(-m_t))

            b_last = b_col[chunk - 1:chunk, :]
            g_col = b_last - b_col + i_col
            m_new = jnp.maximum(b_last + m_prev, jnp.max(g_col, axis=0, keepdims=True))
            carry = jnp.exp(b_last + m_prev - m_new)
            kw = kh * jnp.exp(g_col - m_new)
            c_ref[h] = carry * c_prev + lax.dot_general(
                kw.astype(BF16), vh.astype(BF16), (((0,), (0,)), ((), ())), preferred_element_type=F32)
            n_new = carry * n_prev + jnp.sum(kw, axis=0, keepdims=True)
            n_ref[h] = jnp.broadcast_to(n_new, (SUBLANES, ML_DQK))
            m_ref[h] = jnp.broadcast_to(m_new, (SUBLANES, LANES))

            mu = jnp.mean(hh, axis=1, keepdims=True)
            cen = hh - mu
            var = jnp.mean(cen * cen, axis=1, keepdims=True)
            y = cen * lax.rsqrt(var + LN_EPS) * normg_ref[:, h * ML_DV:(h + 1) * ML_DV]
            h_ref[rs, h * ML_DV:(h + 1) * ML_DV] = (
                _sigmoid(og[rs, h * ML_DV:(h + 1) * ML_DV]) * y).astype(BF16)


def _mlstm(xb, wml, wg, convw, convb, gb, normg, tile, chunk, after=()):
    bsz, s, d = xb.shape
    tri = (jnp.arange(chunk)[:, None] >= jnp.arange(chunk)[None, :]).astype(BF16)
    kern = _ordered_after(functools.partial(_mlstm_kernel, tile=tile, chunk=chunk), 8, len(after))
    return pl.pallas_call(
        kern,
        grid=(bsz, s // tile),
        in_specs=[pl.BlockSpec((None, tile, d), lambda b, j: (b, j, 0)),
                  _const_spec(wml.shape), _const_spec(wg.shape), _const_spec(convw.shape),
                  _const_spec(convb.shape), _const_spec(gb.shape), _const_spec(normg.shape),
                  _const_spec(tri.shape)] + _after_specs(after),
        out_specs=pl.BlockSpec((None, tile, ML_W), lambda b, j: (b, j, 0)),
        out_shape=jax.ShapeDtypeStruct((bsz, s, ML_W), BF16),
        scratch_shapes=[pltpu.VMEM((tile + SUBLANES, ML_QK_W), F32),
                        pltpu.VMEM((ML_HEADS, ML_DQK, ML_DV), F32),
                        pltpu.VMEM((ML_HEADS, SUBLANES, ML_DQK), F32),
                        pltpu.VMEM((ML_HEADS, SUBLANES, LANES), F32)],
        compiler_params=_params("parallel", "arbitrary"),
        name="mlstm",
    )(xb, wml, wg, convw, convb, gb, normg, tri, *after)


RW_PAD_IN = 3 * RW_W + 3 * LANES
SCAN_R, SCAN_W, SCAN_K, SCAN_V, SCAN_KK, SCAN_KKA = range(6)
SCAN_OPERANDS = 6


def _rwkv_prep_kernel(xb_ref, wrw_ref, mu_ref, w0_ref, wup_ref, a0_ref, aup_ref, gup_ref, kkw_ref,
                      ka_ref, rk_ref, red_ref, bc_ref,
                      scan_ref, g_ref, bonus_ref, ubuf, *, tile):
    @pl.when(pl.program_id(1) == 0)
    def _():
        ubuf[0:SUBLANES, :] = jnp.zeros((SUBLANES, RW_PAD_IN), F32)

    u = jnp.dot(xb_ref[...], wrw_ref[...], preferred_element_type=F32)
    ubuf[SUBLANES:SUBLANES + tile, :] = u
    u_prev = ubuf[pl.ds(SUBLANES - 1, tile), :]
    ubuf[0:SUBLANES, :] = ubuf[tile:tile + SUBLANES, :]
    us = u + (u_prev - u) * mu_ref[...]
    r = us[:, 0:RW_W]
    kr = us[:, RW_W:2 * RW_W]
    vr = us[:, 2 * RW_W:3 * RW_W]
    wd = us[:, 3 * RW_W:3 * RW_W + LANES]
    ad = us[:, 3 * RW_W + LANES:3 * RW_W + 2 * LANES]
    gd = us[:, 3 * RW_W + 2 * LANES:]

    w_log = -_softplus(-(w0_ref[...] + _dot(jnp.tanh(wd), wup_ref[...]))) - 0.5
    decay = jnp.exp(-jnp.exp(w_log))
    a = _sigmoid(a0_ref[...] + _dot(ad, aup_ref[...]))
    g = _dot(_sigmoid(gd), gup_ref[...])

    red, bc = red_ref[...], bc_ref[...]
    kk = kr * kkw_ref[...]
    kk = kk * lax.rsqrt(jnp.maximum(_head_sum(kk * kk, red, bc), 1e-24))
    k2 = kr * (1.0 + (a - 1.0) * ka_ref[...])
    bonus = _head_sum(r * k2 * rk_ref[...], red, bc) * vr

    scan_ref[SCAN_R] = r
    scan_ref[SCAN_W] = decay
    scan_ref[SCAN_K] = k2
    scan_ref[SCAN_V] = vr
    scan_ref[SCAN_KK] = kk
    scan_ref[SCAN_KKA] = kk * a
    g_ref[...] = g
    bonus_ref[...] = bonus


def _rwkv_prep(xb, wrw, mu, w0, wup, a0, aup, gup, kkw, ka, rk, red, bc, tile):
    bsz, s, d = xb.shape
    consts = (wrw, mu, w0, wup, a0, aup, gup, kkw, ka, rk, red, bc)
    spec = pl.BlockSpec((None, tile, RW_W), lambda b, j: (b, j, 0))
    tspec = pl.BlockSpec((SCAN_OPERANDS, tile, RW_W), lambda b, j: (0, j, b))
    return pl.pallas_call(
        functools.partial(_rwkv_prep_kernel, tile=tile),
        grid=(bsz, s // tile),
        in_specs=[pl.BlockSpec((None, tile, d), lambda b, j: (b, j, 0))] + [_const_spec(c.shape) for c in consts],
        out_specs=[tspec, spec, spec],
        out_shape=[jax.ShapeDtypeStruct((SCAN_OPERANDS, s, bsz * RW_W), F32)]
        + [jax.ShapeDtypeStruct((bsz, s, RW_W), F32)] * 2,
        scratch_shapes=[pltpu.VMEM((tile + SUBLANES, RW_PAD_IN), F32)],
        compiler_params=_params("parallel", "arbitrary"),
        name="rwkv_prep",
    )(xb, *consts)


def _rwkv_scan_kernel(x_ref, o_ref, st_ref, *, steps):
    @pl.when(pl.program_id(1) == 0)
    def _():
        st_ref[...] = jnp.zeros_like(st_ref)

    r_ref, w_ref, k_ref, v_ref, kk_ref, kka_ref = (
        x_ref.at[i] for i in (SCAN_R, SCAN_W, SCAN_K, SCAN_V, SCAN_KK, SCAN_KKA))
    zeros = jnp.zeros((RW_DH, LANES), F32)

    def first(kc, sa):
        return sa + st_ref[kc] * kk_ref[0, pl.ds(kc, 1), :]

    sa0 = lax.fori_loop(0, RW_DH, first, zeros, unroll=8)

    def step(t, sa):
        vt = v_ref[t]
        tn = jnp.minimum(t + 1, steps - 1)

        def body(kc, carry):
            out, sa_next = carry
            row = pl.ds(kc, 1)
            new = st_ref[kc] * w_ref[t, row, :] - sa * kka_ref[t, row, :] + vt * k_ref[t, row, :]
            st_ref[kc] = new
            return out + new * r_ref[t, row, :], sa_next + new * kk_ref[tn, row, :]

        out, sa_next = lax.fori_loop(0, RW_DH, body, (zeros, zeros), unroll=16)
        o_ref[t] = out
        return sa_next

    lax.fori_loop(0, steps, step, sa0)


def _rwkv_scan(x, steps, after=()):
    _, s, dh, nl = x.shape
    spec = pl.BlockSpec((steps, dh, LANES), lambda g, j: (j, 0, g))
    return pl.pallas_call(
        _ordered_after(functools.partial(_rwkv_scan_kernel, steps=steps), 1, len(after)),
        grid=(nl // LANES, s // steps),
        in_specs=[pl.BlockSpec((SCAN_OPERANDS, steps, dh, LANES), lambda g, j: (0, j, 0, g))]
        + _after_specs(after),
        out_specs=spec,
        out_shape=jax.ShapeDtypeStruct((s, dh, nl), F32),
        scratch_shapes=[pltpu.VMEM((dh, dh, LANES), F32)],
        compiler_params=_params("parallel", "arbitrary"),
        name="rwkv_scan",
    )(x, *after)


def _xattn_kernel(xb_ref, wq_ref, kt_ref, v_ref, o_ref):
    q = jnp.dot(xb_ref[...], wq_ref[...], preferred_element_type=F32)
    for h in range(CA_HEADS):
        hs = slice(h * CA_DH, (h + 1) * CA_DH)
        s = jnp.dot(q[:, hs].astype(BF16), kt_ref[hs, :], preferred_element_type=F32) * (CA_DH ** -0.5)
        p = jnp.exp(s - jnp.max(s, axis=1, keepdims=True))
        den = jnp.sum(p, axis=1, keepdims=True)
        o = jnp.dot(p.astype(BF16), v_ref[:, hs], preferred_element_type=F32) / den
        o_ref[:, hs] = o.astype(BF16)


def _xattn(xb, wq, kt, v, tile, after=()):
    bsz, s, d = xb.shape
    m = v.shape[1]
    return pl.pallas_call(
        _ordered_after(_xattn_kernel, 4, len(after)),
        grid=(bsz, s // tile),
        in_specs=[pl.BlockSpec((None, tile, d), lambda b, j: (b, j, 0)), _const_spec(wq.shape),
                  pl.BlockSpec((None, CA_W, m), lambda b, j: (b, 0, 0)),
                  pl.BlockSpec((None, m, CA_W), lambda b, j: (b, 0, 0))] + _after_specs(after),
        out_specs=pl.BlockSpec((None, tile, CA_W), lambda b, j: (b, j, 0)),
        out_shape=jax.ShapeDtypeStruct((bsz, s, CA_W), BF16),
        compiler_params=_params("parallel", "parallel"),
        name="xattn",
    )(xb, wq, kt, v, *after)


ROUTE_IDX, ROUTE_POS, ROUTE_GATE = 0, TOP_K, 2 * TOP_K


def _merge_kernel(x_ref, xb_ref, hml_ref, o_ref, bonus_ref, g_ref, hca_ref,
                  wgate_ref, gateb_ref, red_ref, bc_ref, rwg_ref, rwb_ref, wml_ref, wrw_ref, wca_ref, wo_ref,
                  lng_ref, lnb_ref, rwhi_ref, rwlo_ref, rb_ref, tri_ref, upper_ref,
                  x1_ref, x1b_ref, route_ref, cnt_ref, *, tile, group, alpha, d_model):
    red, bc = red_ref[...], bc_ref[...]
    o = o_ref[...]
    mu = _head_sum(o, red, bc) * (1.0 / RW_DH)
    cen = o - mu
    var = _head_sum(cen * cen, red, bc) * (1.0 / RW_DH)
    h_rw = (cen * lax.rsqrt(var + RW_GN_EPS) * rwg_ref[...] + rwb_ref[...] + bonus_ref[...]) * g_ref[...]

    xb = xb_ref[...]
    y = None
    for br, (h_br, w_ref) in enumerate(((hml_ref[...], wml_ref), (h_rw.astype(BF16), wrw_ref),
                                        (hca_ref[...], wca_ref))):
        cols = slice(br * d_model, (br + 1) * d_model)
        gate = _sigmoid(jnp.dot(xb, wgate_ref[:, cols], preferred_element_type=F32) + gateb_ref[:, cols])
        term = gate * jnp.dot(h_br, w_ref[...], preferred_element_type=F32)
        y = term if y is None else y + term
    mixed = _dot(y, wo_ref[...])
    x1 = _layer_norm(alpha * x_ref[...] + mixed, lng_ref[...], lnb_ref[...])
    x1_ref[...] = x1
    x1b_ref[...] = x1.astype(BF16)

    hi = x1.astype(BF16)
    lo = (x1 - hi.astype(F32)).astype(BF16)
    d = functools.partial(jnp.dot, preferred_element_type=F32)
    logits = d(hi, rwhi_ref[...]) + d(lo, rwhi_ref[...]) + d(hi, rwlo_ref[...]) + rb_ref[...]

    lane = lax.broadcasted_iota(jnp.int32, (tile, LANES), 1)
    vals = logits
    tops, onehots = [], []
    for _ in range(TOP_K):
        m = jnp.max(vals, axis=1, keepdims=True)
        idx = jnp.min(jnp.where(vals == m, lane, LANES), axis=1, keepdims=True)
        sel = lane == idx
        tops.append((m, idx))
        onehots.append(sel.astype(F32))
        vals = jnp.where(sel, -jnp.inf, vals)
    exps = [jnp.exp(m - tops[0][0]) for m, _ in tops]
    den = exps[0] + exps[1] + exps[2] + exps[3]

    cnt = onehots[0] + onehots[1] + onehots[2] + onehots[3]
    starts = []
    for gi in range(tile // group):
        cg = cnt[gi * group:(gi + 1) * group]
        total = jnp.sum(cg, axis=0, keepdims=True)
        run = jnp.floor((total + (MOE_CHUNK - 1)) * (1.0 / MOE_CHUNK)) * MOE_CHUNK
        run_off = jnp.dot(jnp.broadcast_to(run, (SUBLANES, LANES)).astype(BF16), upper_ref[...],
                          preferred_element_type=F32)[0:1, :]
        starts.append(jnp.dot(tri_ref[...], cg.astype(BF16), preferred_element_type=F32) + run_off)
        cnt_ref[gi] = jnp.broadcast_to(total, (SUBLANES, LANES))
    start = starts[0] if len(starts) == 1 else jnp.concatenate(starts, axis=0)
    route = jnp.zeros((tile, LANES), F32)
    for kq in range(TOP_K):
        pos = jnp.sum(onehots[kq] * start, axis=1, keepdims=True)
        route = jnp.where(lane == ROUTE_IDX + kq, tops[kq][1].astype(F32), route)
        route = jnp.where(lane == ROUTE_POS + kq, pos, route)
        route = jnp.where(lane == ROUTE_GATE + kq, exps[kq] / den, route)
    route_ref[...] = route


def _merge(x, xb, hml, o_tm, bonus, g, hca, consts, tile, group, alpha):
    n, d = x.shape
    tiles_per_seq = o_tm.shape[0] // tile
    gpt = tile // group
    tri = (jnp.arange(group)[:, None] > jnp.arange(group)[None, :]).astype(BF16)
    upper = (jnp.arange(LANES)[:, None] < jnp.arange(LANES)[None, :]).astype(BF16)
    consts = tuple(consts) + (tri, upper)

    def rows(w):
        return pl.BlockSpec((tile, w), lambda i: (i, 0))

    o_spec = pl.BlockSpec((tile, RW_W), lambda i: (i % tiles_per_seq, i // tiles_per_seq))
    return pl.pallas_call(
        functools.partial(_merge_kernel, tile=tile, group=group, alpha=alpha, d_model=d),
        grid=(n // tile,),
        in_specs=[rows(d), rows(d), rows(ML_W), o_spec, rows(RW_W), rows(RW_W), rows(CA_W)]
        + [_const_spec(c.shape) for c in consts],
        out_specs=[rows(d), rows(d), rows(LANES),
                   pl.BlockSpec((gpt, SUBLANES, LANES), lambda i: (i, 0, 0))],
        out_shape=[jax.ShapeDtypeStruct((n, d), F32), jax.ShapeDtypeStruct((n, d), BF16),
                   jax.ShapeDtypeStruct((n, LANES), F32),
                   jax.ShapeDtypeStruct((n // group, SUBLANES, LANES), F32)],
        compiler_params=_params("parallel"),
        name="merge_route",
    )(x, xb, hml, o_tm, bonus, g, hca, *consts)


MOE_CHUNK = SUBLANES
CHUNK_TABLE = 1024


def _group_rows(group):
    return -(-(group * TOP_K + N_EXPERTS * (MOE_CHUNK - 1)) // LANES) * LANES


def _dispatch_kernel(tab_ref, zero_blk_ref, route_ref, xb_ref, xd_ref, xs_buf, zbuf, live_ref, sems, zsem,
                     *, group, nsteps, block, prows):
    i = pl.program_id(0)
    slot = i % 2

    @pl.when(i == 0)
    def _():
        zbuf[...] = jnp.zeros_like(zbuf)
        n_zero = zero_blk_ref.shape[0]

        def fresh(j):
            return jnp.logical_or(j == 0, zero_blk_ref[j] != zero_blk_ref[jnp.maximum(j - 1, 0)])

        def zero_copy(j):
            return pltpu.make_async_copy(zbuf, xd_ref.at[pl.ds(zero_blk_ref[j] * block, block)], zsem)

        def start(j, carry):
            @pl.when(fresh(j))
            def _():
                zero_copy(j).start()
            return carry

        def wait(j, carry):
            @pl.when(fresh(j))
            def _():
                zero_copy(j).wait()
            return carry

        lax.fori_loop(0, n_zero, start, 0)
        lax.fori_loop(0, n_zero, wait, 0)

    def chunk_copy(c, s):
        return pltpu.make_async_copy(xs_buf.at[s, pl.ds(c * MOE_CHUNK, MOE_CHUNK)],
                                     xd_ref.at[pl.ds(pl.multiple_of(tab_ref[c], MOE_CHUNK), MOE_CHUNK)],
                                     sems.at[s])

    def retire(s):
        def wait(c, carry):
            chunk_copy(0, s).wait()
            return carry

        lax.fori_loop(0, live_ref[s], wait, 0)

    @pl.when(i >= 2)
    def _():
        retire(slot)

    route_t = route_ref[...].T
    rows = lax.broadcasted_iota(jnp.int32, (prows, group), 0).astype(F32)
    pick = jnp.zeros((prows, group), F32)
    for kq in range(TOP_K):
        pick = pick + jnp.where(rows == route_t[ROUTE_POS + kq:ROUTE_POS + kq + 1, :], 1.0, 0.0)
    xs_buf[slot] = jnp.dot(pick.astype(BF16), xb_ref[...], preferred_element_type=F32)

    live = tab_ref[CHUNK_TABLE - 1]
    live_ref[slot] = live

    def issue(c, carry):
        chunk_copy(c, slot).start()
        return carry

    lax.fori_loop(0, live, issue, 0)

    @pl.when(i == nsteps - 1)
    def _():
        retire(slot)
        if nsteps > 1:
            retire(1 - slot)


def _dispatch(table, zero_blk, route, x1b, n_rows, group, block):
    n, d = x1b.shape
    nsteps = n // group
    prows = _group_rows(group)
    return pl.pallas_call(
        functools.partial(_dispatch_kernel, group=group, nsteps=nsteps, block=block, prows=prows),
        grid=(nsteps,),
        in_specs=[pl.BlockSpec((CHUNK_TABLE,), lambda i: (i,), memory_space=pltpu.SMEM),
                  pl.BlockSpec(memory_space=pltpu.SMEM),
                  pl.BlockSpec((group, LANES), lambda i: (i, 0)),
                  pl.BlockSpec((group, d), lambda i: (i, 0))],
        out_specs=pl.BlockSpec(memory_space=pl.ANY),
        out_shape=jax.ShapeDtypeStruct((n_rows, d), F32),
        scratch_shapes=[pltpu.VMEM((2, prows, d), F32), pltpu.VMEM((block, d), F32),
                        pltpu.SMEM((2,), jnp.int32),
                        pltpu.SemaphoreType.DMA((2,)), pltpu.SemaphoreType.DMA],
        compiler_params=_params("arbitrary"),
        name="moe_dispatch",
    )(table, zero_blk, route, x1b)


DEINT_COLS = 512


def _deinterleave_kernel(w_ref, pe_ref, po_ref, glu_ref, lin_ref):
    wb = w_ref[...].astype(BF16)
    glu_ref[...] = jnp.dot(wb, pe_ref[...], preferred_element_type=F32).astype(BF16)
    lin_ref[...] = jnp.dot(wb, po_ref[...], preferred_element_type=F32).astype(BF16)


def _deinterleave(w_gu_all, layer):
    _, e, d, two_ff = w_gu_all.shape
    half = DEINT_COLS // 2
    src = jnp.arange(DEINT_COLS)[:, None]
    dst = jnp.arange(half)[None, :]
    pe = (src == 2 * dst).astype(BF16)
    po = (src == 2 * dst + 1).astype(BF16)
    out = jax.ShapeDtypeStruct((e, d, two_ff // 2), BF16)
    return pl.pallas_call(
        _deinterleave_kernel,
        grid=(e, two_ff // DEINT_COLS),
        in_specs=[pl.BlockSpec((None, None, d, DEINT_COLS), lambda i, c: (layer, i, 0, c)),
                  _const_spec(pe.shape), _const_spec(po.shape)],
        out_specs=[pl.BlockSpec((None, d, half), lambda i, c: (i, 0, c))] * 2,
        out_shape=[out, out],
        compiler_params=_params("parallel", "parallel"),
        name="deinterleave_w",
    )(w_gu_all, pe, po)


def _expert_kernel(blk_e_ref, nused_ref, xd_ref, wglu_ref, wlin_ref, bglu_ref, blin_ref, wdn_ref, bdn_ref, y_ref):
    del blk_e_ref
    live = pl.program_id(0) < nused_ref[0]

    @pl.when(live)
    def _():
        xb = xd_ref[...].astype(BF16)
        h_glu = jnp.dot(xb, wglu_ref[...], preferred_element_type=F32) + bglu_ref[...]
        h_lin = jnp.dot(xb, wlin_ref[...], preferred_element_type=F32) + blin_ref[...]
        x_glu = jnp.minimum(h_glu, SWIGLU_LIMIT)
        x_lin = jnp.clip(h_lin, -SWIGLU_LIMIT, SWIGLU_LIMIT)
        act = x_glu * _sigmoid(SWIGLU_ALPHA * x_glu) * (x_lin + 1.0)
        y_ref[...] = _dot(act, wdn_ref[...]) + bdn_ref[...]

    @pl.when(jnp.logical_not(live))
    def _():
        y_ref[...] = jnp.zeros_like(y_ref)


def _experts(blk_e, nused, xd, wglu, wlin, bglu, blin, wdn, bdn, block):
    n_rows, d = xd.shape
    dff = wglu.shape[2]
    grid_spec = pltpu.PrefetchScalarGridSpec(
        num_scalar_prefetch=2,
        grid=(n_rows // block,),
        in_specs=[pl.BlockSpec((block, d), lambda i, be, nu: (jnp.minimum(i, nu[0] - 1), 0)),
                  pl.BlockSpec((None, d, dff), lambda i, be, nu: (be[i], 0, 0)),
                  pl.BlockSpec((None, d, dff), lambda i, be, nu: (be[i], 0, 0)),
                  pl.BlockSpec((None, 1, dff), lambda i, be, nu: (be[i], 0, 0)),
                  pl.BlockSpec((None, 1, dff), lambda i, be, nu: (be[i], 0, 0)),
                  pl.BlockSpec((None, dff, d), lambda i, be, nu: (be[i], 0, 0)),
                  pl.BlockSpec((None, 1, d), lambda i, be, nu: (be[i], 0, 0))],
        out_specs=pl.BlockSpec((block, d), lambda i, be, nu: (i, 0)),
    )
    return pl.pallas_call(
        _expert_kernel,
        grid_spec=grid_spec,
        out_shape=jax.ShapeDtypeStruct((n_rows, d), F32),
        compiler_params=_params("arbitrary"),
        name="moe_experts",
    )(blk_e, nused, xd, wglu, wlin, bglu, blin, wdn, bdn)


def _combine_kernel(tab_ref, tab_next_ref, route_ref, x1_ref, y_ref, lng_ref, lnb_ref, x2_ref, x2b_ref,
                    ybuf, sems, *, group, alpha, nsteps, prows):
    i = pl.program_id(0)
    slot = i % 2
    nchunk = prows // MOE_CHUNK

    def fetch(t_ref, s):
        def issue(c, carry):
            pltpu.make_async_copy(y_ref.at[pl.ds(pl.multiple_of(t_ref[c], MOE_CHUNK), MOE_CHUNK)],
                                  ybuf.at[s, pl.ds(c * MOE_CHUNK, MOE_CHUNK)], sems.at[s]).start()
            return carry

        lax.fori_loop(0, nchunk, issue, 0, unroll=4)

    @pl.when(i == 0)
    def _():
        fetch(tab_ref, slot)

    @pl.when(i + 1 < nsteps)
    def _():
        fetch(tab_next_ref, 1 - slot)

    pltpu.make_async_copy(y_ref.at[pl.ds(0, prows)], ybuf.at[slot], sems.at[slot]).wait()
    route = route_ref[...]
    cols = lax.broadcasted_iota(jnp.int32, (group, prows), 1).astype(F32)
    weights = jnp.zeros((group, prows), F32)
    for kq in range(TOP_K):
        weights = weights + jnp.where(cols == route[:, ROUTE_POS + kq:ROUTE_POS + kq + 1],
                                      route[:, ROUTE_GATE + kq:ROUTE_GATE + kq + 1], 0.0)
    moe = jnp.dot(weights.astype(BF16), ybuf[slot].astype(BF16), preferred_element_type=F32)
    x2 = _layer_norm(alpha * x1_ref[...] + moe, lng_ref[...], lnb_ref[...])
    x2_ref[...] = x2
    x2b_ref[...] = x2.astype(BF16)


def _combine(table, route, x1, y_disp, lng, lnb, group, alpha):
    n, d = x1.shape
    nsteps = n // group
    prows = _group_rows(group)
    return pl.pallas_call(
        functools.partial(_combine_kernel, group=group, alpha=alpha, nsteps=nsteps, prows=prows),
        grid=(nsteps,),
        in_specs=[pl.BlockSpec((CHUNK_TABLE,), lambda i: (i,), memory_space=pltpu.SMEM),
                  pl.BlockSpec((CHUNK_TABLE,), lambda i: (jnp.minimum(i + 1, nsteps - 1),),
                               memory_space=pltpu.SMEM),
                  pl.BlockSpec((group, LANES), lambda i: (i, 0)),
                  pl.BlockSpec((group, d), lambda i: (i, 0)),
                  pl.BlockSpec(memory_space=pl.ANY),
                  _const_spec((1, d)), _const_spec((1, d))],
        out_specs=[pl.BlockSpec((group, d), lambda i: (i, 0)), pl.BlockSpec((group, d), lambda i: (i, 0))],
        out_shape=[jax.ShapeDtypeStruct((n, d), F32), jax.ShapeDtypeStruct((n, d), BF16)],
        scratch_shapes=[pltpu.VMEM((2, prows, d), F32), pltpu.SemaphoreType.DMA((2,))],
        compiler_params=_params("arbitrary"),
        name="moe_combine",
    )(table, table, route, x1, y_disp, lng.reshape(1, d), lnb.reshape(1, d))


def _tiles(bsz, seq):
    n = bsz * seq
    return dict(
        ln=min(1024, n),
        mlstm=min(512, seq), mlstm_chunk=min(128, seq),
        rwkv_prep=min(512, seq),
        scan_steps=min(32, seq),
        xattn=min(512, seq),
        merge=min(512, seq),
        moe_group=min(256, seq),
        moe_block=512,
    )


def _pad_cols(w, width):
    return jnp.pad(w, ((0, 0), (0, width - w.shape[1])))


def kernel(x, mem, ln_in_g, ln_in_b, mem_ln_g, mem_ln_b, w_in, ml_conv_w, ml_conv_b, ml_ig_b, ml_fg_b, ml_norm_g, rw_mu, rw_w0, rw_w_up, rw_a0, rw_a_up, rw_g_up, rw_kk, rw_ka, rw_rk, rw_ln_g, rw_ln_b, ca_w_kv, gate_b, w_br_ml, w_br_rw, w_br_ca, w_o, ln1_g, ln1_b, router_w, router_b, w_gu, b_gu, w_dn, b_dn, ln2_g, ln2_b):
    bsz, seq, d = x.shape
    mem_len = mem.shape[1]
    depth = w_in.shape[0]
    n = bsz * seq
    t = _tiles(bsz, seq)
    alpha = (2 * depth) ** 0.25
    d_ff = w_dn.shape[2]

    xf, xb = _ln_rows(x.reshape(n, d), ln_in_g, ln_in_b, t["ln"])
    _, memb = _ln_rows(mem.reshape(bsz * mem_len, d), mem_ln_g, mem_ln_b, min(t["ln"], bsz * mem_len))

    o_qk, o_v, o_og = 0, ML_QK_W, ML_QK_W + ML_W
    o_ig = o_og + ML_W
    o_fg = o_ig + ML_HEADS
    o_rw = o_fg + ML_HEADS
    o_ca = o_rw + 3 * RW_W + RW_DECAY_LORA + RW_AAA_LORA + RW_GATE_LORA
    o_gate = o_ca + CA_W

    head_of_lane = jnp.arange(RW_W) // RW_DH
    red = (head_of_lane[:, None] == jnp.arange(LANES)[None, :]).astype(BF16)
    bc = red.T
    n_asg = n * TOP_K
    block = t["moe_block"]
    group = t["moe_group"]
    n_groups = n // group
    n_chunk = _group_rows(group) // MOE_CHUNK
    assert n_chunk < CHUNK_TABLE and block % MOE_CHUNK == 0
    n_blocks = -(-(n_asg + n_groups * N_EXPERTS * (MOE_CHUNK - 1)) // block) + N_EXPERTS
    n_rows = n_blocks * block

    for l in range(depth):
        w = w_in[l]
        o_wd = o_rw + 3 * RW_W
        o_ad = o_wd + RW_DECAY_LORA
        o_gd = o_ad + RW_AAA_LORA
        wrw = jnp.concatenate([w[:, o_rw:o_wd], _pad_cols(w[:, o_wd:o_ad], LANES),
                               _pad_cols(w[:, o_ad:o_gd], LANES), w[:, o_gd:o_ca]], axis=1).astype(BF16)
        mu = rw_mu[l]
        mu_p = jnp.concatenate([mu[:3 * RW_W], jnp.pad(mu[3 * RW_W:3 * RW_W + RW_DECAY_LORA], (0, LANES - RW_DECAY_LORA)),
                                jnp.pad(mu[3 * RW_W + RW_DECAY_LORA:3 * RW_W + RW_DECAY_LORA + RW_AAA_LORA],
                                        (0, LANES - RW_AAA_LORA)),
                                mu[3 * RW_W + RW_DECAY_LORA + RW_AAA_LORA:]])[None, :]
        wup = jnp.pad(rw_w_up[l], ((0, LANES - RW_DECAY_LORA), (0, 0))).astype(BF16)
        aup = jnp.pad(rw_a_up[l], ((0, LANES - RW_AAA_LORA), (0, 0))).astype(BF16)
        scan_in, g, bonus = _rwkv_prep(
            xb.reshape(bsz, seq, d), wrw, mu_p, rw_w0[l][None, :], wup, rw_a0[l][None, :], aup,
            rw_g_up[l].astype(BF16), rw_kk[l][None, :], rw_ka[l][None, :], rw_rk[l].reshape(1, RW_W), red, bc,
            t["rwkv_prep"])

        nl = bsz * RW_HEADS
        nl_pad = -(-nl // LANES) * LANES
        ops = scan_in.reshape(SCAN_OPERANDS, seq, nl, RW_DH).transpose(0, 1, 3, 2)
        if nl_pad != nl:
            ops = jnp.pad(ops, ((0, 0), (0, 0), (0, 0), (0, nl_pad - nl)))

        wml = w[:, o_qk:o_ig].astype(BF16)
        wg = _pad_cols(w[:, o_ig:o_rw], LANES).astype(BF16)
        gb = _pad_cols(jnp.concatenate([ml_ig_b[l], ml_fg_b[l]])[None, :], LANES)
        h_ml = _mlstm(xb.reshape(bsz, seq, d), wml, wg, ml_conv_w[l], ml_conv_b[l][None, :], gb,
                      ml_norm_g[l][None, :], t["mlstm"], t["mlstm_chunk"], after=(g,))

        kv = _matmul(memb, ca_w_kv[l].astype(BF16), min(512, bsz * mem_len)).reshape(bsz, mem_len, 2 * CA_W)
        kt = kv[:, :, :CA_W].transpose(0, 2, 1).astype(BF16)
        vm = kv[:, :, CA_W:].astype(BF16)
        h_ca = _xattn(xb.reshape(bsz, seq, d), w[:, o_ca:o_gate].astype(BF16), kt, vm, t["xattn"])

        o_scan = _rwkv_scan(ops, t["scan_steps"], after=(h_ml, h_ca))[:, :, :nl]
        o_rwkv = o_scan.transpose(0, 2, 1).reshape(seq, bsz * RW_W)

        rw_pad = _pad_cols(router_w[l], LANES)
        rw_hi = rw_pad.astype(BF16)
        rw_lo = (rw_pad - rw_hi.astype(F32)).astype(BF16)
        rb = jnp.concatenate([router_b[l], jnp.full((LANES - N_EXPERTS,), NEG_BIG, F32)])[None, :]
        consts = (w[:, o_gate:].astype(BF16), gate_b[l][None, :], red, bc, rw_ln_g[l][None, :], rw_ln_b[l][None, :],
                  w_br_ml[l].astype(BF16), w_br_rw[l].astype(BF16), w_br_ca[l].astype(BF16), w_o[l].astype(BF16),
                  ln1_g[l][None, :], ln1_b[l][None, :], rw_hi, rw_lo, rb)
        x1, x1b, route, counts = _merge(xf, xb, h_ml.reshape(n, ML_W), o_rwkv, bonus.reshape(n, RW_W),
                                        g.reshape(n, RW_W), h_ca.reshape(n, CA_W), consts, t["merge"], group, alpha)

        cnt = counts[:, 0, :N_EXPERTS].astype(jnp.int32)
        run = (cnt + MOE_CHUNK - 1) // MOE_CHUNK * MOE_CHUNK
        run_off = jnp.cumsum(run, axis=1) - run
        group_rows = jnp.sum(run, axis=1)
        blocks_per = (jnp.sum(run, axis=0) + block - 1) // block
        blk_end = jnp.cumsum(blocks_per)
        slot_start = (blk_end - blocks_per) * block
        run_slot = slot_start[None, :] + jnp.cumsum(run, axis=0) - run
        chunk_row = (jnp.arange(n_chunk, dtype=jnp.int32) * MOE_CHUNK)[None, :, None]
        in_run = jnp.logical_and(chunk_row >= run_off[:, None, :], chunk_row < (run_off + run)[:, None, :])
        chunk_slot = chunk_row[:, :, 0] + jnp.sum(jnp.where(in_run, (run_slot - run_off)[:, None, :], 0), axis=2)
        chunk_slot = jnp.where(chunk_row[:, :, 0] < group_rows[:, None], chunk_slot, 0)
        table = jnp.zeros((n_groups, CHUNK_TABLE), jnp.int32)
        table = table.at[:, :n_chunk].set(chunk_slot).at[:, CHUNK_TABLE - 1].set(group_rows // MOE_CHUNK)
        table = table.reshape(n_groups * CHUNK_TABLE)
        blk_ids = jnp.arange(n_blocks, dtype=jnp.int32)
        blk_e = jnp.minimum(jnp.sum((blk_ids[:, None] >= blk_end[None, :]).astype(jnp.int32), axis=1),
                            N_EXPERTS - 1)
        nused = blk_end[-1:].astype(jnp.int32)

        last_blk = jnp.maximum(blk_end - 1, 0).astype(jnp.int32)
        trailing = jnp.minimum(nused[0] + jnp.arange(n_blocks - n_asg // block, dtype=jnp.int32), n_blocks - 1)
        zero_blk = jnp.concatenate([last_blk, trailing])
        xd = _dispatch(table, zero_blk, route, x1b, n_rows, group, block)
        w_glu, w_lin = _deinterleave(w_gu, l)
        y_disp = _experts(blk_e, nused, xd, w_glu, w_lin,
                          b_gu[l][:, None, 0::2], b_gu[l][:, None, 1::2], w_dn[l].astype(BF16),
                          b_dn[l][:, None, :], block)
        xf, xb = _combine(table, route, x1, y_disp, ln2_g[l], ln2_b[l], group, alpha)

    del d_ff
    return xf.reshape(bsz, seq, d)
```

```python
import functools

import jax
import jax.numpy as jnp
from jax import lax
from jax.experimental import pallas as pl
from jax.experimental.pallas import tpu as pltpu

ML_HEADS, ML_DQK, ML_DV, ML_CONV = 4, 64, 128, 4
ML_W = ML_HEADS * ML_DV
ML_QK_W = 2 * ML_HEADS * ML_DQK
RW_HEADS, RW_DH = 8, 64
RW_W = RW_HEADS * RW_DH
RW_DECAY_LORA, RW_AAA_LORA, RW_GATE_LORA = 64, 64, 128
RW_GN_EPS = 64e-5
CA_HEADS, CA_DH = 4, 128
CA_W = CA_HEADS * CA_DH
N_BRANCH = 3
N_EXPERTS, TOP_K = 32, 4
SWIGLU_LIMIT, SWIGLU_ALPHA = 7.0, 1.702
LN_EPS = 1e-5

LANES = 128
SUBLANES = 8
VMEM_LIMIT = 56 * 1024 * 1024

BF16 = jnp.bfloat16
F32 = jnp.float32
NEG_BIG = -1e30


def _dot(a, b):
    return jnp.dot(a.astype(BF16), b.astype(BF16), preferred_element_type=F32)


def _split3(a):
    hi = a.astype(BF16)
    r1 = a - hi.astype(F32)
    mid = r1.astype(BF16)
    lo = (r1 - mid.astype(F32)).astype(BF16)
    return hi, mid, lo


def _dot2_rhs(a, b_bf16):
    hi = a.astype(BF16)
    lo = (a - hi.astype(F32)).astype(BF16)
    d = functools.partial(jnp.dot, preferred_element_type=F32)
    return d(hi, b_bf16) + d(lo, b_bf16)


def _head_sum(x, red, bc):
    return _dot2_rhs(_dot2_rhs(x, red), bc)


def _dot_exact_lhs(a_bf16, b):
    hi, mid, lo = _split3(b)
    d = functools.partial(jnp.dot, preferred_element_type=F32)
    return d(a_bf16, hi) + d(a_bf16, mid) + d(a_bf16, lo)


def _sigmoid(x):
    return 1.0 / (1.0 + jnp.exp(-x))


def _softplus(x):
    return jnp.maximum(x, 0.0) + jnp.log1p(jnp.exp(-jnp.abs(x)))


def _layer_norm(v, g, b, eps=LN_EPS):
    mu = jnp.mean(v, axis=-1, keepdims=True)
    c = v - mu
    var = jnp.mean(c * c, axis=-1, keepdims=True)
    return c * lax.rsqrt(var + eps) * g + b


def _params(*sem):
    return pltpu.CompilerParams(dimension_semantics=sem, vmem_limit_bytes=VMEM_LIMIT)


def _ordered_after(kernel, first, count):
    def wrapped(*refs):
        return kernel(*refs[:first], *refs[first + count:])
    return wrapped


def _after_specs(after):
    return [pl.BlockSpec(memory_space=pl.ANY)] * len(after)


def _const_spec(shape):
    nd = len(shape)
    return pl.BlockSpec(shape, lambda *_: (0,) * nd, pipeline_mode=pl.Buffered(1))


def _ln_kernel(x_ref, g_ref, b_ref, o_ref, ob_ref):
    y = _layer_norm(x_ref[...], g_ref[...], b_ref[...])
    o_ref[...] = y
    ob_ref[...] = y.astype(BF16)


def _ln_rows(x2d, g, b, tile):
    n, d = x2d.shape
    return pl.pallas_call(
        _ln_kernel,
        grid=(n // tile,),
        in_specs=[pl.BlockSpec((tile, d), lambda i: (i, 0)), _const_spec((1, d)), _const_spec((1, d))],
        out_specs=[pl.BlockSpec((tile, d), lambda i: (i, 0)), pl.BlockSpec((tile, d), lambda i: (i, 0))],
        out_shape=[jax.ShapeDtypeStruct((n, d), F32), jax.ShapeDtypeStruct((n, d), BF16)],
        compiler_params=_params("parallel"),
        name="ln_rows",
    )(x2d, g.reshape(1, d), b.reshape(1, d))


def _mm_kernel(a_ref, b_ref, o_ref):
    o_ref[...] = jnp.dot(a_ref[...], b_ref[...], preferred_element_type=F32)


def _matmul(a, b, tile):
    m, k = a.shape
    n = b.shape[1]
    return pl.pallas_call(
        _mm_kernel,
        grid=(m // tile,),
        in_specs=[pl.BlockSpec((tile, k), lambda i: (i, 0)), _const_spec((k, n))],
        out_specs=pl.BlockSpec((tile, n), lambda i: (i, 0)),
        out_shape=jax.ShapeDtypeStruct((m, n), F32),
        compiler_params=_params("parallel"),
        name="matmul",
    )(a, b)


def _mlstm_kernel(xb_ref, wml_ref, wg_ref, convw_ref, convb_ref, gb_ref, normg_ref, tri_ref,
                  h_ref, ubuf, c_ref, n_ref, m_ref, *, tile, chunk):
    @pl.when(pl.program_id(1) == 0)
    def _():
        ubuf[0:SUBLANES, :] = jnp.zeros((SUBLANES, ML_QK_W), F32)
        c_ref[...] = jnp.zeros_like(c_ref)
        n_ref[...] = jnp.zeros_like(n_ref)
        m_ref[...] = jnp.zeros_like(m_ref)

    xb = xb_ref[...]
    u = jnp.dot(xb, wml_ref[...], preferred_element_type=F32)
    gates = jnp.dot(xb, wg_ref[...], preferred_element_type=F32) + gb_ref[...]

    ubuf[SUBLANES:SUBLANES + tile, :] = u[:, :ML_QK_W]
    acc = jnp.broadcast_to(convb_ref[...], (tile, ML_QK_W))
    for j in range(ML_CONV):
        acc = acc + convw_ref[j:j + 1, :] * ubuf[pl.ds(SUBLANES - ML_CONV + 1 + j, tile), :]
    ubuf[0:SUBLANES, :] = ubuf[tile:tile + SUBLANES, :]
    qk = acc * _sigmoid(acc)
    q = qk[:, :ML_HEADS * ML_DQK] * (ML_DQK ** -0.5)
    k = qk[:, ML_HEADS * ML_DQK:]
    v = u[:, ML_QK_W:ML_QK_W + ML_W]
    og = u[:, ML_QK_W + ML_W:]

    lane = lax.broadcasted_iota(jnp.int32, (chunk, LANES), 1)
    row = lax.broadcasted_iota(jnp.int32, (chunk, chunk), 0)
    col = lax.broadcasted_iota(jnp.int32, (chunk, chunk), 1)
    causal = row >= col
    log_f = -_softplus(-gates)

    for c in range(tile // chunk):
        rs = slice(c * chunk, (c + 1) * chunk)
        bcum = _dot_exact_lhs(tri_ref[...], log_f[rs])
        mcol = jnp.where(lane < ML_HEADS, gates[rs], bcum)
        mrow = mcol.T
        for h in range(ML_HEADS):
            i_col = mcol[:, h:h + 1]
            b_col = mcol[:, ML_HEADS + h:ML_HEADS + h + 1]
            i_row = mrow[h:h + 1, :]
            b_row = mrow[ML_HEADS + h:ML_HEADS + h + 1, :]
            m_prev = m_ref[h][0:1, 0:1]
            n_prev = n_ref[h][0:1, :]
            c_prev = c_ref[h]
            qh = q[rs, h * ML_DQK:(h + 1) * ML_DQK]
            kh = k[rs, h * ML_DQK:(h + 1) * ML_DQK]
            vh = v[rs, h * ML_DV:(h + 1) * ML_DV]

            dmat = jnp.where(causal, b_col - b_row + i_row, -jnp.inf)
            m_inter = b_col + m_prev
            m_t = jnp.maximum(m_inter, jnp.max(dmat, axis=1, keepdims=True))
            s = lax.dot_general(qh.astype(BF16), kh.astype(BF16), (((1,), (1,)), ((), ())),
                                preferred_element_type=F32)
            w_intra = jnp.exp(dmat - m_t) * s
            s_inter = jnp.exp(m_inter - m_t)
            num = s_inter * _dot(qh, c_prev) + _dot(w_intra, vh)
            den = (s_inter * jnp.sum(qh * n_prev, axis=1, keepdims=True)
                   + jnp.sum(w_intra, axis=1, keepdims=True))
            hh = num / jnp.maximum(jnp.abs(den), jnp.exp(-m_t))

            b_last = b_col[chunk - 1:chunk, :]
            g_col = b_last - b_col + i_col
            m_new = jnp.maximum(b_last + m_prev, jnp.max(g_col, axis=0, keepdims=True))
            carry = jnp.exp(b_last + m_prev - m_new)
            kw = kh * jnp.exp(g_col - m_new)
            c_ref[h] = carry * c_prev + lax.dot_general(
                kw.astype(BF16), vh.astype(BF16), (((0,), (0,)), ((), ())), preferred_element_type=F32)
            n_new = carry * n_prev + jnp.sum(kw, axis=0, keepdims=True)
            n_ref[h] = jnp.broadcast_to(n_new, (SUBLANES, ML_DQK))
            m_ref[h] = jnp.broadcast_to(m_new, (SUBLANES, LANES))

            mu = jnp.mean(hh, axis=1, keepdims=True)
            cen = hh - mu
            var = jnp.mean(cen * cen, axis=1, keepdims=True)
            y = cen * lax.rsqrt(var + LN_EPS) * normg_ref[:, h * ML_DV:(h + 1) * ML_DV]
            h_ref[rs, h * ML_DV:(h + 1) * ML_DV] = (
                _sigmoid(og[rs, h * ML_DV:(h + 1) * ML_DV]) * y).astype(BF16)


def _mlstm(xb, wml, wg, convw, convb, gb, normg, tile, chunk, after=()):
    bsz, s, d = xb.shape
    tri = (jnp.arange(chunk)[:, None] >= jnp.arange(chunk)[None, :]).astype(BF16)
    kern = _ordered_after(functools.partial(_mlstm_kernel, tile=tile, chunk=chunk), 8, len(after))
    return pl.pallas_call(
        kern,
        grid=(bsz, s // tile),
        in_specs=[pl.BlockSpec((None, tile, d), lambda b, j: (b, j, 0)),
                  _const_spec(wml.shape), _const_spec(wg.shape), _const_spec(convw.shape),
                  _const_spec(convb.shape), _const_spec(gb.shape), _const_spec(normg.shape),
                  _const_spec(tri.shape)] + _after_specs(after),
        out_specs=pl.BlockSpec((None, tile, ML_W), lambda b, j: (b, j, 0)),
        out_shape=jax.ShapeDtypeStruct((bsz, s, ML_W), BF16),
        scratch_shapes=[pltpu.VMEM((tile + SUBLANES, ML_QK_W), F32),
                        pltpu.VMEM((ML_HEADS, ML_DQK, ML_DV), F32),
                        pltpu.VMEM((ML_HEADS, SUBLANES, ML_DQK), F32),
                        pltpu.VMEM((ML_HEADS, SUBLANES, LANES), F32)],
        compiler_params=_params("parallel", "arbitrary"),
        name="mlstm",
    )(xb, wml, wg, convw, convb, gb, normg, tri, *after)


RW_PAD_IN = 3 * RW_W + 3 * LANES
SCAN_R, SCAN_W, SCAN_K, SCAN_V, SCAN_KK, SCAN_KKA = range(6)
SCAN_OPERANDS = 6


def _rwkv_prep_kernel(xb_ref, wrw_ref, mu_ref, w0_ref, wup_ref, a0_ref, aup_ref, gup_ref, kkw_ref,
                      ka_ref, rk_ref, red_ref, bc_ref,
                      scan_ref, g_ref, bonus_ref, ubuf, *, tile):
    @pl.when(pl.program_id(1) == 0)
    def _():
        ubuf[0:SUBLANES, :] = jnp.zeros((SUBLANES, RW_PAD_IN), F32)

    u = jnp.dot(xb_ref[...], wrw_ref[...], preferred_element_type=F32)
    ubuf[SUBLANES:SUBLANES + tile, :] = u
    u_prev = ubuf[pl.ds(SUBLANES - 1, tile), :]
    ubuf[0:SUBLANES, :] = ubuf[tile:tile + SUBLANES, :]
    us = u + (u_prev - u) * mu_ref[...]
    r = us[:, 0:RW_W]
    kr = us[:, RW_W:2 * RW_W]
    vr = us[:, 2 * RW_W:3 * RW_W]
    wd = us[:, 3 * RW_W:3 * RW_W + LANES]
    ad = us[:, 3 * RW_W + LANES:3 * RW_W + 2 * LANES]
    gd = us[:, 3 * RW_W + 2 * LANES:]

    w_log = -_softplus(-(w0_ref[...] + _dot(jnp.tanh(wd), wup_ref[...]))) - 0.5
    decay = jnp.exp(-jnp.exp(w_log))
    a = _sigmoid(a0_ref[...] + _dot(ad, aup_ref[...]))
    g = _dot(_sigmoid(gd), gup_ref[...])

    red, bc = red_ref[...], bc_ref[...]
    kk = kr * kkw_ref[...]
    kk = kk * lax.rsqrt(jnp.maximum(_head_sum(kk * kk, red, bc), 1e-24))
    k2 = kr * (1.0 + (a - 1.0) * ka_ref[...])
    bonus = _head_sum(r * k2 * rk_ref[...], red, bc) * vr

    scan_ref[SCAN_R] = r
    scan_ref[SCAN_W] = decay
    scan_ref[SCAN_K] = k2
    scan_ref[SCAN_V] = vr
    scan_ref[SCAN_KK] = kk
    scan_ref[SCAN_KKA] = kk * a
    g_ref[...] = g
    bonus_ref[...] = bonus


def _rwkv_prep(xb, wrw, mu, w0, wup, a0, aup, gup, kkw, ka, rk, red, bc, tile):
    bsz, s, d = xb.shape
    consts = (wrw, mu, w0, wup, a0, aup, gup, kkw, ka, rk, red, bc)
    spec = pl.BlockSpec((None, tile, RW_W), lambda b, j: (b, j, 0))
    tspec = pl.BlockSpec((SCAN_OPERANDS, tile, RW_W), lambda b, j: (0, j, b))
    return pl.pallas_call(
        functools.partial(_rwkv_prep_kernel, tile=tile),
        grid=(bsz, s // tile),
        in_specs=[pl.BlockSpec((None, tile, d), lambda b, j: (b, j, 0))] + [_const_spec(c.shape) for c in consts],
        out_specs=[tspec, spec, spec],
        out_shape=[jax.ShapeDtypeStruct((SCAN_OPERANDS, s, bsz * RW_W), F32)]
        + [jax.ShapeDtypeStruct((bsz, s, RW_W), F32)] * 2,
        scratch_shapes=[pltpu.VMEM((tile + SUBLANES, RW_PAD_IN), F32)],
        compiler_params=_params("parallel", "arbitrary"),
        name="rwkv_prep",
    )(xb, *consts)


def _rwkv_scan_kernel(x_ref, o_ref, st_ref, *, steps):
    @pl.when(pl.program_id(1) == 0)
    def _():
        st_ref[...] = jnp.zeros_like(st_ref)

    r_ref, w_ref, k_ref, v_ref, kk_ref, kka_ref = (
        x_ref.at[i] for i in (SCAN_R, SCAN_W, SCAN_K, SCAN_V, SCAN_KK, SCAN_KKA))
    zeros = jnp.zeros((RW_DH, LANES), F32)

    def first(kc, sa):
        return sa + st_ref[kc] * kk_ref[0, pl.ds(kc, 1), :]

    sa0 = lax.fori_loop(0, RW_DH, first, zeros, unroll=8)

    def step(t, sa):
        vt = v_ref[t]
        tn = jnp.minimum(t + 1, steps - 1)

        def body(kc, carry):
            out, sa_next = carry
            row = pl.ds(kc, 1)
            new = st_ref[kc] * w_ref[t, row, :] - sa * kka_ref[t, row, :] + vt * k_ref[t, row, :]
            st_ref[kc] = new
            return out + new * r_ref[t, row, :], sa_next + new * kk_ref[tn, row, :]

        out, sa_next = lax.fori_loop(0, RW_DH, body, (zeros, zeros), unroll=16)
        o_ref[t] = out
        return sa_next

    lax.fori_loop(0, steps, step, sa0)


def _rwkv_scan(x, steps, after=()):
    _, s, dh, nl = x.shape
    spec = pl.BlockSpec((steps, dh, LANES), lambda g, j: (j, 0, g))
    return pl.pallas_call(
        _ordered_after(functools.partial(_rwkv_scan_kernel, steps=steps), 1, len(after)),
        grid=(nl // LANES, s // steps),
        in_specs=[pl.BlockSpec((SCAN_OPERANDS, steps, dh, LANES), lambda g, j: (0, j, 0, g))]
        + _after_specs(after),
        out_specs=spec,
        out_shape=jax.ShapeDtypeStruct((s, dh, nl), F32),
        scratch_shapes=[pltpu.VMEM((dh, dh, LANES), F32)],
        compiler_params=_params("parallel", "arbitrary"),
        name="rwkv_scan",
    )(x, *after)


def _xattn_kernel(xb_ref, wq_ref, kt_ref, v_ref, o_ref):
    q = jnp.dot(xb_ref[...], wq_ref[...], preferred_element_type=F32)
    for h in range(CA_HEADS):
        hs = slice(h * CA_DH, (h + 1) * CA_DH)
        s = jnp.dot(q[:, hs].astype(BF16), kt_ref[hs, :], preferred_element_type=F32) * (CA_DH ** -0.5)
        p = jnp.exp(s - jnp.max(s, axis=1, keepdims=True))
        den = jnp.sum(p, axis=1, keepdims=True)
        o = jnp.dot(p.astype(BF16), v_ref[:, hs], preferred_element_type=F32) / den
        o_ref[:, hs] = o.astype(BF16)


def _xattn(xb, wq, kt, v, tile, after=()):
    bsz, s, d = xb.shape
    m = v.shape[1]
    return pl.pallas_call(
        _ordered_after(_xattn_kernel, 4, len(after)),
        grid=(bsz, s // tile),
        in_specs=[pl.BlockSpec((None, tile, d), lambda b, j: (b, j, 0)), _const_spec(wq.shape),
                  pl.BlockSpec((None, CA_W, m), lambda b, j: (b, 0, 0)),
                  pl.BlockSpec((None, m, CA_W), lambda b, j: (b, 0, 0))] + _after_specs(after),
        out_specs=pl.BlockSpec((None, tile, CA_W), lambda b, j: (b, j, 0)),
        out_shape=jax.ShapeDtypeStruct((bsz, s, CA_W), BF16),
        compiler_params=_params("parallel", "parallel"),
        name="xattn",
    )(xb, wq, kt, v, *after)


ROUTE_IDX, ROUTE_POS, ROUTE_GATE = 0, TOP_K, 2 * TOP_K


def _merge_kernel(x_ref, xb_ref, hml_ref, o_ref, bonus_ref, g_ref, hca_ref,
                  wgate_ref, gateb_ref, red_ref, bc_ref, rwg_ref, rwb_ref, wml_ref, wrw_ref, wca_ref, wo_ref,
                  lng_ref, lnb_ref, rwhi_ref, rwlo_ref, rb_ref, tri_ref, upper_ref,
                  x1_ref, x1b_ref, route_ref, cnt_ref, *, tile, group, alpha, d_model):
    red, bc = red_ref[...], bc_ref[...]
    o = o_ref[...]
    mu = _head_sum(o, red, bc) * (1.0 / RW_DH)
    cen = o - mu
    var = _head_sum(cen * cen, red, bc) * (1.0 / RW_DH)
    h_rw = (cen * lax.rsqrt(var + RW_GN_EPS) * rwg_ref[...] + rwb_ref[...] + bonus_ref[...]) * g_ref[...]

    xb = xb_ref[...]
    y = None
    for br, (h_br, w_ref) in enumerate(((hml_ref[...], wml_ref), (h_rw.astype(BF16), wrw_ref),
                                        (hca_ref[...], wca_ref))):
        cols = slice(br * d_model, (br + 1) * d_model)
        gate = _sigmoid(jnp.dot(xb, wgate_ref[:, cols], preferred_element_type=F32) + gateb_ref[:, cols])
        term = gate * jnp.dot(h_br, w_ref[...], preferred_element_type=F32)
        y = term if y is None else y + term
    mixed = _dot(y, wo_ref[...])
    x1 = _layer_norm(alpha * x_ref[...] + mixed, lng_ref[...], lnb_ref[...])
    x1_ref[...] = x1
    x1b_ref[...] = x1.astype(BF16)

    hi = x1.astype(BF16)
    lo = (x1 - hi.astype(F32)).astype(BF16)
    d = functools.partial(jnp.dot, preferred_element_type=F32)
    logits = d(hi, rwhi_ref[...]) + d(lo, rwhi_ref[...]) + d(hi, rwlo_ref[...]) + rb_ref[...]

    lane = lax.broadcasted_iota(jnp.int32, (tile, LANES), 1)
    vals = logits
    tops, onehots = [], []
    for _ in range(TOP_K):
        m = jnp.max(vals, axis=1, keepdims=True)
        idx = jnp.min(jnp.where(vals == m, lane, LANES), axis=1, keepdims=True)
        sel = lane == idx
        tops.append((m, idx))
        onehots.append(sel.astype(F32))
        vals = jnp.where(sel, -jnp.inf, vals)
    exps = [jnp.exp(m - tops[0][0]) for m, _ in tops]
    den = exps[0] + exps[1] + exps[2] + exps[3]

    cnt = onehots[0] + onehots[1] + onehots[2] + onehots[3]
    starts = []
    for gi in range(tile // group):
        cg = cnt[gi * group:(gi + 1) * group]
        total = jnp.sum(cg, axis=0, keepdims=True)
        run = jnp.floor((total + (MOE_CHUNK - 1)) * (1.0 / MOE_CHUNK)) * MOE_CHUNK
        run_off = jnp.dot(jnp.broadcast_to(run, (SUBLANES, LANES)).astype(BF16), upper_ref[...],
                          preferred_element_type=F32)[0:1, :]
        starts.append(jnp.dot(tri_ref[...], cg.astype(BF16), preferred_element_type=F32) + run_off)
        cnt_ref[gi] = jnp.broadcast_to(total, (SUBLANES, LANES))
    start = starts[0] if len(starts) == 1 else jnp.concatenate(starts, axis=0)
    route = jnp.zeros((tile, LANES), F32)
    for kq in range(TOP_K):
        pos = jnp.sum(onehots[kq] * start, axis=1, keepdims=True)
        route = jnp.where(lane == ROUTE_IDX + kq, tops[kq][1].astype(F32), route)
        route = jnp.where(lane == ROUTE_POS + kq, pos, route)
        route = jnp.where(lane == ROUTE_GATE + kq, exps[kq] / den, route)
    route_ref[...] = route


def _merge(x, xb, hml, o_tm, bonus, g, hca, consts, tile, group, alpha):
    n, d = x.shape
    tiles_per_seq = o_tm.shape[0] // tile
    gpt = tile // group
    tri = (jnp.arange(group)[:, None] > jnp.arange(group)[None, :]).astype(BF16)
    upper = (jnp.arange(LANES)[:, None] < jnp.arange(LANES)[None, :]).astype(BF16)
    consts = tuple(consts) + (tri, upper)

    def rows(w):
        return pl.BlockSpec((tile, w), lambda i: (i, 0))

    o_spec = pl.BlockSpec((tile, RW_W), lambda i: (i % tiles_per_seq, i // tiles_per_seq))
    return pl.pallas_call(
        functools.partial(_merge_kernel, tile=tile, group=group, alpha=alpha, d_model=d),
        grid=(n // tile,),
        in_specs=[rows(d), rows(d), rows(ML_W), o_spec, rows(RW_W), rows(RW_W), rows(CA_W)]
        + [_const_spec(c.shape) for c in consts],
        out_specs=[rows(d), rows(d), rows(LANES),
                   pl.BlockSpec((gpt, SUBLANES, LANES), lambda i: (i, 0, 0))],
        out_shape=[jax.ShapeDtypeStruct((n, d), F32), jax.ShapeDtypeStruct((n, d), BF16),
                   jax.ShapeDtypeStruct((n, LANES), F32),
                   jax.ShapeDtypeStruct((n // group, SUBLANES, LANES), F32)],
        compiler_params=_params("parallel"),
        name="merge_route",
    )(x, xb, hml, o_tm, bonus, g, hca, *consts)


MOE_CHUNK = SUBLANES
CHUNK_TABLE = 1024


def _group_rows(group):
    return -(-(group * TOP_K + N_EXPERTS * (MOE_CHUNK - 1)) // LANES) * LANES


def _dispatch_kernel(tab_ref, zero_blk_ref, route_ref, xb_ref, xd_ref, xs_buf, zbuf, live_ref, sems, zsem,
                     *, group, nsteps, block, prows):
    i = pl.program_id(0)
    slot = i % 2

    @pl.when(i == 0)
    def _():
        zbuf[...] = jnp.zeros_like(zbuf)
        n_zero = zero_blk_ref.shape[0]

        def fresh(j):
            return jnp.logical_or(j == 0, zero_blk_ref[j] != zero_blk_ref[jnp.maximum(j - 1, 0)])

        def zero_copy(j):
            return pltpu.make_async_copy(zbuf, xd_ref.at[pl.ds(zero_blk_ref[j] * block, block)], zsem)

        def start(j, carry):
            @pl.when(fresh(j))
            def _():
                zero_copy(j).start()
            return carry

        def wait(j, carry):
            @pl.when(fresh(j))
            def _():
                zero_copy(j).wait()
            return carry

        lax.fori_loop(0, n_zero, start, 0)
        lax.fori_loop(0, n_zero, wait, 0)

    def chunk_copy(c, s):
        return pltpu.make_async_copy(xs_buf.at[s, pl.ds(c * MOE_CHUNK, MOE_CHUNK)],
                                     xd_ref.at[pl.ds(pl.multiple_of(tab_ref[c], MOE_CHUNK), MOE_CHUNK)],
                                     sems.at[s])

    def retire(s):
        def wait(c, carry):
            chunk_copy(0, s).wait()
            return carry

        lax.fori_loop(0, live_ref[s], wait, 0)

    @pl.when(i >= 2)
    def _():
        retire(slot)

    route_t = route_ref[...].T
    rows = lax.broadcasted_iota(jnp.int32, (prows, group), 0).astype(F32)
    pick = jnp.zeros((prows, group), F32)
    for kq in range(TOP_K):
        pick = pick + jnp.where(rows == route_t[ROUTE_POS + kq:ROUTE_POS + kq + 1, :], 1.0, 0.0)
    xs_buf[slot] = jnp.dot(pick.astype(BF16), xb_ref[...], preferred_element_type=F32)

    live = tab_ref[CHUNK_TABLE - 1]
    live_ref[slot] = live

    def issue(c, carry):
        chunk_copy(c, slot).start()
        return carry

    lax.fori_loop(0, live, issue, 0)

    @pl.when(i == nsteps - 1)
    def _():
        retire(slot)
        if nsteps > 1:
            retire(1 - slot)


def _dispatch(table, zero_blk, route, x1b, n_rows, group, block):
    n, d = x1b.shape
    nsteps = n // group
    prows = _group_rows(group)
    return pl.pallas_call(
        functools.partial(_dispatch_kernel, group=group, nsteps=nsteps, block=block, prows=prows),
        grid=(nsteps,),
        in_specs=[pl.BlockSpec((CHUNK_TABLE,), lambda i: (i,), memory_space=pltpu.SMEM),
                  pl.BlockSpec(memory_space=pltpu.SMEM),
                  pl.BlockSpec((group, LANES), lambda i: (i, 0)),
                  pl.BlockSpec((group, d), lambda i: (i, 0))],
        out_specs=pl.BlockSpec(memory_space=pl.ANY),
        out_shape=jax.ShapeDtypeStruct((n_rows, d), F32),
        scratch_shapes=[pltpu.VMEM((2, prows, d), F32), pltpu.VMEM((block, d), F32),
                        pltpu.SMEM((2,), jnp.int32),
                        pltpu.SemaphoreType.DMA((2,)), pltpu.SemaphoreType.DMA],
        compiler_params=_params("arbitrary"),
        name="moe_dispatch",
    )(table, zero_blk, route, x1b)


DEINT_COLS = 512


def _deinterleave_kernel(w_ref, pe_ref, po_ref, glu_ref, lin_ref):
    wb = w_ref[...].astype(BF16)
    glu_ref[...] = jnp.dot(wb, pe_ref[...], preferred_element_type=F32).astype(BF16)
    lin_ref[...] = jnp.dot(wb, po_ref[...], preferred_element_type=F32).astype(BF16)


def _deinterleave(w_gu_all, layer):
    _, e, d, two_ff = w_gu_all.shape
    half = DEINT_COLS // 2
    src = jnp.arange(DEINT_COLS)[:, None]
    dst = jnp.arange(half)[None, :]
    pe = (src == 2 * dst).astype(BF16)
    po = (src == 2 * dst + 1).astype(BF16)
    out = jax.ShapeDtypeStruct((e, d, two_ff // 2), BF16)
    return pl.pallas_call(
        _deinterleave_kernel,
        grid=(e, two_ff // DEINT_COLS),
        in_specs=[pl.BlockSpec((None, None, d, DEINT_COLS), lambda i, c: (layer, i, 0, c)),
                  _const_spec(pe.shape), _const_spec(po.shape)],
        out_specs=[pl.BlockSpec((None, d, half), lambda i, c: (i, 0, c))] * 2,
        out_shape=[out, out],
        compiler_params=_params("parallel", "parallel"),
        name="deinterleave_w",
    )(w_gu_all, pe, po)


def _expert_kernel(blk_e_ref, nused_ref, xd_ref, wglu_ref, wlin_ref, bglu_ref, blin_ref, wdn_ref, bdn_ref, y_ref):
    del blk_e_ref
    live = pl.program_id(0) < nused_ref[0]

    @pl.when(live)
    def _():
        xb = xd_ref[...].astype(BF16)
        h_glu = jnp.dot(xb, wglu_ref[...], preferred_element_type=F32) + bglu_ref[...]
        h_lin = jnp.dot(xb, wlin_ref[...], preferred_element_type=F32) + blin_ref[...]
        x_glu = jnp.minimum(h_glu, SWIGLU_LIMIT)
        x_lin = jnp.clip(h_lin, -SWIGLU_LIMIT, SWIGLU_LIMIT)
        act = x_glu * _sigmoid(SWIGLU_ALPHA * x_glu) * (x_lin + 1.0)
        y_ref[...] = _dot(act, wdn_ref[...]) + bdn_ref[...]

    @pl.when(jnp.logical_not(live))
    def _():
        y_ref[...] = jnp.zeros_like(y_ref)


def _experts(blk_e, nused, xd, wglu, wlin, bglu, blin, wdn, bdn, block):
    n_rows, d = xd.shape
    dff = wglu.shape[2]
    grid_spec = pltpu.PrefetchScalarGridSpec(
        num_scalar_prefetch=2,
        grid=(n_rows // block,),
        in_specs=[pl.BlockSpec((block, d), lambda i, be, nu: (jnp.minimum(i, nu[0] - 1), 0)),
                  pl.BlockSpec((None, d, dff), lambda i, be, nu: (be[i], 0, 0)),
                  pl.BlockSpec((None, d, dff), lambda i, be, nu: (be[i], 0, 0)),
                  pl.BlockSpec((None, 1, dff), lambda i, be, nu: (be[i], 0, 0)),
                  pl.BlockSpec((None, 1, dff), lambda i, be, nu: (be[i], 0, 0)),
                  pl.BlockSpec((None, dff, d), lambda i, be, nu: (be[i], 0, 0)),
                  pl.BlockSpec((None, 1, d), lambda i, be, nu: (be[i], 0, 0))],
        out_specs=pl.BlockSpec((block, d), lambda i, be, nu: (i, 0)),
    )
    return pl.pallas_call(
        _expert_kernel,
        grid_spec=grid_spec,
        out_shape=jax.ShapeDtypeStruct((n_rows, d), F32),
        compiler_params=_params("arbitrary"),
        name="moe_experts",
    )(blk_e, nused, xd, wglu, wlin, bglu, blin, wdn, bdn)


def _combine_kernel(tab_ref, tab_next_ref, route_ref, x1_ref, y_ref, lng_ref, lnb_ref, x2_ref, x2b_ref,
                    ybuf, sems, *, group, alpha, nsteps, prows):
    i = pl.program_id(0)
    slot = i % 2
    nchunk = prows // MOE_CHUNK

    def fetch(t_ref, s):
        def issue(c, carry):
            pltpu.make_async_copy(y_ref.at[pl.ds(pl.multiple_of(t_ref[c], MOE_CHUNK), MOE_CHUNK)],
                                  ybuf.at[s, pl.ds(c * MOE_CHUNK, MOE_CHUNK)], sems.at[s]).start()
            return carry

        lax.fori_loop(0, nchunk, issue, 0, unroll=4)

    @pl.when(i == 0)
    def _():
        fetch(tab_ref, slot)

    @pl.when(i + 1 < nsteps)
    def _():
        fetch(tab_next_ref, 1 - slot)

    pltpu.make_async_copy(y_ref.at[pl.ds(0, prows)], ybuf.at[slot], sems.at[slot]).wait()
    route = route_ref[...]
    cols = lax.broadcasted_iota(jnp.int32, (group, prows), 1).astype(F32)
    weights = jnp.zeros((group, prows), F32)
    for kq in range(TOP_K):
        weights = weights + jnp.where(cols == route[:, ROUTE_POS + kq:ROUTE_POS + kq + 1],
                                      route[:, ROUTE_GATE + kq:ROUTE_GATE + kq + 1], 0.0)
    moe = jnp.dot(weights.astype(BF16), ybuf[slot].astype(BF16), preferred_element_type=F32)
    x2 = _layer_norm(alpha * x1_ref[...] + moe, lng_ref[...], lnb_ref[...])
    x2_ref[...] = x2
    x2b_ref[...] = x2.astype(BF16)


def _combine(table, route, x1, y_disp, lng, lnb, group, alpha):
    n, d = x1.shape
    nsteps = n // group
    prows = _group_rows(group)
    return pl.pallas_call(
        functools.partial(_combine_kernel, group=group, alpha=alpha, nsteps=nsteps, prows=prows),
        grid=(nsteps,),
        in_specs=[pl.BlockSpec((CHUNK_TABLE,), lambda i: (i,), memory_space=pltpu.SMEM),
                  pl.BlockSpec((CHUNK_TABLE,), lambda i: (jnp.minimum(i + 1, nsteps - 1),),
                               memory_space=pltpu.SMEM),
                  pl.BlockSpec((group, LANES), lambda i: (i, 0)),
                  pl.BlockSpec((group, d), lambda i: (i, 0)),
                  pl.BlockSpec(memory_space=pl.ANY),
                  _const_spec((1, d)), _const_spec((1, d))],
        out_specs=[pl.BlockSpec((group, d), lambda i: (i, 0)), pl.BlockSpec((group, d), lambda i: (i, 0))],
        out_shape=[jax.ShapeDtypeStruct((n, d), F32), jax.ShapeDtypeStruct((n, d), BF16)],
        scratch_shapes=[pltpu.VMEM((2, prows, d), F32), pltpu.SemaphoreType.DMA((2,))],
        compiler_params=_params("arbitrary"),
        name="moe_combine",
    )(table, table, route, x1, y_disp, lng.reshape(1, d), lnb.reshape(1, d))


def _tiles(bsz, seq):
    n = bsz * seq
    return dict(
        ln=min(1024, n),
        mlstm=min(512, seq), mlstm_chunk=min(128, seq),
        rwkv_prep=min(512, seq),
        scan_steps=min(32, seq),
        xattn=min(1024, seq),
        merge=min(512, seq),
        moe_group=min(256, seq),
        moe_block=512,
    )


def _pad_cols(w, width):
    return jnp.pad(w, ((0, 0), (0, width - w.shape[1])))


def kernel(x, mem, ln_in_g, ln_in_b, mem_ln_g, mem_ln_b, w_in, ml_conv_w, ml_conv_b, ml_ig_b, ml_fg_b, ml_norm_g, rw_mu, rw_w0, rw_w_up, rw_a0, rw_a_up, rw_g_up, rw_kk, rw_ka, rw_rk, rw_ln_g, rw_ln_b, ca_w_kv, gate_b, w_br_ml, w_br_rw, w_br_ca, w_o, ln1_g, ln1_b, router_w, router_b, w_gu, b_gu, w_dn, b_dn, ln2_g, ln2_b):
    bsz, seq, d = x.shape
    mem_len = mem.shape[1]
    depth = w_in.shape[0]
    n = bsz * seq
    t = _tiles(bsz, seq)
    alpha = (2 * depth) ** 0.25
    d_ff = w_dn.shape[2]

    xf, xb = _ln_rows(x.reshape(n, d), ln_in_g, ln_in_b, t["ln"])
    _, memb = _ln_rows(mem.reshape(bsz * mem_len, d), mem_ln_g, mem_ln_b, min(t["ln"], bsz * mem_len))

    o_qk, o_v, o_og = 0, ML_QK_W, ML_QK_W + ML_W
    o_ig = o_og + ML_W
    o_fg = o_ig + ML_HEADS
    o_rw = o_fg + ML_HEADS
    o_ca = o_rw + 3 * RW_W + RW_DECAY_LORA + RW_AAA_LORA + RW_GATE_LORA
    o_gate = o_ca + CA_W

    head_of_lane = jnp.arange(RW_W) // RW_DH
    red = (head_of_lane[:, None] == jnp.arange(LANES)[None, :]).astype(BF16)
    bc = red.T
    n_asg = n * TOP_K
    block = t["moe_block"]
    group = t["moe_group"]
    n_groups = n // group
    n_chunk = _group_rows(group) // MOE_CHUNK
    assert n_chunk < CHUNK_TABLE and block % MOE_CHUNK == 0
    n_blocks = -(-(n_asg + n_groups * N_EXPERTS * (MOE_CHUNK - 1)) // block) + N_EXPERTS
    n_rows = n_blocks * block

    for l in range(depth):
        w = w_in[l]
        o_wd = o_rw + 3 * RW_W
        o_ad = o_wd + RW_DECAY_LORA
        o_gd = o_ad + RW_AAA_LORA
        wrw = jnp.concatenate([w[:, o_rw:o_wd], _pad_cols(w[:, o_wd:o_ad], LANES),
                               _pad_cols(w[:, o_ad:o_gd], LANES), w[:, o_gd:o_ca]], axis=1).astype(BF16)
        mu = rw_mu[l]
        mu_p = jnp.concatenate([mu[:3 * RW_W], jnp.pad(mu[3 * RW_W:3 * RW_W + RW_DECAY_LORA], (0, LANES - RW_DECAY_LORA)),
                                jnp.pad(mu[3 * RW_W + RW_DECAY_LORA:3 * RW_W + RW_DECAY_LORA + RW_AAA_LORA],
                                        (0, LANES - RW_AAA_LORA)),
                                mu[3 * RW_W + RW_DECAY_LORA + RW_AAA_LORA:]])[None, :]
        wup = jnp.pad(rw_w_up[l], ((0, LANES - RW_DECAY_LORA), (0, 0))).astype(BF16)
        aup = jnp.pad(rw_a_up[l], ((0, LANES - RW_AAA_LORA), (0, 0))).astype(BF16)
        scan_in, g, bonus = _rwkv_prep(
            xb.reshape(bsz, seq, d), wrw, mu_p, rw_w0[l][None, :], wup, rw_a0[l][None, :], aup,
            rw_g_up[l].astype(BF16), rw_kk[l][None, :], rw_ka[l][None, :], rw_rk[l].reshape(1, RW_W), red, bc,
            t["rwkv_prep"])

        nl = bsz * RW_HEADS
        nl_pad = -(-nl // LANES) * LANES
        ops = scan_in.reshape(SCAN_OPERANDS, seq, nl, RW_DH).transpose(0, 1, 3, 2)
        if nl_pad != nl:
            ops = jnp.pad(ops, ((0, 0), (0, 0), (0, 0), (0, nl_pad - nl)))

        wml = w[:, o_qk:o_ig].astype(BF16)
        wg = _pad_cols(w[:, o_ig:o_rw], LANES).astype(BF16)
        gb = _pad_cols(jnp.concatenate([ml_ig_b[l], ml_fg_b[l]])[None, :], LANES)
        h_ml = _mlstm(xb.reshape(bsz, seq, d), wml, wg, ml_conv_w[l], ml_conv_b[l][None, :], gb,
                      ml_norm_g[l][None, :], t["mlstm"], t["mlstm_chunk"], after=(g,))

        kv = _matmul(memb, ca_w_kv[l].astype(BF16), min(512, bsz * mem_len)).reshape(bsz, mem_len, 2 * CA_W)
        kt = kv[:, :, :CA_W].transpose(0, 2, 1).astype(BF16)
        vm = kv[:, :, CA_W:].astype(BF16)
        h_ca = _xattn(xb.reshape(bsz, seq, d), w[:, o_ca:o_gate].astype(BF16), kt, vm, t["xattn"], after=(g,))

        o_scan = _rwkv_scan(ops, t["scan_steps"], after=(h_ml, h_ca))[:, :, :nl]
        o_rwkv = o_scan.transpose(0, 2, 1).reshape(seq, bsz * RW_W)

        rw_pad = _pad_cols(router_w[l], LANES)
        rw_hi = rw_pad.astype(BF16)
        rw_lo = (rw_pad - rw_hi.astype(F32)).astype(BF16)
        rb = jnp.concatenate([router_b[l], jnp.full((LANES - N_EXPERTS,), NEG_BIG, F32)])[None, :]
        consts = (w[:, o_gate:].astype(BF16), gate_b[l][None, :], red, bc, rw_ln_g[l][None, :], rw_ln_b[l][None, :],
                  w_br_ml[l].astype(BF16), w_br_rw[l].astype(BF16), w_br_ca[l].astype(BF16), w_o[l].astype(BF16),
                  ln1_g[l][None, :], ln1_b[l][None, :], rw_hi, rw_lo, rb)
        x1, x1b, route, counts = _merge(xf, xb, h_ml.reshape(n, ML_W), o_rwkv, bonus.reshape(n, RW_W),
                                        g.reshape(n, RW_W), h_ca.reshape(n, CA_W), consts, t["merge"], group, alpha)

        cnt = counts[:, 0, :N_EXPERTS].astype(jnp.int32)
        run = (cnt + MOE_CHUNK - 1) // MOE_CHUNK * MOE_CHUNK
        run_off = jnp.cumsum(run, axis=1) - run
        group_rows = jnp.sum(run, axis=1)
        blocks_per = (jnp.sum(run, axis=0) + block - 1) // block
        blk_end = jnp.cumsum(blocks_per)
        slot_start = (blk_end - blocks_per) * block
        run_slot = slot_start[None, :] + jnp.cumsum(run, axis=0) - run
        chunk_row = (jnp.arange(n_chunk, dtype=jnp.int32) * MOE_CHUNK)[None, :, None]
        in_run = jnp.logical_and(chunk_row >= run_off[:, None, :], chunk_row < (run_off + run)[:, None, :])
        chunk_slot = chunk_row[:, :, 0] + jnp.sum(jnp.where(in_run, (run_slot - run_off)[:, None, :], 0), axis=2)
        chunk_slot = jnp.where(chunk_row[:, :, 0] < group_rows[:, None], chunk_slot, 0)
        table = jnp.zeros((n_groups, CHUNK_TABLE), jnp.int32)
        table = table.at[:, :n_chunk].set(chunk_slot).at[:, CHUNK_TABLE - 1].set(group_rows // MOE_CHUNK)
        table = table.reshape(n_groups * CHUNK_TABLE)
        blk_ids = jnp.arange(n_blocks, dtype=jnp.int32)
        blk_e = jnp.minimum(jnp.sum((blk_ids[:, None] >= blk_end[None, :]).astype(jnp.int32), axis=1),
                            N_EXPERTS - 1)
        nused = blk_end[-1:].astype(jnp.int32)

        last_blk = jnp.maximum(blk_end - 1, 0).astype(jnp.int32)
        trailing = jnp.minimum(nused[0] + jnp.arange(n_blocks - n_asg // block, dtype=jnp.int32), n_blocks - 1)
        zero_blk = jnp.concatenate([last_blk, trailing])
        xd = _dispatch(table, zero_blk, route, x1b, n_rows, group, block)
        w_glu, w_lin = _deinterleave(w_gu, l)
        y_disp = _experts(blk_e, nused, xd, w_glu, w_lin,
                          b_gu[l][:, None, 0::2], b_gu[l][:, None, 1::2], w_dn[l].astype(BF16),
                          b_dn[l][:, None, :], block)
        xf, xb = _combine(table, route, x1, y_disp, ln2_g[l], ln2_b[l], group, alpha)

    del d_ff
    return xf.reshape(bsz, seq, d)
```
